```python
import math
import jax, jax.numpy as jnp
from jax import lax
import numpy as np


D_MODEL = 1024
BATCH = 8
SEQ = 4096
DEPTH = 1

RMS_EPS = 1e-6
MASK_VALUE = -1e30

NSA_HEADS = 8
NSA_KV_GROUPS = 2
NSA_HEAD_DIM = 64
NSA_CMP_BLOCK = 32
NSA_CMP_STRIDE = 16
NSA_CMP_HIDDEN = 128
NSA_SEL_BLOCK = 64
NSA_TOP_N = 16
NSA_WINDOW = 512
NSA_Q_BLOCK = 64
NSA_FORCE_BONUS = 1e3
NSA_Q_W = NSA_HEADS * NSA_HEAD_DIM
NSA_KV_W = NSA_KV_GROUPS * NSA_HEAD_DIM
NSA_GATE_W = NSA_HEADS * 3

SSM_HEADS = 16
SSM_HEAD_DIM = 64
SSM_INNER = SSM_HEADS * SSM_HEAD_DIM
SSM_GROUPS = 2
SSM_STATE = 128
SSM_CONV = 4
SSM_CHUNK = 256
SSM_CONV_DIM = SSM_INNER + 2 * SSM_GROUPS * SSM_STATE
SSM_DT_MIN = 0.001
SSM_DT_MAX = 0.1

MOE_GROUPS = 4
MOE_EXPERTS_PER_GROUP = 8
MOE_EXPERTS = MOE_GROUPS * MOE_EXPERTS_PER_GROUP
MOE_TOP_K = 2
MOE_HIDDEN = 256
MOE_ROW_BLOCK = 256

PLE_DIM = 256

IN_SIZES = (NSA_Q_W, NSA_KV_W, NSA_KV_W, NSA_KV_W, NSA_KV_W, NSA_KV_W, NSA_KV_W, NSA_GATE_W,
            SSM_INNER, SSM_CONV_DIM, SSM_HEADS, 2 * D_MODEL)
IN_W = sum(IN_SIZES)

kernel_name = 'nsa_ssd_hier_moe_hybrid_block'


def rmsnorm(x, gain):
    xf = x.astype(jnp.float32)
    y = xf * lax.rsqrt(jnp.mean(xf * xf, axis=-1, keepdims=True) + RMS_EPS)
    return (y * gain.astype(jnp.float32)).astype(x.dtype)


def _compress(k, pos, w1, w2):
    b, s, g, hd = k.shape
    r = NSA_CMP_BLOCK // NSA_CMP_STRIDE
    n_cmp = s // NSA_CMP_STRIDE - r + 1
    chunks = k.reshape(b, s // NSA_CMP_STRIDE, NSA_CMP_STRIDE, g, hd)
    blocks = jnp.concatenate([chunks[:, j:j + n_cmp] for j in range(r)], axis=2)
    blocks = blocks + pos[None, None, :, None, :]
    flat = blocks.transpose(0, 3, 1, 2, 4).reshape(b, g, n_cmp, NSA_CMP_BLOCK * hd)
    return jax.nn.silu(flat @ w1) @ w2


def nsa_mixer(q, kc, vc, ks, vs, kw, vw, gate_logits, q_gain, kc_gain, ks_gain, kw_gain,
              pos_k, w1_k, w2_k, pos_v, w1_v, w2_v):
    b, s = q.shape[:2]
    G, E, hd = NSA_KV_GROUPS, NSA_HEADS // NSA_KV_GROUPS, NSA_HEAD_DIM
    QB, SB, W = NSA_Q_BLOCK, NSA_SEL_BLOCK, NSA_WINDOW
    nqb, n_sel = s // QB, s // SB
    n_top = min(NSA_TOP_N, n_sel)
    scale = hd ** -0.5

    q = rmsnorm(q.reshape(b, s, G, E, hd), q_gain)
    to_g = lambda t: t.transpose(0, 2, 1, 3)
    k_cmp = rmsnorm(_compress(kc, pos_k, w1_k, w2_k), kc_gain)
    v_cmp = _compress(vc, pos_v, w1_v, w2_v)
    n_cmp = k_cmp.shape[2]
    k_sel = to_g(rmsnorm(ks, ks_gain)).reshape(b, G, n_sel, SB, hd)
    v_sel = to_g(vs).reshape(b, G, n_sel, SB, hd)
    pad = ((0, 0), (0, 0), (W, 0), (0, 0))
    k_win = jnp.pad(to_g(rmsnorm(kw, kw_gain)), pad)
    v_win = jnp.pad(to_g(vw), pad)

    cmp_start = jnp.arange(n_cmp) * NSA_CMP_STRIDE
    cmp_end = cmp_start + NSA_CMP_BLOCK - 1
    sel_idx = jnp.arange(n_sel)
    overlap = ((cmp_start[:, None] <= sel_idx[None, :] * SB + SB - 1)
               & (cmp_end[:, None] >= sel_idx[None, :] * SB)).astype(jnp.float32)
    bi = jnp.arange(b)[:, None, None, None]
    gi = jnp.arange(G)[None, :, None, None]

    q_blk = q.reshape(b, nqb, QB, G, E, hd).transpose(1, 0, 3, 4, 2, 5)
    g_blk = jax.nn.sigmoid(gate_logits.reshape(b, nqb, QB, G, E, 3)).transpose(1, 0, 3, 4, 2, 5)

    def block(args):
        qq, gg, i = args
        t = i * QB + jnp.arange(QB)
        s_c = jnp.einsum('bgeqd,bgcd->bgeqc', qq, k_cmp).astype(jnp.float32) * scale
        m_c = cmp_end[None, :] <= t[:, None]
        p_c = jax.nn.softmax(jnp.where(m_c, s_c, MASK_VALUE), axis=-1) * jnp.any(m_c, axis=-1)[:, None]
        o_c = jnp.einsum('bgeqc,bgcd->bgeqd', p_c.astype(v_cmp.dtype), v_cmp)
        imp = jnp.einsum('bgeqc,cj->bgqj', p_c, overlap)
        cur = t // SB
        j = sel_idx[None, :]
        forced = (j == 0) | (j == cur[:, None]) | (j == cur[:, None] - 1)
        valid = j * SB <= t[:, None]
        imp = jnp.where(valid, imp + NSA_FORCE_BONUS * forced, -jnp.inf)
        _, top = lax.top_k(imp, n_top)
        top_ok = jnp.take_along_axis(jnp.broadcast_to(valid, imp.shape), top, axis=-1)
        kg = k_sel[bi, gi, top]
        vg = v_sel[bi, gi, top]
        kpos = top[..., None] * SB + jnp.arange(SB)
        m_s = (kpos <= t[:, None, None]) & top_ok[..., None]
        s_s = jnp.einsum('bgeqd,bgqnkd->bgeqnk', qq, kg).astype(jnp.float32) * scale
        s_s = jnp.where(m_s[:, :, None], s_s, MASK_VALUE).reshape(b, G, E, QB, n_top * SB)
        p_s = jax.nn.softmax(s_s, axis=-1).reshape(b, G, E, QB, n_top, SB)
        o_s = jnp.einsum('bgeqnk,bgqnkd->bgeqd', p_s.astype(vg.dtype), vg)
        kw_b = lax.dynamic_slice_in_dim(k_win, i * QB, QB + W, axis=2)
        vw_b = lax.dynamic_slice_in_dim(v_win, i * QB, QB + W, axis=2)
        wpos = i * QB - W + jnp.arange(QB + W)
        dist = t[:, None] - wpos[None, :]
        m_w = (wpos[None, :] >= 0) & (dist >= 0) & (dist < W)
        s_w = jnp.einsum('bgeqd,bgkd->bgeqk', qq, kw_b).astype(jnp.float32) * scale
        p_w = jax.nn.softmax(jnp.where(m_w, s_w, MASK_VALUE), axis=-1)
        o_w = jnp.einsum('bgeqk,bgkd->bgeqd', p_w.astype(vw_b.dtype), vw_b)
        return gg[..., 0:1] * o_c + gg[..., 1:2] * o_s + gg[..., 2:3] * o_w

    out = lax.map(block, (q_blk, g_blk, jnp.arange(nqb)))
    return out.transpose(1, 0, 4, 2, 3, 5).reshape(b, s, NSA_HEADS * hd)


def ssd_mixer(z, xbc, dt_raw, conv_w, conv_b, dt_bias, a_log, d_skip, norm_gain):
    b, s, cdim = xbc.shape
    G, E, P, N = SSM_GROUPS, SSM_HEADS // SSM_GROUPS, SSM_HEAD_DIM, SSM_STATE
    f32 = jnp.float32
    Lc = math.gcd(SSM_CHUNK, s)
    nc = s // Lc
    conv = lax.conv_general_dilated(xbc, conv_w[:, None, :], window_strides=(1,),
                                    padding=[(SSM_CONV - 1, 0)],
                                    dimension_numbers=('NWC', 'WIO', 'NWC'),
                                    feature_group_count=cdim)
    xbc = jax.nn.silu(conv + conv_b)
    xs, bm, cm = jnp.split(xbc, [SSM_INNER, SSM_INNER + G * N], axis=-1)
    x = xs.reshape(b, s, G, E, P)
    dt = jax.nn.softplus(dt_raw.astype(f32) + dt_bias.astype(f32)).reshape(b, s, G, E)
    a = -jnp.exp(a_log.astype(f32)).reshape(G, E)
    xd = (x.astype(f32) * dt[..., None]).reshape(b, nc, Lc, G, E, P)
    bc = bm.astype(f32).reshape(b, nc, Lc, G, N)
    cc = cm.astype(f32).reshape(b, nc, Lc, G, N)
    a_cs = jnp.cumsum((dt * a).reshape(b, nc, Lc, G, E).transpose(0, 3, 4, 1, 2), axis=-1)
    causal = jnp.tril(jnp.ones((Lc, Lc), dtype=bool))
    seg = jnp.exp(jnp.where(causal, a_cs[..., :, None] - a_cs[..., None, :], -jnp.inf))
    cb = jnp.einsum('bclgn,bcsgn->bgcls', cc, bc)
    y_diag = jnp.einsum('bgecls,bcsgep->bclgep', cb[:, :, None] * seg, xd)
    to_cl = lambda t: t.transpose(0, 3, 4, 1, 2)
    decay_in = to_cl(jnp.exp(a_cs[..., -1:] - a_cs))
    states = jnp.einsum('bclgn,bclgep->bcgepn', bc, xd * decay_in[..., None])
    chunk_decay = jnp.exp(a_cs[..., -1])

    def step(h, inp):
        st, dec = inp
        return h * dec[..., None, None] + st, h

    _, prev = lax.scan(step, jnp.zeros((b, G, E, P, N), f32),
                       (jnp.moveaxis(states, 1, 0), jnp.moveaxis(chunk_decay, -1, 0)))
    prev = jnp.moveaxis(prev, 0, 1)
    y_off = jnp.einsum('bclgn,bcgepn->bclgep', cc, prev) * to_cl(jnp.exp(a_cs))[..., None]
    y = (y_diag + y_off).reshape(b, s, G, E, P) + x.astype(f32) * d_skip.astype(f32).reshape(G, E, 1)
    gw = SSM_INNER // G
    y = y.reshape(b, s, G, gw) * jax.nn.silu(z.astype(f32)).reshape(b, s, G, gw)
    y = rmsnorm(y, norm_gain.reshape(G, gw))
    return y.reshape(b, s, SSM_INNER).astype(z.dtype)


def hier_moe(h, w_group, b_group, w_expert, b_expert, w_gate, w_up, w_down):
    b, s, d = h.shape
    n = b * s
    hf = h.reshape(n, d)
    grp_p = jax.nn.softmax((hf @ w_group + b_group).astype(jnp.float32), axis=-1)
    grp_w, grp_i = lax.top_k(grp_p, 1)
    e_logits = (hf @ w_expert + b_expert).astype(jnp.float32).reshape(n, MOE_GROUPS, MOE_EXPERTS_PER_GROUP)
    e_logits = jnp.take_along_axis(e_logits, grp_i[:, :, None], axis=1)[:, 0]
    top_p, top_i = lax.top_k(jax.nn.softmax(e_logits, axis=-1), MOE_TOP_K)
    weights = grp_w * top_p / jnp.sum(top_p, axis=-1, keepdims=True)
    experts = grp_i * MOE_EXPERTS_PER_GROUP + top_i

    e_flat = experts.reshape(-1)
    w_flat = weights.reshape(-1)
    tok = jnp.repeat(jnp.arange(n, dtype=jnp.int32), MOE_TOP_K)
    order = jnp.argsort(e_flat)
    se, st, sw = e_flat[order], tok[order], w_flat[order]
    counts = jnp.zeros((MOE_EXPERTS,), jnp.int32).at[e_flat].add(1)
    starts = jnp.cumsum(counts) - counts
    padded = (counts + MOE_ROW_BLOCK - 1) // MOE_ROW_BLOCK * MOE_ROW_BLOCK
    pend = jnp.cumsum(padded)
    pstarts = pend - padded
    dest = pstarts[se] + jnp.arange(n * MOE_TOP_K, dtype=jnp.int32) - starts[se]
    n_blocks = -(-(n * MOE_TOP_K) // MOE_ROW_BLOCK) + MOE_EXPERTS
    slot_tok = jnp.full((n_blocks * MOE_ROW_BLOCK,), n, jnp.int32).at[dest].set(st)
    slot_w = jnp.zeros((n_blocks * MOE_ROW_BLOCK,), jnp.float32).at[dest].set(sw)
    block_e = jnp.clip(jnp.searchsorted(pend, jnp.arange(n_blocks) * MOE_ROW_BLOCK, side='right'),
                       0, MOE_EXPERTS - 1)
    x_pad = jnp.concatenate([hf, jnp.zeros((1, d), hf.dtype)], axis=0)
    x_slots = x_pad[slot_tok].reshape(n_blocks, MOE_ROW_BLOCK, d)

    def expert_rows(args):
        xb, e = args
        return (jax.nn.silu(xb @ w_gate[e]) * (xb @ w_up[e])) @ w_down[e]

    out = lax.map(expert_rows, (x_slots, block_e)).reshape(-1, d)
    out = out * slot_w[:, None].astype(out.dtype)
    y = jnp.zeros((n + 1, d), out.dtype).at[slot_tok].add(out)[:n]
    return y.reshape(b, s, d)


def setup_inputs(seed: int = 0) -> dict:
    key = jax.random.key(seed)
    ks = iter(jax.random.split(key, 48))
    L = DEPTH
    hd = NSA_HEAD_DIM

    def nrm(shape, scale):
        return jax.random.normal(next(ks), shape, jnp.float32) * scale

    def gain(n):
        return 1.0 + nrm((L, n), 0.05)

    x = nrm((BATCH, SEQ, D_MODEL), 1.0)
    p = nrm((DEPTH, BATCH, SEQ, PLE_DIM), 1.0)
    dt = jnp.exp(jax.random.uniform(next(ks), (L, SSM_HEADS), jnp.float32,
                                    minval=math.log(SSM_DT_MIN), maxval=math.log(SSM_DT_MAX)))
    dt_bias = dt + jnp.log(-jnp.expm1(-dt))
    a_log = jnp.log(jax.random.uniform(next(ks), (L, SSM_HEADS), jnp.float32, minval=1.0, maxval=16.0))
    cl = NSA_CMP_BLOCK * hd
    return {
        'x': x,
        'p': p,
        'mix_norm': gain(D_MODEL),
        'w_in': nrm((L, D_MODEL, IN_W), D_MODEL ** -0.5),
        'nsa_q_gain': gain(hd),
        'nsa_kc_gain': gain(hd),
        'nsa_ks_gain': gain(hd),
        'nsa_kw_gain': gain(hd),
        'cmp_pos_k': nrm((L, NSA_CMP_BLOCK, hd), 0.1),
        'cmp_w1_k': nrm((L, cl, NSA_CMP_HIDDEN), cl ** -0.5),
        'cmp_w2_k': nrm((L, NSA_CMP_HIDDEN, hd), NSA_CMP_HIDDEN ** -0.5),
        'cmp_pos_v': nrm((L, NSA_CMP_BLOCK, hd), 0.1),
        'cmp_w1_v': nrm((L, cl, NSA_CMP_HIDDEN), cl ** -0.5),
        'cmp_w2_v': nrm((L, NSA_CMP_HIDDEN, hd), NSA_CMP_HIDDEN ** -0.5),
        'ssm_conv_w': nrm((L, SSM_CONV, SSM_CONV_DIM), SSM_CONV ** -0.5),
        'ssm_conv_b': nrm((L, SSM_CONV_DIM), 0.02),
        'ssm_dt_bias': dt_bias,
        'ssm_a_log': a_log,
        'ssm_d': 1.0 + nrm((L, SSM_HEADS), 0.1),
        'ssm_norm': gain(SSM_INNER),
        'w_branch_a': nrm((L, NSA_Q_W, D_MODEL), NSA_Q_W ** -0.5),
        'w_branch_b': nrm((L, SSM_INNER, D_MODEL), SSM_INNER ** -0.5),
        'w_out': nrm((L, D_MODEL, D_MODEL), D_MODEL ** -0.5),
        'moe_norm': gain(D_MODEL),
        'moe_w_group': nrm((L, D_MODEL, MOE_GROUPS), D_MODEL ** -0.5),
        'moe_b_group': nrm((L, MOE_GROUPS), 0.01),
        'moe_w_expert': nrm((L, D_MODEL, MOE_EXPERTS), D_MODEL ** -0.5),
        'moe_b_expert': nrm((L, MOE_EXPERTS), 0.01),
        'moe_w_gate': nrm((L, MOE_EXPERTS, D_MODEL, MOE_HIDDEN), D_MODEL ** -0.5),
        'moe_w_up': nrm((L, MOE_EXPERTS, D_MODEL, MOE_HIDDEN), D_MODEL ** -0.5),
        'moe_w_down': nrm((L, MOE_EXPERTS, MOE_HIDDEN, D_MODEL), MOE_HIDDEN ** -0.5),
        'ple_norm': gain(D_MODEL),
        'ple_w_gate': nrm((L, D_MODEL, D_MODEL), D_MODEL ** -0.5),
        'ple_w_proj': nrm((L, PLE_DIM, D_MODEL), PLE_DIM ** -0.5),
    }


def reference(x, p, mix_norm, w_in, nsa_q_gain, nsa_kc_gain, nsa_ks_gain, nsa_kw_gain,
              cmp_pos_k, cmp_w1_k, cmp_w2_k, cmp_pos_v, cmp_w1_v, cmp_w2_v,
              ssm_conv_w, ssm_conv_b, ssm_dt_bias, ssm_a_log, ssm_d, ssm_norm,
              w_branch_a, w_branch_b, w_out,
              moe_norm, moe_w_group, moe_b_group, moe_w_expert, moe_b_expert,
              moe_w_gate, moe_w_up, moe_w_down,
              ple_norm, ple_w_gate, ple_w_proj):
    b, s, _ = x.shape
    offsets = np.cumsum(IN_SIZES)[:-1].tolist()
    for i in range(DEPTH):
        h = rmsnorm(x, mix_norm[i])
        proj = h @ w_in[i]
        (q, kc, vc, ks, vs, kw, vw, nsa_g, z, xbc, dt_raw, br_g) = jnp.split(proj, offsets, axis=-1)
        kv = lambda t: t.reshape(b, s, NSA_KV_GROUPS, NSA_HEAD_DIM)
        y_a = nsa_mixer(q.reshape(b, s, NSA_HEADS, NSA_HEAD_DIM), kv(kc), kv(vc), kv(ks), kv(vs),
                        kv(kw), kv(vw), nsa_g, nsa_q_gain[i], nsa_kc_gain[i], nsa_ks_gain[i],
                        nsa_kw_gain[i], cmp_pos_k[i], cmp_w1_k[i], cmp_w2_k[i],
                        cmp_pos_v[i], cmp_w1_v[i], cmp_w2_v[i])
        y_b = ssd_mixer(z, xbc, dt_raw, ssm_conv_w[i], ssm_conv_b[i], ssm_dt_bias[i],
                        ssm_a_log[i], ssm_d[i], ssm_norm[i])
        gate_a, gate_b = jnp.split(jax.nn.sigmoid(br_g), 2, axis=-1)
        merged = gate_a * (y_a @ w_branch_a[i]) + gate_b * (y_b @ w_branch_b[i])
        x = x + merged @ w_out[i]
        h = rmsnorm(x, moe_norm[i])
        x = x + hier_moe(h, moe_w_group[i], moe_b_group[i], moe_w_expert[i], moe_b_expert[i],
                         moe_w_gate[i], moe_w_up[i], moe_w_down[i])
        h = rmsnorm(x, ple_norm[i])
        x = x + jax.nn.sigmoid(h @ ple_w_gate[i]) * (p[i] @ ple_w_proj[i])
    return x
```

```python
import functools
import math

import jax
import jax.numpy as jnp
from jax import lax
from jax.experimental import pallas as pl
from jax.experimental.pallas import tpu as pltpu

F32 = jnp.float32
BF16 = jnp.bfloat16

D_MODEL = 1024
RMS_EPS = 1e-6
NEG = -1e30

NSA_HEADS = 8
NSA_GROUPS = 2
NSA_E = NSA_HEADS // NSA_GROUPS
HD = 64
CMP_BLOCK = 32
CMP_STRIDE = 16
CMP_HIDDEN = 128
SEL_BLOCK = 64
TOP_N = 16
WINDOW = 512
FORCE_BONUS = 1e3

SSM_HEADS = 16
SSM_P = 64
SSM_INNER = SSM_HEADS * SSM_P
SSM_GROUPS = 2
SSM_N = 128
SSM_CONV = 4
SSM_CHUNK = 256
SSM_CONV_DIM = SSM_INNER + 2 * SSM_GROUPS * SSM_N

MOE_GROUPS = 4
MOE_EPG = 8
MOE_EXPERTS = MOE_GROUPS * MOE_EPG
MOE_TOP_K = 2
MOE_HIDDEN = 256
MOE_ROW_BLOCK = 256
PLE_DIM = 256

LANE = 128
VMEM_LIMIT = 52 * 1024 * 1024

Q_W = NSA_HEADS * HD
KVC_W = 2 * NSA_GROUPS * HD
KVSW_W = 4 * NSA_GROUPS * HD
GL_W = NSA_GROUPS * LANE
DT_W = LANE
BRG_W = 2 * D_MODEL
SEG_WIDTHS = (Q_W, KVC_W, KVSW_W, GL_W, SSM_INNER, SSM_CONV_DIM, DT_W, BRG_W)
SEG_DTYPES = (BF16, BF16, BF16, F32, BF16, BF16, F32, BF16)
PACKED_W = sum(SEG_WIDTHS)


def _cparams(sem):
    return pltpu.CompilerParams(dimension_semantics=sem, vmem_limit_bytes=VMEM_LIMIT)


def _const_spec(shape):
    n = len(shape)
    return pl.BlockSpec(shape, lambda *_: (0,) * n, pipeline_mode=pl.Buffered(1))


def _rms(xf, gain):
    return xf * lax.rsqrt(jnp.mean(xf * xf, axis=-1, keepdims=True) + RMS_EPS) * gain


def _silu(x):
    return x * jax.nn.sigmoid(x)


def _dot(a, b):
    return jnp.dot(a, b, preferred_element_type=F32)


def _dot_nt(a, b):
    return lax.dot_general(a, b, (((1,), (1,)), ((), ())), preferred_element_type=F32)


def _inproj_kernel(x_ref, g_ref, w_ref, *refs):
    outs, h_scr = refs[:-1], refs[-1]
    h_scr[...] = _rms(x_ref[...], g_ref[...]).astype(BF16)
    off = 0
    for o_ref, width in zip(outs, SEG_WIDTHS):
        for lo in range(0, width, 512):
            hi = min(lo + 512, width)
            o_ref[:, lo:hi] = _dot(h_scr[...], w_ref[:, off + lo:off + hi]).astype(o_ref.dtype)
        off += width


def _pack_w_in(w_in):
    sizes = (Q_W, 128, 128, 128, 128, 128, 128, NSA_HEADS * 3, SSM_INNER, SSM_CONV_DIM, SSM_HEADS, BRG_W)
    offs = [0]
    for s in sizes:
        offs.append(offs[-1] + s)
    seg = lambda i: w_in[:, offs[i]:offs[i + 1]]
    q, kc, vc, ks, vs, kw, vw, ng, z, xbc, dt, brg = (seg(i) for i in range(12))
    d = w_in.shape[0]
    pad = lambda a, w: jnp.concatenate([a, jnp.zeros((d, w - a.shape[1]), a.dtype)], axis=1)
    per_g = NSA_E * 3
    gl = jnp.concatenate([pad(ng[:, g * per_g:(g + 1) * per_g], LANE) for g in range(NSA_GROUPS)], axis=1)
    packed = jnp.concatenate([q, kc, vc, ks, vs, kw, vw, gl, z, xbc, pad(dt, DT_W), brg], axis=1)
    return packed.astype(BF16)


def _in_proj(x2, gain, w_packed, tm=512):
    n, d = x2.shape
    out_shape = tuple(jax.ShapeDtypeStruct((n, w), dt) for w, dt in zip(SEG_WIDTHS, SEG_DTYPES))
    return pl.pallas_call(
        _inproj_kernel,
        grid=(n // tm,),
        in_specs=[pl.BlockSpec((tm, d), lambda i: (i, 0)),
                  _const_spec((1, d)),
                  _const_spec((d, PACKED_W))],
        out_specs=tuple(pl.BlockSpec((tm, w), lambda i: (i, 0)) for w in SEG_WIDTHS),
        out_shape=out_shape,
        scratch_shapes=[pltpu.VMEM((tm, d), BF16)],
        compiler_params=_cparams(("parallel",)),
        name="in_proj",
    )(x2, gain.reshape(1, d), w_packed)


def _compress_kernel(x_ref, w1_ref, w2_ref, pos_ref, gain_ref, o_ref):
    kv = pl.program_id(1)
    x = x_ref[0, 0, 0]
    half = x.shape[1]
    h1 = _dot(x, w1_ref[0, :half, :])
    h2 = _dot(x, w1_ref[0, half:, :])
    nrow = x.shape[0]
    h2 = pltpu.roll(h2, nrow - 1, 0)
    bias = _dot(pos_ref[0], w1_ref[0])[0:1]
    hid = _silu(h1 + h2 + bias)
    out = _dot(hid.astype(BF16), w2_ref[0])
    normed = _rms(out, gain_ref[...])
    o_ref[0, 0, 0] = jnp.where(kv == 0, normed, out).astype(o_ref.dtype)


def _compress(kvc, pos, w1, w2, kc_gain, b, s):
    nrow = s // CMP_STRIDE
    x = kvc.reshape(b, nrow, CMP_STRIDE, 2, NSA_GROUPS, HD).transpose(0, 3, 4, 1, 2, 5)
    x = x.reshape(b, 2, NSA_GROUPS, nrow, CMP_STRIDE * HD)
    cl = CMP_BLOCK * HD
    pos_flat = jnp.zeros((2, 8, cl), BF16).at[:, 0, :].set(pos.reshape(2, cl).astype(BF16))
    return pl.pallas_call(
        _compress_kernel,
        grid=(b, 2, NSA_GROUPS),
        in_specs=[pl.BlockSpec((1, 1, 1, nrow, CMP_STRIDE * HD), lambda i, k, g: (i, k, g, 0, 0)),
                  pl.BlockSpec((1, cl, CMP_HIDDEN), lambda i, k, g: (k, 0, 0)),
                  pl.BlockSpec((1, CMP_HIDDEN, HD), lambda i, k, g: (k, 0, 0)),
                  pl.BlockSpec((1, 8, cl), lambda i, k, g: (k, 0, 0)),
                  pl.BlockSpec((1, HD), lambda i, k, g: (0, 0))],
        out_specs=pl.BlockSpec((1, 1, 1, nrow, HD), lambda i, k, g: (i, k, g, 0, 0)),
        out_shape=jax.ShapeDtypeStruct((b, 2, NSA_GROUPS, nrow, HD), BF16),
        compiler_params=_cparams(("parallel", "parallel", "parallel")),
        name="nsa_compress",
    )(x, w1.astype(BF16), w2.astype(BF16), pos_flat, kc_gain.reshape(1, HD))


def _kvprep_kernel(x_ref, gs_ref, gw_ref, ksel_ref, vsel_ref, kwin_ref, vwin_ref):
    ts = x_ref.shape[0]
    base = pl.program_id(1) * ts
    x = x_ref[...].astype(F32)
    n_hot = ksel_ref.shape[-1] - HD
    blk = (base + lax.broadcasted_iota(jnp.int32, (ts, n_hot), 0)) // SEL_BLOCK
    onehot = jnp.where(blk == lax.broadcasted_iota(jnp.int32, (ts, n_hot), 1), 1.0, 0.0)
    for g in range(NSA_GROUPS):
        col = lambda j: x[:, j * LANE + g * HD:j * LANE + (g + 1) * HD]
        ksel_ref[0, g] = jnp.concatenate([_rms(col(0), gs_ref[...]), onehot], axis=1).astype(BF16)
        vsel_ref[0, g] = col(1).astype(BF16)
        kwin_ref[0, g] = _rms(col(2), gw_ref[...]).astype(BF16)
        vwin_ref[0, g] = col(3).astype(BF16)


def _kv_prep(kvsw, ks_gain, kw_gain, b, s, ts=1024):
    ts = min(ts, s)
    g = NSA_GROUPS
    aug_w = max(LANE, HD + s // SEL_BLOCK)
    spec = lambda w: pl.BlockSpec((1, g, ts, w), lambda i, j: (i, 0, j, 0))
    return pl.pallas_call(
        _kvprep_kernel,
        grid=(b, s // ts),
        in_specs=[pl.BlockSpec((ts, KVSW_W), lambda i, j: (i * (s // ts) + j, 0)),
                  pl.BlockSpec((1, HD), lambda i, j: (0, 0)),
                  pl.BlockSpec((1, HD), lambda i, j: (0, 0))],
        out_specs=(spec(aug_w), spec(HD), spec(HD), spec(HD)),
        out_shape=(jax.ShapeDtypeStruct((b, g, s, aug_w), BF16),) + tuple(
            jax.ShapeDtypeStruct((b, g, s, HD), BF16) for _ in range(3)),
        compiler_params=_cparams(("parallel", "parallel")),
        name="nsa_kv_prep",
    )(kvsw, ks_gain.reshape(1, HD), kw_gain.reshape(1, HD))


def _flash_update(q, k, v, mask, m_ref, l_ref, acc_ref):
    s = _dot_nt(q, k)
    if mask is not None:
        s = jnp.where(mask, s, NEG)
    m_old = m_ref[...]
    m_new = jnp.maximum(m_old, jnp.max(s, axis=-1, keepdims=True))
    alpha = jnp.exp(m_old - m_new)
    p = jnp.exp(s - m_new)
    l_ref[...] = alpha * l_ref[...] + jnp.sum(p, axis=-1, keepdims=True)
    acc_ref[...] = alpha * acc_ref[...] + _dot(p.astype(BF16), v)
    m_ref[...] = m_new


def _flash_reset(m_ref, l_ref, acc_ref):
    m_ref[...] = jnp.full(m_ref.shape, NEG, F32)
    l_ref[...] = jnp.zeros(l_ref.shape, F32)
    acc_ref[...] = jnp.zeros(acc_ref.shape, F32)


def _nsa_kernel(q_ref, gl_ref, kc_ref, vc_ref, ksel_ref, vsel_ref, kwin_ref, vwin_ref, ovl_ref, qg_ref,
                o_ref, qs_ref, imp_ref, m_ref, l_ref, acc_ref, oc_ref, os_ref, *, tq):
    i = pl.program_id(2)
    rows = NSA_E * tq
    n_sel = ovl_ref.shape[0]
    scale = HD ** -0.5

    qf = q_ref[...].astype(F32)
    for e in range(NSA_E):
        qs_ref[e * tq:(e + 1) * tq, 0:HD] = (_rms(qf[:, e * HD:(e + 1) * HD], qg_ref[...]) * scale).astype(BF16)
    q64 = qs_ref[:, 0:HD]

    t_row = i * tq + (lax.broadcasted_iota(jnp.int32, (rows, 1), 0) & (tq - 1))

    n_cmp = kc_ref.shape[3]
    s_c = _dot_nt(q64, kc_ref[0, 0, 0])
    c_end = lax.broadcasted_iota(jnp.int32, (1, n_cmp), 1) * CMP_STRIDE + (CMP_BLOCK - 1)
    s_c = jnp.where(c_end <= t_row, s_c, NEG)
    e_c = jnp.exp(s_c - jnp.max(s_c, axis=-1, keepdims=True))
    p_c = e_c / jnp.sum(e_c, axis=-1, keepdims=True)
    p_c = jnp.where(t_row >= CMP_BLOCK - 1, p_c, 0.0)
    oc_ref[...] = _dot(p_c.astype(BF16), vc_ref[0, 0, 0])

    p_sum = p_c[0:tq]
    for e in range(1, NSA_E):
        p_sum = p_sum + p_c[e * tq:(e + 1) * tq]
    p_hi = p_sum.astype(BF16)
    p_lo = (p_sum - p_hi.astype(F32)).astype(BF16)
    imp = _dot_nt(ovl_ref[...], p_hi) + _dot_nt(ovl_ref[...], p_lo)
    j_idx = lax.broadcasted_iota(jnp.int32, (n_sel, tq), 0)
    cur = (i * tq + lax.broadcasted_iota(jnp.int32, (n_sel, tq), 1)) // SEL_BLOCK
    forced = (j_idx == 0) | (j_idx == cur) | (j_idx == cur - 1)
    valid = j_idx <= cur
    imp = jnp.where(valid, imp + jnp.where(forced, FORCE_BONUS, 0.0), -jnp.inf)
    imp_ref[...] = imp
    rank = jnp.zeros((n_sel, tq), F32)
    for j2 in range(n_sel):
        other = imp_ref[j2:j2 + 1, :]
        before = (other > imp) | ((other == imp) & (j_idx > j2))
        rank = rank + jnp.where(before, 1.0, 0.0)
    sel = (rank < min(TOP_N, n_sel)) & valid
    n_hot = qs_ref.shape[1] - HD
    sel_bias = jnp.where(sel, 0.0, NEG)
    sel_bias = jnp.concatenate([sel_bias, jnp.zeros((LANE - n_sel, tq), F32)], axis=0).T
    sel_bias = sel_bias[:, 0:n_hot].astype(BF16)
    for e in range(NSA_E):
        qs_ref[e * tq:(e + 1) * tq, HD:] = sel_bias

    col = lax.broadcasted_iota(jnp.int32, (1, tq), 1)
    r_in = lax.broadcasted_iota(jnp.int32, (rows, 1), 0) & (tq - 1)
    causal = col <= r_in
    start = pl.multiple_of(i * tq, tq)

    _flash_reset(m_ref, l_ref, acc_ref)
    _flash_update(qs_ref[...], ksel_ref[0, 0, pl.ds(start, tq), :], vsel_ref[0, 0, pl.ds(start, tq), :],
                  causal, m_ref, l_ref, acc_ref)

    def sel_body(kt, carry):
        ks = pl.multiple_of(kt * tq, tq)
        _flash_update(qs_ref[...], ksel_ref[0, 0, pl.ds(ks, tq), :], vsel_ref[0, 0, pl.ds(ks, tq), :],
                      None, m_ref, l_ref, acc_ref)
        return carry

    lax.fori_loop(0, i, sel_body, 0)
    os_ref[...] = acc_ref[...] / l_ref[...]

    _flash_reset(m_ref, l_ref, acc_ref)
    _flash_update(q64, kwin_ref[0, 0, pl.ds(start, tq), :], vwin_ref[0, 0, pl.ds(start, tq), :],
                  causal, m_ref, l_ref, acc_ref)
    for back in range(1, WINDOW // tq + 1):
        @pl.when(i >= back)
        def _():
            ks = pl.multiple_of((i - back) * tq, tq)
            dist = back * tq + r_in - col
            mask = (dist < WINDOW) if (back + 1) * tq > WINDOW else None
            _flash_update(q64, kwin_ref[0, 0, pl.ds(ks, tq), :], vwin_ref[0, 0, pl.ds(ks, tq), :],
                          mask, m_ref, l_ref, acc_ref)
    o_w = acc_ref[...] / l_ref[...]

    gate = jax.nn.sigmoid(gl_ref[...])
    for e in range(NSA_E):
        rs = slice(e * tq, (e + 1) * tq)
        y = (gate[:, 3 * e:3 * e + 1] * oc_ref[rs, :] + gate[:, 3 * e + 1:3 * e + 2] * os_ref[rs, :]
             + gate[:, 3 * e + 2:3 * e + 3] * o_w[rs, :])
        o_ref[:, e * HD:(e + 1) * HD] = y.astype(o_ref.dtype)


def _nsa_attention(q, gl, cmp_kv, ksel, vsel, kwin, vwin, q_gain, b, s, tq=256):
    tq = min(tq, s)
    nq = s // tq
    n_sel = s // SEL_BLOCK
    nrow = s // CMP_STRIDE
    g = NSA_GROUPS
    c0 = jnp.arange(nrow) * CMP_STRIDE
    j0 = jnp.arange(n_sel) * SEL_BLOCK
    ovl = ((c0[None, :] <= j0[:, None] + SEL_BLOCK - 1) & (c0[None, :] + CMP_BLOCK - 1 >= j0[:, None]))
    ovl = ovl.astype(BF16)
    aug_w = max(2 * HD, HD + n_sel)
    assert ksel.shape[-1] == aug_w
    rows = NSA_E * tq
    kv_spec = lambda w: pl.BlockSpec((1, 1, s, w), lambda bi, gi, i: (bi, gi, 0, 0))
    return pl.pallas_call(
        functools.partial(_nsa_kernel, tq=tq),
        grid=(b, g, nq),
        in_specs=[pl.BlockSpec((tq, NSA_E * HD), lambda bi, gi, i: (bi * nq + i, gi)),
                  pl.BlockSpec((tq, LANE), lambda bi, gi, i: (bi * nq + i, gi)),
                  pl.BlockSpec((1, 1, 1, nrow, HD), lambda bi, gi, i: (bi, 0, gi, 0, 0)),
                  pl.BlockSpec((1, 1, 1, nrow, HD), lambda bi, gi, i: (bi, 1, gi, 0, 0)),
                  kv_spec(aug_w), kv_spec(HD), kv_spec(HD), kv_spec(HD),
                  pl.BlockSpec((n_sel, nrow), lambda bi, gi, i: (0, 0)),
                  pl.BlockSpec((1, HD), lambda bi, gi, i: (0, 0))],
        out_specs=pl.BlockSpec((tq, NSA_E * HD), lambda bi, gi, i: (bi * nq + i, gi)),
        out_shape=jax.ShapeDtypeStruct((b * s, NSA_HEADS * HD), BF16),
        scratch_shapes=[pltpu.VMEM((rows, aug_w), BF16),
                        pltpu.VMEM((n_sel, tq), F32),
                        pltpu.VMEM((rows, 1), F32),
                        pltpu.VMEM((rows, 1), F32),
                        pltpu.VMEM((rows, HD), F32),
                        pltpu.VMEM((rows, HD), F32),
                        pltpu.VMEM((rows, HD), F32)],
        compiler_params=_cparams(("parallel", "parallel", "arbitrary")),
        name="nsa_attention",
    )(q, gl, cmp_kv, cmp_kv, ksel, vsel, kwin, vwin, ovl, q_gain.reshape(1, HD))


def _softplus(x):
    return jnp.maximum(x, 0.0) + jnp.log1p(jnp.exp(-jnp.abs(x)))


def _dot_exact(a, b):
    return jnp.dot(a, b, precision=lax.Precision.HIGHEST, preferred_element_type=F32)


def _ssd_kernel(z_ref, xbc_ref, dt_ref, dtT_ref, cw_ref, cb_ref, dtb_ref, dtbT_ref, alog_ref, alogT_ref,
                dskip_ref, ng_ref, exp_ref, o_ref, xpad_ref, state_ref, y_ref):
    c = pl.program_id(1)
    L = xbc_ref.shape[0]
    gw = SSM_INNER // SSM_GROUPS
    e_per_g = SSM_HEADS // SSM_GROUPS

    @pl.when(c == 0)
    def _():
        xpad_ref[0:8, :] = jnp.zeros((8, SSM_CONV_DIM), F32)
        state_ref[...] = jnp.zeros(state_ref.shape, F32)

    xpad_ref[8:8 + L, :] = xbc_ref[...].astype(F32)
    acc = cb_ref[...] + cw_ref[0:1, :] * xpad_ref[5:5 + L, :]
    for k in range(1, SSM_CONV):
        acc = acc + cw_ref[k:k + 1, :] * xpad_ref[5 + k:5 + k + L, :]
    xpad_ref[0:8, :] = xpad_ref[L:L + 8, :]
    xact = _silu(acc)
    xs = xact[:, :SSM_INNER]

    li = lax.broadcasted_iota(jnp.int32, (L, L), 0)
    si = lax.broadcasted_iota(jnp.int32, (L, L), 1)
    causal = li >= si
    dt = _softplus(dt_ref[...] + dtb_ref[...])
    a_cs = _dot_exact(jnp.where(causal, 1.0, 0.0), dt * -jnp.exp(alog_ref[...]))
    dtT = _softplus(dtT_ref[0] + dtbT_ref[...])
    a_csT = _dot_exact(dtT * -jnp.exp(alogT_ref[...]), jnp.where(li <= si, 1.0, 0.0))

    dt_x = _dot_exact(dt, exp_ref[...])
    acs_x = _dot_exact(a_cs, exp_ref[...])
    alast_x = acs_x[L - 1:L, :]
    xd = xs * dt_x
    xdd = (xd * jnp.exp(alast_x - acs_x)).astype(BF16)
    eacs_x = jnp.exp(acs_x)
    xd = xd.astype(BF16)

    for g in range(SSM_GROUPS):
        gs = slice(g * gw, (g + 1) * gw)
        bg = xact[:, SSM_INNER + g * SSM_N:SSM_INNER + (g + 1) * SSM_N].astype(BF16)
        cg = xact[:, SSM_INNER + (SSM_GROUPS + g) * SSM_N:SSM_INNER + (SSM_GROUPS + g + 1) * SSM_N].astype(BF16)
        cb = _dot_nt(cg, bg)
        st = state_ref[g]
        y_ref[:, gs] = _dot(cg, st.astype(BF16)) * eacs_x[:, gs] + xs[:, gs] * dskip_ref[:, gs]
        for e in range(e_per_g):
            h = g * e_per_g + e
            hs = slice(h * SSM_P, (h + 1) * SSM_P)
            seg = jnp.exp(jnp.where(causal, a_cs[:, h:h + 1] - a_csT[h:h + 1, :], NEG))
            y_ref[:, hs] += _dot((cb * seg).astype(BF16), xd[:, hs])
        upd = lax.dot_general(bg, xdd[:, gs], (((0,), (0,)), ((), ())), preferred_element_type=F32)
        state_ref[g] = st * jnp.exp(alast_x[:, gs]) + upd

    y = y_ref[...] * _silu(z_ref[...].astype(F32))
    for g in range(SSM_GROUPS):
        gs = slice(g * gw, (g + 1) * gw)
        o_ref[:, gs] = _rms(y[:, gs], ng_ref[:, gs]).astype(o_ref.dtype)


def _ssd_mixer(z, xbc, dt, conv_w, conv_b, dt_bias, a_log, d_skip, norm_gain, b, s):
    L = math.gcd(SSM_CHUNK, s)
    nc = s // L
    h = SSM_HEADS
    dtT = dt[:, :h].reshape(b, s, h).transpose(0, 2, 1)
    pad_l = lambda v: jnp.zeros((1, LANE), F32).at[0, :h].set(v)
    expand = jnp.zeros((LANE, SSM_INNER), F32).at[:h].set(jnp.repeat(jnp.eye(h, dtype=F32), SSM_P, axis=1))
    row = lambda w: pl.BlockSpec((L, w), lambda bi, ci: (bi * nc + ci, 0))
    full = lambda shp: pl.BlockSpec(shp, lambda bi, ci: (0,) * len(shp))
    return pl.pallas_call(
        _ssd_kernel,
        grid=(b, nc),
        in_specs=[row(SSM_INNER), row(SSM_CONV_DIM), row(LANE),
                  pl.BlockSpec((1, h, L), lambda bi, ci: (bi, 0, ci)),
                  full((SSM_CONV, SSM_CONV_DIM)), full((1, SSM_CONV_DIM)),
                  full((1, LANE)), full((h, 1)), full((1, LANE)), full((h, 1)),
                  full((1, SSM_INNER)), full((1, SSM_INNER)), full((LANE, SSM_INNER))],
        out_specs=row(SSM_INNER),
        out_shape=jax.ShapeDtypeStruct((b * s, SSM_INNER), BF16),
        scratch_shapes=[pltpu.VMEM((L + 8, SSM_CONV_DIM), F32),
                        pltpu.VMEM((SSM_GROUPS, SSM_N, SSM_INNER // SSM_GROUPS), F32),
                        pltpu.VMEM((L, SSM_INNER), F32)],
        compiler_params=_cparams(("parallel", "arbitrary")),
        name="ssd_mixer",
    )(z, xbc, dt, dtT, conv_w, conv_b.reshape(1, -1), pad_l(dt_bias), dt_bias.reshape(h, 1),
      pad_l(a_log), a_log.reshape(h, 1), jnp.repeat(d_skip, SSM_P).reshape(1, -1),
      norm_gain.reshape(1, -1), expand)


ROUTE_ROWS = 8 + MOE_EXPERTS


def _merge_kernel(x_ref, ya_ref, yb_ref, brg_ref, wa_ref, wb_ref, wo_ref, mg_ref, wr_ref, br_ref,
                  x1_ref, h2_ref, eid_ref, ew_ref):
    tm = x_ref.shape[0]
    gate = jax.nn.sigmoid(brg_ref[...].astype(F32))
    merged = gate[:, :D_MODEL] * _dot(ya_ref[...], wa_ref[...]) + gate[:, D_MODEL:] * _dot(yb_ref[...], wb_ref[...])
    x1 = x_ref[...] + _dot(merged.astype(BF16), wo_ref[...])
    x1_ref[...] = x1
    h2 = _rms(x1, mg_ref[...])
    h2_ref[...] = h2

    lg = _dot_nt(wr_ref[...], h2.astype(BF16)) + br_ref[...]
    r = lax.broadcasted_iota(jnp.int32, (ROUTE_ROWS, tm), 0)
    big = ROUTE_ROWS
    is_g = r < MOE_GROUPS
    gmax = jnp.max(jnp.where(is_g, lg, -jnp.inf), axis=0, keepdims=True)
    gsum = jnp.sum(jnp.where(is_g, jnp.exp(lg - gmax), 0.0), axis=0, keepdims=True)
    grp_w = 1.0 / gsum
    grp_i = jnp.min(jnp.where(is_g & (lg == gmax), r, big), axis=0, keepdims=True)
    is_e = (r >= 8) & (((r - 8) // MOE_EPG) == grp_i)
    le = jnp.where(is_e, lg, -jnp.inf)
    emax = jnp.max(le, axis=0, keepdims=True)
    esum = jnp.sum(jnp.where(is_e, jnp.exp(lg - emax), 0.0), axis=0, keepdims=True)
    i1 = jnp.min(jnp.where(le == emax, r, big), axis=0, keepdims=True)
    le2 = jnp.where(r == i1, -jnp.inf, le)
    e2max = jnp.max(le2, axis=0, keepdims=True)
    i2 = jnp.min(jnp.where((le2 == e2max) & is_e & (r != i1), r, big), axis=0, keepdims=True)
    p1 = 1.0 / esum
    p2 = jnp.exp(e2max - emax) / esum
    w1 = grp_w * p1 / (p1 + p2)
    w2 = grp_w * p2 / (p1 + p2)
    r8 = lax.broadcasted_iota(jnp.int32, (8, tm), 0)
    eid_ref[...] = jnp.where(r8 == 0, i1 - 8, jnp.where(r8 == 1, i2 - 8, 0))
    ew_ref[...] = jnp.where(r8 == 0, w1, jnp.where(r8 == 1, w2, 0.0))


def _merge_route(x2, y_a, y_b, brg, w_a, w_b, w_o, moe_gain, w_group, b_group, w_expert, b_expert, tm=512):
    n, d = x2.shape
    tm = min(tm, n)
    wr = jnp.zeros((ROUTE_ROWS, d), F32).at[:MOE_GROUPS].set(w_group.T).at[8:].set(w_expert.T).astype(BF16)
    br = jnp.zeros((ROUTE_ROWS, 1), F32).at[:MOE_GROUPS, 0].set(b_group).at[8:, 0].set(b_expert)
    row = lambda w: pl.BlockSpec((tm, w), lambda i: (i, 0))
    colblk = pl.BlockSpec((8, tm), lambda i: (0, i))
    return pl.pallas_call(
        _merge_kernel,
        grid=(n // tm,),
        in_specs=[row(d), row(Q_W), row(SSM_INNER), row(BRG_W),
                  _const_spec((Q_W, d)), _const_spec((SSM_INNER, d)), _const_spec((d, d)),
                  _const_spec((1, d)), _const_spec((ROUTE_ROWS, d)), _const_spec((ROUTE_ROWS, 1))],
        out_specs=(row(d), row(d), colblk, colblk),
        out_shape=(jax.ShapeDtypeStruct((n, d), F32), jax.ShapeDtypeStruct((n, d), F32),
                   jax.ShapeDtypeStruct((8, n), jnp.int32), jax.ShapeDtypeStruct((8, n), F32)),
        compiler_params=_cparams(("parallel",)),
        name="merge_route",
    )(x2, y_a, y_b, brg, w_a.astype(BF16), w_b.astype(BF16), w_o.astype(BF16), moe_gain.reshape(1, d), wr, br)


def _rank_kernel(eid_ref, rank_ref, cnt_ref, run_ref, tri_ref):
    i = pl.program_id(0)
    tm = eid_ref.shape[1]

    @pl.when(i == 0)
    def _():
        run_ref[...] = jnp.zeros(run_ref.shape, F32)
        earlier = lax.broadcasted_iota(jnp.int32, (tm, tm), 0) < lax.broadcasted_iota(jnp.int32, (tm, tm), 1)
        tri_ref[...] = jnp.where(earlier, 1.0, 0.0).astype(BF16)

    r = lax.broadcasted_iota(jnp.int32, (MOE_EXPERTS, tm), 0)
    oh0 = r == eid_ref[0:1, :]
    oh1 = r == eid_ref[1:2, :]
    oh = jnp.where(oh0 | oh1, 1.0, 0.0)
    prefix = _dot(oh.astype(BF16), tri_ref[...]) + run_ref[:, 0:1]
    rank0 = jnp.sum(jnp.where(oh0, prefix, 0.0), axis=0, keepdims=True)
    rank1 = jnp.sum(jnp.where(oh1, prefix, 0.0), axis=0, keepdims=True)
    r8 = lax.broadcasted_iota(jnp.int32, (8, tm), 0)
    rank_ref[...] = jnp.where(r8 == 0, rank0, jnp.where(r8 == 1, rank1, 0.0)).astype(jnp.int32)
    run_ref[...] = run_ref[...] + jnp.sum(oh, axis=1, keepdims=True)
    cnt_ref[...] = run_ref[...]


def _moe_rank(eid, tm=512):
    n = eid.shape[1]
    tm = min(tm, n)
    return pl.pallas_call(
        _rank_kernel,
        grid=(n // tm,),
        in_specs=[pl.BlockSpec((8, tm), lambda i: (0, i))],
        out_specs=(pl.BlockSpec((8, tm), lambda i: (0, i)),
                   pl.BlockSpec((MOE_EXPERTS, LANE), lambda i: (0, 0))),
        out_shape=(jax.ShapeDtypeStruct((8, n), jnp.int32),
                   jax.ShapeDtypeStruct((MOE_EXPERTS, LANE), F32)),
        scratch_shapes=[pltpu.VMEM((MOE_EXPERTS, LANE), F32), pltpu.VMEM((tm, tm), BF16)],
        compiler_params=_cparams(("arbitrary",)),
        name="moe_rank",
    )(eid)


def _row_copy(src, dst, sem):
    return pltpu.make_async_copy(src, dst, sem)


def _dispatch_kernel(slot_ref, h_ref, xs_in_ref, xs_ref, sem):
    del xs_in_ref
    tm = h_ref.shape[0]

    def issue(n, carry):
        for k in range(MOE_TOP_K):
            _row_copy(h_ref.at[pl.ds(n, 1), :], xs_ref.at[pl.ds(slot_ref[k, n], 1), :], sem).start()
        return carry

    lax.fori_loop(0, tm, issue, 0)

    def drain(n, carry):
        for k in range(MOE_TOP_K):
            _row_copy(h_ref.at[pl.ds(0, 1), :], xs_ref.at[pl.ds(0, 1), :], sem).wait()
        return carry

    lax.fori_loop(0, tm, drain, 0)


def _moe_dispatch(slot, h2, n_slots, tm=256):
    n, d = h2.shape
    tm = min(tm, n)
    xs0 = jnp.zeros((n_slots, d), h2.dtype)
    return pl.pallas_call(
        _dispatch_kernel,
        grid=(n // tm,),
        in_specs=[pl.BlockSpec((MOE_TOP_K, tm), lambda i: (0, i), memory_space=pltpu.SMEM),
                  pl.BlockSpec((tm, d), lambda i: (i, 0)),
                  pl.BlockSpec(memory_space=pl.ANY)],
        out_specs=pl.BlockSpec(memory_space=pl.ANY),
        out_shape=jax.ShapeDtypeStruct((n_slots, d), h2.dtype),
        scratch_shapes=[pltpu.SemaphoreType.DMA],
        input_output_aliases={2: 0},
        compiler_params=_cparams(("arbitrary",)),
        name="moe_dispatch",
    )(slot, h2, xs0)


def _expert_kernel(be_ref, nu_ref, xs_ref, wg_ref, wu_ref, wd_ref, o_ref):
    j = pl.program_id(0)

    @pl.when(j < nu_ref[0])
    def _():
        x = xs_ref[...].astype(BF16)
        act = _silu(_dot(x, wg_ref[0])) * _dot(x, wu_ref[0])
        o_ref[...] = _dot(act.astype(BF16), wd_ref[0])

    @pl.when(j >= nu_ref[0])
    def _():
        o_ref[...] = jnp.zeros(o_ref.shape, o_ref.dtype)


def _moe_experts(block_e, n_used, xs, w_gate, w_up, w_down):
    n_slots, d = xs.shape
    rb = MOE_ROW_BLOCK
    wspec = lambda shp: pl.BlockSpec((1,) + shp, lambda j, be, nu: (be[j], 0, 0))
    return pl.pallas_call(
        _expert_kernel,
        grid_spec=pltpu.PrefetchScalarGridSpec(
            num_scalar_prefetch=2,
            grid=(n_slots // rb,),
            in_specs=[pl.BlockSpec((rb, d), lambda j, be, nu: (j, 0)),
                      wspec((d, MOE_HIDDEN)), wspec((d, MOE_HIDDEN)), wspec((MOE_HIDDEN, d))],
            out_specs=pl.BlockSpec((rb, d), lambda j, be, nu: (j, 0))),
        out_shape=jax.ShapeDtypeStruct((n_slots, d), F32),
        compiler_params=_cparams(("arbitrary",)),
        name="moe_experts",
    )(block_e, n_used, xs, w_gate.astype(BF16), w_up.astype(BF16), w_down.astype(BF16))


def _combine_kernel(slot_ref, ew_ref, x1_ref, p_ref, eo_ref, pg_ref, wg_ref, wp_ref, o_ref, rows_ref, sem):
    tm = x1_ref.shape[0]

    def issue(n, carry):
        for k in range(MOE_TOP_K):
            _row_copy(eo_ref.at[pl.ds(slot_ref[k, n], 1), :], rows_ref.at[k, pl.ds(n, 1), :], sem).start()
        return carry

    lax.fori_loop(0, tm, issue, 0)

    def drain(n, carry):
        for k in range(MOE_TOP_K):
            _row_copy(eo_ref.at[pl.ds(0, 1), :], rows_ref.at[k, pl.ds(0, 1), :], sem).wait()
        return carry

    lax.fori_loop(0, tm, drain, 0)

    x2 = x1_ref[...] + ew_ref[:, 0:1] * rows_ref[0] + ew_ref[:, 1:2] * rows_ref[1]
    h3 = _rms(x2, pg_ref[...]).astype(BF16)
    o_ref[...] = x2 + jax.nn.sigmoid(_dot(h3, wg_ref[...])) * _dot(p_ref[...].astype(BF16), wp_ref[...])


def _moe_combine_ple(slot, ew_cols, x1, p2, expert_out, ple_gain, w_gate, w_proj, tm=256):
    n, d = x1.shape
    tm = min(tm, n)
    row = lambda w: pl.BlockSpec((tm, w), lambda i: (i, 0))
    return pl.pallas_call(
        _combine_kernel,
        grid=(n // tm,),
        in_specs=[pl.BlockSpec((MOE_TOP_K, tm), lambda i: (0, i), memory_space=pltpu.SMEM),
                  row(8), row(d), row(PLE_DIM),
                  pl.BlockSpec(memory_space=pl.ANY),
                  _const_spec((1, d)), _const_spec((d, d)), _const_spec((PLE_DIM, d))],
        out_specs=row(d),
        out_shape=jax.ShapeDtypeStruct((n, d), F32),
        scratch_shapes=[pltpu.VMEM((MOE_TOP_K, tm, d), F32), pltpu.SemaphoreType.DMA],
        compiler_params=_cparams(("arbitrary",)),
        name="moe_combine_ple",
    )(slot, ew_cols, x1, p2, expert_out, ple_gain.reshape(1, d), w_gate.astype(BF16), w_proj.astype(BF16))


def _layer(x2, p2, b, s, mix_norm, w_in, nsa_q_gain, nsa_kc_gain, nsa_ks_gain, nsa_kw_gain,
           cmp_pos_k, cmp_w1_k, cmp_w2_k, cmp_pos_v, cmp_w1_v, cmp_w2_v,
           ssm_conv_w, ssm_conv_b, ssm_dt_bias, ssm_a_log, ssm_d, ssm_norm,
           w_branch_a, w_branch_b, w_out,
           moe_norm, moe_w_group, moe_b_group, moe_w_expert, moe_b_expert,
           moe_w_gate, moe_w_up, moe_w_down, ple_norm, ple_w_gate, ple_w_proj):
    n = b * s
    q, kvc, kvsw, gl, z, xbc, dt, brg = _in_proj(x2, mix_norm, _pack_w_in(w_in))

    cmp_kv = _compress(kvc, jnp.stack([cmp_pos_k, cmp_pos_v]), jnp.stack([cmp_w1_k, cmp_w1_v]),
                       jnp.stack([cmp_w2_k, cmp_w2_v]), nsa_kc_gain, b, s)
    ksel, vsel, kwin, vwin = _kv_prep(kvsw, nsa_ks_gain, nsa_kw_gain, b, s)
    y_a = _nsa_attention(q, gl, cmp_kv, ksel, vsel, kwin, vwin, nsa_q_gain, b, s)
    y_b = _ssd_mixer(z, xbc, dt, ssm_conv_w, ssm_conv_b, ssm_dt_bias, ssm_a_log, ssm_d, ssm_norm, b, s)

    x1, h2, eid, ew = _merge_route(x2, y_a, y_b, brg, w_branch_a, w_branch_b, w_out, moe_norm,
                                   moe_w_group, moe_b_group, moe_w_expert, moe_b_expert)

    rank, cnt = _moe_rank(eid)
    rb = MOE_ROW_BLOCK
    counts = cnt[:, 0].astype(jnp.int32)
    padded = (counts + rb - 1) // rb * rb
    pend = jnp.cumsum(padded)
    pstart = pend - padded
    slot = pstart[eid[:MOE_TOP_K]] + rank[:MOE_TOP_K]
    n_blocks = -(-(n * MOE_TOP_K) // rb) + MOE_EXPERTS
    block_e = jnp.clip(jnp.searchsorted(pend, jnp.arange(n_blocks) * rb, side='right'), 0, MOE_EXPERTS - 1)
    n_used = (pend[-1:] // rb).astype(jnp.int32)

    xs = _moe_dispatch(slot, h2, n_blocks * rb)
    expert_out = _moe_experts(block_e.astype(jnp.int32), n_used, xs, moe_w_gate, moe_w_up, moe_w_down)
    ew_cols = ew.T
    return _moe_combine_ple(slot, ew_cols, x1, p2, expert_out, ple_norm, ple_w_gate, ple_w_proj)


def kernel(x, p, mix_norm, w_in, nsa_q_gain, nsa_kc_gain, nsa_ks_gain, nsa_kw_gain, cmp_pos_k, cmp_w1_k, cmp_w2_k, cmp_pos_v, cmp_w1_v, cmp_w2_v, ssm_conv_w, ssm_conv_b, ssm_dt_bias, ssm_a_log, ssm_d, ssm_norm, w_branch_a, w_branch_b, w_out, moe_norm, moe_w_group, moe_b_group, moe_w_expert, moe_b_expert, moe_w_gate, moe_w_up, moe_w_down, ple_norm, ple_w_gate, ple_w_proj):
    b, s, d = x.shape
    params = (mix_norm, w_in, nsa_q_gain, nsa_kc_gain, nsa_ks_gain, nsa_kw_gain,
              cmp_pos_k, cmp_w1_k, cmp_w2_k, cmp_pos_v, cmp_w1_v, cmp_w2_v,
              ssm_conv_w, ssm_conv_b, ssm_dt_bias, ssm_a_log, ssm_d, ssm_norm,
              w_branch_a, w_branch_b, w_out,
              moe_norm, moe_w_group, moe_b_group, moe_w_expert, moe_b_expert,
              moe_w_gate, moe_w_up, moe_w_down, ple_norm, ple_w_gate, ple_w_proj)
    x2 = x.reshape(b * s, d)
    for i in range(p.shape[0]):
        x2 = _layer(x2, p[i].reshape(b * s, -1), b, s, *(w[i] for w in params))
    return x2.reshape(b, s, d)
```

```python
import functools
import math

import jax
import jax.numpy as jnp
from jax import lax
from jax.experimental import pallas as pl
from jax.experimental.pallas import tpu as pltpu

F32 = jnp.float32
BF16 = jnp.bfloat16

D_MODEL = 1024
RMS_EPS = 1e-6
NEG = -1e30

NSA_HEADS = 8
NSA_GROUPS = 2
NSA_E = NSA_HEADS // NSA_GROUPS
HD = 64
CMP_BLOCK = 32
CMP_STRIDE = 16
CMP_HIDDEN = 128
SEL_BLOCK = 64
TOP_N = 16
WINDOW = 512
FORCE_BONUS = 1e3

SSM_HEADS = 16
SSM_P = 64
SSM_INNER = SSM_HEADS * SSM_P
SSM_GROUPS = 2
SSM_N = 128
SSM_CONV = 4
SSM_CHUNK = 256
SSM_CONV_DIM = SSM_INNER + 2 * SSM_GROUPS * SSM_N

MOE_GROUPS = 4
MOE_EPG = 8
MOE_EXPERTS = MOE_GROUPS * MOE_EPG
MOE_TOP_K = 2
MOE_HIDDEN = 256
MOE_ROW_BLOCK = 256
PLE_DIM = 256

LANE = 128
VMEM_LIMIT = 52 * 1024 * 1024

Q_W = NSA_HEADS * HD
KVC_W = 2 * NSA_GROUPS * HD
KVSW_W = 4 * NSA_GROUPS * HD
GL_W = NSA_GROUPS * LANE
DT_W = LANE
BRG_W = 2 * D_MODEL
SEG_WIDTHS = (Q_W, KVC_W, KVSW_W, GL_W, SSM_INNER, SSM_CONV_DIM, DT_W, BRG_W)
SEG_DTYPES = (BF16, BF16, BF16, F32, BF16, BF16, F32, BF16)
PACKED_W = sum(SEG_WIDTHS)


def _cparams(sem):
    return pltpu.CompilerParams(dimension_semantics=sem, vmem_limit_bytes=VMEM_LIMIT)


def _const_spec(shape):
    n = len(shape)
    return pl.BlockSpec(shape, lambda *_: (0,) * n, pipeline_mode=pl.Buffered(1))


def _rms(xf, gain):
    return xf * lax.rsqrt(jnp.mean(xf * xf, axis=-1, keepdims=True) + RMS_EPS) * gain


def _silu(x):
    return x * jax.nn.sigmoid(x)


def _dot(a, b):
    return jnp.dot(a, b, preferred_element_type=F32)


def _dot_nt(a, b):
    return lax.dot_general(a, b, (((1,), (1,)), ((), ())), preferred_element_type=F32)


def _inproj_kernel(x_ref, g_ref, w_ref, *refs):
    outs, h_scr = refs[:-1], refs[-1]
    h_scr[...] = _rms(x_ref[...], g_ref[...]).astype(BF16)
    off = 0
    for o_ref, width in zip(outs, SEG_WIDTHS):
        for lo in range(0, width, 512):
            hi = min(lo + 512, width)
            o_ref[:, lo:hi] = _dot(h_scr[...], w_ref[:, off + lo:off + hi]).astype(o_ref.dtype)
        off += width


def _pack_w_in(w_in):
    sizes = (Q_W, 128, 128, 128, 128, 128, 128, NSA_HEADS * 3, SSM_INNER, SSM_CONV_DIM, SSM_HEADS, BRG_W)
    offs = [0]
    for s in sizes:
        offs.append(offs[-1] + s)
    seg = lambda i: w_in[:, offs[i]:offs[i + 1]]
    q, kc, vc, ks, vs, kw, vw, ng, z, xbc, dt, brg = (seg(i) for i in range(12))
    d = w_in.shape[0]
    pad = lambda a, w: jnp.concatenate([a, jnp.zeros((d, w - a.shape[1]), a.dtype)], axis=1)
    per_g = NSA_E * 3
    gl = jnp.concatenate([pad(ng[:, g * per_g:(g + 1) * per_g], LANE) for g in range(NSA_GROUPS)], axis=1)
    packed = jnp.concatenate([q, kc, vc, ks, vs, kw, vw, gl, z, xbc, pad(dt, DT_W), brg], axis=1)
    return packed.astype(BF16)


def _in_proj(x2, gain, w_packed, tm=512):
    n, d = x2.shape
    out_shape = tuple(jax.ShapeDtypeStruct((n, w), dt) for w, dt in zip(SEG_WIDTHS, SEG_DTYPES))
    return pl.pallas_call(
        _inproj_kernel,
        grid=(n // tm,),
        in_specs=[pl.BlockSpec((tm, d), lambda i: (i, 0)),
                  _const_spec((1, d)),
                  _const_spec((d, PACKED_W))],
        out_specs=tuple(pl.BlockSpec((tm, w), lambda i: (i, 0)) for w in SEG_WIDTHS),
        out_shape=out_shape,
        scratch_shapes=[pltpu.VMEM((tm, d), BF16)],
        compiler_params=_cparams(("parallel",)),
        name="in_proj",
    )(x2, gain.reshape(1, d), w_packed)


def _compress_kernel(x_ref, w1_ref, w2_ref, w2t_ref, pos_ref, gain_ref, o_ref, ot_ref):
    kv = pl.program_id(1)
    x = x_ref[0, 0, 0]
    half = x.shape[1]
    h1 = _dot(x, w1_ref[0, :half, :])
    h2 = _dot(x, w1_ref[0, half:, :])
    nrow = x.shape[0]
    h2 = pltpu.roll(h2, nrow - 1, 0)
    bias = _dot(pos_ref[0], w1_ref[0])[0:1]
    hid = _silu(h1 + h2 + bias)
    out = _dot(hid.astype(BF16), w2_ref[0])
    normed = _rms(out, gain_ref[...])
    o_ref[0, 0, 0] = jnp.where(kv == 0, normed, out).astype(o_ref.dtype)
    ot_ref[0, 0, 0] = _dot_nt(w2t_ref[0], hid.astype(BF16)).astype(ot_ref.dtype)


def _compress(kvc, pos, w1, w2, kc_gain, b, s):
    nrow = s // CMP_STRIDE
    x = kvc.reshape(b, nrow, CMP_STRIDE, 2, NSA_GROUPS, HD).transpose(0, 3, 4, 1, 2, 5)
    x = x.reshape(b, 2, NSA_GROUPS, nrow, CMP_STRIDE * HD)
    cl = CMP_BLOCK * HD
    pos_flat = jnp.zeros((2, 8, cl), BF16).at[:, 0, :].set(pos.reshape(2, cl).astype(BF16))
    return pl.pallas_call(
        _compress_kernel,
        grid=(b, 2, NSA_GROUPS),
        in_specs=[pl.BlockSpec((1, 1, 1, nrow, CMP_STRIDE * HD), lambda i, k, g: (i, k, g, 0, 0)),
                  pl.BlockSpec((1, cl, CMP_HIDDEN), lambda i, k, g: (k, 0, 0)),
                  pl.BlockSpec((1, CMP_HIDDEN, HD), lambda i, k, g: (k, 0, 0)),
                  pl.BlockSpec((1, HD, CMP_HIDDEN), lambda i, k, g: (k, 0, 0)),
                  pl.BlockSpec((1, 8, cl), lambda i, k, g: (k, 0, 0)),
                  pl.BlockSpec((1, HD), lambda i, k, g: (0, 0))],
        out_specs=(pl.BlockSpec((1, 1, 1, nrow, HD), lambda i, k, g: (i, k, g, 0, 0)),
                   pl.BlockSpec((1, 1, 1, HD, nrow), lambda i, k, g: (i, k, g, 0, 0))),
        out_shape=(jax.ShapeDtypeStruct((b, 2, NSA_GROUPS, nrow, HD), BF16),
                   jax.ShapeDtypeStruct((b, 2, NSA_GROUPS, HD, nrow), BF16)),
        compiler_params=_cparams(("parallel", "parallel", "parallel")),
        name="nsa_compress",
    )(x, w1.astype(BF16), w2.astype(BF16), w2.transpose(0, 2, 1).astype(BF16), pos_flat, kc_gain.reshape(1, HD))


V_ROWS = HD + 16


def _kvprep_kernel(x_ref, gs_ref, gw_ref, ksel_ref, vsel_ref, kwin_ref, vwin_ref):
    ts = x_ref.shape[0]
    base = pl.program_id(1) * ts
    x = x_ref[...].astype(F32)
    n_hot = ksel_ref.shape[-1] - HD
    blk = (base + lax.broadcasted_iota(jnp.int32, (ts, n_hot), 0)) // SEL_BLOCK
    onehot = jnp.where(blk == lax.broadcasted_iota(jnp.int32, (ts, n_hot), 1), 1.0, 0.0)
    ones_row = jnp.where(lax.broadcasted_iota(jnp.int32, (V_ROWS - HD, ts), 0) == 0, 1.0, 0.0)
    vs_t = x[:, LANE:2 * LANE].T
    vw_t = x[:, 3 * LANE:4 * LANE].T
    for g in range(NSA_GROUPS):
        col = lambda j: x[:, j * LANE + g * HD:j * LANE + (g + 1) * HD]
        ksel_ref[0, g] = jnp.concatenate([_rms(col(0), gs_ref[...]), onehot], axis=1).astype(BF16)
        kwin_ref[0, g] = _rms(col(2), gw_ref[...]).astype(BF16)
        vsel_ref[0, g] = jnp.concatenate([vs_t[g * HD:(g + 1) * HD], ones_row], axis=0).astype(BF16)
        vwin_ref[0, g] = jnp.concatenate([vw_t[g * HD:(g + 1) * HD], ones_row], axis=0).astype(BF16)


def _kv_prep(kvsw, ks_gain, kw_gain, b, s, ts=1024):
    ts = min(ts, s)
    g = NSA_GROUPS
    aug_w = max(LANE, HD + s // SEL_BLOCK)
    spec = lambda w: pl.BlockSpec((1, g, ts, w), lambda i, j: (i, 0, j, 0))
    spec_t = pl.BlockSpec((1, g, V_ROWS, ts), lambda i, j: (i, 0, 0, j))
    shape_t = jax.ShapeDtypeStruct((b, g, V_ROWS, s), BF16)
    return pl.pallas_call(
        _kvprep_kernel,
        grid=(b, s // ts),
        in_specs=[pl.BlockSpec((ts, KVSW_W), lambda i, j: (i * (s // ts) + j, 0)),
                  pl.BlockSpec((1, HD), lambda i, j: (0, 0)),
                  pl.BlockSpec((1, HD), lambda i, j: (0, 0))],
        out_specs=(spec(aug_w), spec_t, spec(HD), spec_t),
        out_shape=(jax.ShapeDtypeStruct((b, g, s, aug_w), BF16), shape_t,
                   jax.ShapeDtypeStruct((b, g, s, HD), BF16), shape_t),
        compiler_params=_cparams(("parallel", "parallel")),
        name="nsa_kv_prep",
    )(kvsw, ks_gain.reshape(1, HD), kw_gain.reshape(1, HD))


def _flash_update(k, q_t, v_t, mask, m_ref, acc_ref):
    s = _dot(k, q_t)
    if mask is not None:
        s = jnp.where(mask, s, NEG)
    m_old = m_ref[...]
    m_new = jnp.maximum(m_old, jnp.max(s, axis=0, keepdims=True))
    alpha = jnp.exp(m_old - m_new)
    p = jnp.exp(s - m_new).astype(BF16)
    acc_ref[...] = alpha * acc_ref[...] + _dot(v_t, p)
    m_ref[...] = m_new


def _flash_reset(m_ref, acc_ref):
    m_ref[...] = jnp.full(m_ref.shape, NEG, F32)
    acc_ref[...] = jnp.zeros(acc_ref.shape, F32)


def _nsa_kernel(q_ref, gl_ref, kc_ref, vct_ref, ksel_ref, vsel_ref, kwin_ref, vwin_ref, ovl_ref, qg_ref,
                o_ref, qt_ref, imp_ref, m_ref, acc_ref, oc_ref, os_ref, *, tq):
    i = pl.program_id(2)
    lanes = NSA_E * tq
    n_sel = ovl_ref.shape[0]
    scale = HD ** -0.5

    q_t = q_ref[...].astype(F32).T
    for e in range(NSA_E):
        qe = q_t[e * HD:(e + 1) * HD, :]
        inv = lax.rsqrt(jnp.mean(qe * qe, axis=0, keepdims=True) + RMS_EPS)
        qt_ref[0:HD, e * tq:(e + 1) * tq] = (qe * inv * (qg_ref[...] * scale)).astype(BF16)
    q64 = qt_ref[0:HD, :]

    r_in = lax.broadcasted_iota(jnp.int32, (1, lanes), 1) & (tq - 1)
    t_lane = i * tq + r_in

    n_cmp = kc_ref.shape[3]
    s_c = _dot(kc_ref[0, 0, 0], q64)
    c_end = lax.broadcasted_iota(jnp.int32, (n_cmp, 1), 0) * CMP_STRIDE + (CMP_BLOCK - 1)
    s_c = jnp.where(c_end <= t_lane, s_c, NEG)
    e_c = jnp.exp(s_c - jnp.max(s_c, axis=0, keepdims=True))
    p_c = e_c / jnp.sum(e_c, axis=0, keepdims=True)
    p_c = jnp.where(t_lane >= CMP_BLOCK - 1, p_c, 0.0)
    oc_ref[...] = _dot(vct_ref[0, 0, 0], p_c.astype(BF16))

    p_sum = p_c[:, 0:tq]
    for e in range(1, NSA_E):
        p_sum = p_sum + p_c[:, e * tq:(e + 1) * tq]
    p_hi = p_sum.astype(BF16)
    p_lo = (p_sum - p_hi.astype(F32)).astype(BF16)
    imp = _dot(ovl_ref[...], p_hi) + _dot(ovl_ref[...], p_lo)
    j_idx = lax.broadcasted_iota(jnp.int32, (n_sel, tq), 0)
    cur = (i * tq + lax.broadcasted_iota(jnp.int32, (n_sel, tq), 1)) // SEL_BLOCK
    forced = (j_idx == 0) | (j_idx == cur) | (j_idx == cur - 1)
    valid = j_idx <= cur
    imp = jnp.where(valid, imp + jnp.where(forced, FORCE_BONUS, 0.0), -jnp.inf)
    imp_ref[...] = imp
    rank = jnp.zeros((n_sel, tq), F32)
    for j2 in range(n_sel):
        other = imp_ref[j2:j2 + 1, :]
        before = (other > imp) | ((other == imp) & (j_idx > j2))
        rank = rank + jnp.where(before, 1.0, 0.0)
    sel = (rank < min(TOP_N, n_sel)) & valid
    sel_bias = jnp.where(sel, 0.0, NEG).astype(BF16)
    for e in range(NSA_E):
        qt_ref[HD:HD + n_sel, e * tq:(e + 1) * tq] = sel_bias
    if HD + n_sel < qt_ref.shape[0]:
        qt_ref[HD + n_sel:, :] = jnp.zeros((qt_ref.shape[0] - HD - n_sel, lanes), BF16)

    key_in = lax.broadcasted_iota(jnp.int32, (tq, 1), 0)
    causal = key_in <= r_in
    start = pl.multiple_of(i * tq, tq)

    _flash_reset(m_ref, acc_ref)
    _flash_update(ksel_ref[0, 0, pl.ds(start, tq), :], qt_ref[...], vsel_ref[0, 0, :, pl.ds(start, tq)],
                  causal, m_ref, acc_ref)

    def sel_body(kt, carry):
        ks = pl.multiple_of(kt * tq, tq)
        _flash_update(ksel_ref[0, 0, pl.ds(ks, tq), :], qt_ref[...], vsel_ref[0, 0, :, pl.ds(ks, tq)],
                      None, m_ref, acc_ref)
        return carry

    lax.fori_loop(0, i, sel_body, 0)
    os_ref[...] = acc_ref[0:HD, :] / acc_ref[HD:HD + 1, :]

    _flash_reset(m_ref, acc_ref)
    _flash_update(kwin_ref[0, 0, pl.ds(start, tq), :], q64, vwin_ref[0, 0, :, pl.ds(start, tq)],
                  causal, m_ref, acc_ref)
    for back in range(1, WINDOW // tq + 1):
        @pl.when(i >= back)
        def _():
            ks = pl.multiple_of((i - back) * tq, tq)
            dist = back * tq + r_in - key_in
            mask = (dist < WINDOW) if (back + 1) * tq > WINDOW else None
            _flash_update(kwin_ref[0, 0, pl.ds(ks, tq), :], q64, vwin_ref[0, 0, :, pl.ds(ks, tq)],
                          mask, m_ref, acc_ref)
    o_w = acc_ref[0:HD, :] / acc_ref[HD:HD + 1, :]

    gate = jax.nn.sigmoid(gl_ref[...].T)
    ys = []
    for e in range(NSA_E):
        cs = slice(e * tq, (e + 1) * tq)
        ys.append(gate[3 * e:3 * e + 1, :] * oc_ref[:, cs] + gate[3 * e + 1:3 * e + 2, :] * os_ref[:, cs]
                  + gate[3 * e + 2:3 * e + 3, :] * o_w[:, cs])
    o_ref[...] = jnp.concatenate(ys, axis=0).T.astype(o_ref.dtype)


def _nsa_attention(q, gl, cmp_kv, cmp_t, ksel, vsel, kwin, vwin, q_gain, b, s, tq=256):
    tq = min(tq, s)
    nq = s // tq
    n_sel = s // SEL_BLOCK
    nrow = s // CMP_STRIDE
    g = NSA_GROUPS
    c0 = jnp.arange(nrow) * CMP_STRIDE
    j0 = jnp.arange(n_sel) * SEL_BLOCK
    ovl = ((c0[None, :] <= j0[:, None] + SEL_BLOCK - 1) & (c0[None, :] + CMP_BLOCK - 1 >= j0[:, None]))
    ovl = ovl.astype(BF16)
    aug_w = max(2 * HD, HD + n_sel)
    assert ksel.shape[-1] == aug_w
    lanes = NSA_E * tq
    kv_spec = lambda w: pl.BlockSpec((1, 1, s, w), lambda bi, gi, i: (bi, gi, 0, 0))
    vt_spec = pl.BlockSpec((1, 1, V_ROWS, s), lambda bi, gi, i: (bi, gi, 0, 0))
    return pl.pallas_call(
        functools.partial(_nsa_kernel, tq=tq),
        grid=(b, g, nq),
        in_specs=[pl.BlockSpec((tq, NSA_E * HD), lambda bi, gi, i: (bi * nq + i, gi)),
                  pl.BlockSpec((tq, LANE), lambda bi, gi, i: (bi * nq + i, gi)),
                  pl.BlockSpec((1, 1, 1, nrow, HD), lambda bi, gi, i: (bi, 0, gi, 0, 0)),
                  pl.BlockSpec((1, 1, 1, HD, nrow), lambda bi, gi, i: (bi, 1, gi, 0, 0)),
                  kv_spec(aug_w), vt_spec, kv_spec(HD), vt_spec,
                  pl.BlockSpec((n_sel, nrow), lambda bi, gi, i: (0, 0)),
                  pl.BlockSpec((HD, 1), lambda bi, gi, i: (0, 0))],
        out_specs=pl.BlockSpec((tq, NSA_E * HD), lambda bi, gi, i: (bi * nq + i, gi)),
        out_shape=jax.ShapeDtypeStruct((b * s, NSA_HEADS * HD), BF16),
        scratch_shapes=[pltpu.VMEM((aug_w, lanes), BF16),
                        pltpu.VMEM((n_sel, tq), F32),
                        pltpu.VMEM((1, lanes), F32),
                        pltpu.VMEM((V_ROWS, lanes), F32),
                        pltpu.VMEM((HD, lanes), F32),
                        pltpu.VMEM((HD, lanes), F32)],
        compiler_params=_cparams(("parallel", "parallel", "arbitrary")),
        name="nsa_attention",
    )(q, gl, cmp_kv, cmp_t, ksel, vsel, kwin, vwin, ovl, q_gain.reshape(HD, 1))


def _softplus(x):
    return jnp.maximum(x, 0.0) + jnp.log1p(jnp.exp(-jnp.abs(x)))


def _dot_exact(a, b):
    return jnp.dot(a, b, precision=lax.Precision.HIGHEST, preferred_element_type=F32)


def _ssd_kernel(z_ref, xbc_ref, dt_ref, dtT_ref, cw_ref, cb_ref, dtb_ref, dtbT_ref, alog_ref, alogT_ref,
                dskip_ref, ng_ref, exp_ref, o_ref, xpad_ref, state_ref, y_ref):
    c = pl.program_id(1)
    L = xbc_ref.shape[0]
    gw = SSM_INNER // SSM_GROUPS
    e_per_g = SSM_HEADS // SSM_GROUPS

    @pl.when(c == 0)
    def _():
        xpad_ref[0:8, :] = jnp.zeros((8, SSM_CONV_DIM), F32)
        state_ref[...] = jnp.zeros(state_ref.shape, F32)

    xpad_ref[8:8 + L, :] = xbc_ref[...].astype(F32)
    acc = cb_ref[...] + cw_ref[0:1, :] * xpad_ref[5:5 + L, :]
    for k in range(1, SSM_CONV):
        acc = acc + cw_ref[k:k + 1, :] * xpad_ref[5 + k:5 + k + L, :]
    xpad_ref[0:8, :] = xpad_ref[L:L + 8, :]
    xact = _silu(acc)
    xs = xact[:, :SSM_INNER]

    li = lax.broadcasted_iota(jnp.int32, (L, L), 0)
    si = lax.broadcasted_iota(jnp.int32, (L, L), 1)
    causal = li >= si
    dt = _softplus(dt_ref[...] + dtb_ref[...])
    a_cs = _dot_exact(jnp.where(causal, 1.0, 0.0), dt * -jnp.exp(alog_ref[...]))
    dtT = _softplus(dtT_ref[0] + dtbT_ref[...])
    a_csT = _dot_exact(dtT * -jnp.exp(alogT_ref[...]), jnp.where(li <= si, 1.0, 0.0))

    dt_x = _dot_exact(dt, exp_ref[...])
    acs_x = _dot_exact(a_cs, exp_ref[...])
    alast_x = acs_x[L - 1:L, :]
    xd = xs * dt_x
    xdd = (xd * jnp.exp(alast_x - acs_x)).astype(BF16)
    eacs_x = jnp.exp(acs_x)
    xd = xd.astype(BF16)

    for g in range(SSM_GROUPS):
        gs = slice(g * gw, (g + 1) * gw)
        bg = xact[:, SSM_INNER + g * SSM_N:SSM_INNER + (g + 1) * SSM_N].astype(BF16)
        cg = xact[:, SSM_INNER + (SSM_GROUPS + g) * SSM_N:SSM_INNER + (SSM_GROUPS + g + 1) * SSM_N].astype(BF16)
        cb = _dot_nt(cg, bg)
        st = state_ref[g]
        y_ref[:, gs] = _dot(cg, st.astype(BF16)) * eacs_x[:, gs] + xs[:, gs] * dskip_ref[:, gs]
        for e in range(e_per_g):
            h = g * e_per_g + e
            hs = slice(h * SSM_P, (h + 1) * SSM_P)
            seg = jnp.exp(jnp.where(causal, a_cs[:, h:h + 1] - a_csT[h:h + 1, :], NEG))
            y_ref[:, hs] += _dot((cb * seg).astype(BF16), xd[:, hs])
        upd = lax.dot_general(bg, xdd[:, gs], (((0,), (0,)), ((), ())), preferred_element_type=F32)
        state_ref[g] = st * jnp.exp(alast_x[:, gs]) + upd

    y = y_ref[...] * _silu(z_ref[...].astype(F32))
    for g in range(SSM_GROUPS):
        gs = slice(g * gw, (g + 1) * gw)
        o_ref[:, gs] = _rms(y[:, gs], ng_ref[:, gs]).astype(o_ref.dtype)


def _ssd_mixer(z, xbc, dt, conv_w, conv_b, dt_bias, a_log, d_skip, norm_gain, b, s):
    L = math.gcd(SSM_CHUNK, s)
    nc = s // L
    h = SSM_HEADS
    dtT = dt[:, :h].reshape(b, s, h).transpose(0, 2, 1)
    pad_l = lambda v: jnp.zeros((1, LANE), F32).at[0, :h].set(v)
    expand = jnp.zeros((LANE, SSM_INNER), F32).at[:h].set(jnp.repeat(jnp.eye(h, dtype=F32), SSM_P, axis=1))
    row = lambda w: pl.BlockSpec((L, w), lambda bi, ci: (bi * nc + ci, 0))
    full = lambda shp: pl.BlockSpec(shp, lambda bi, ci: (0,) * len(shp))
    return pl.pallas_call(
        _ssd_kernel,
        grid=(b, nc),
        in_specs=[row(SSM_INNER), row(SSM_CONV_DIM), row(LANE),
                  pl.BlockSpec((1, h, L), lambda bi, ci: (bi, 0, ci)),
                  full((SSM_CONV, SSM_CONV_DIM)), full((1, SSM_CONV_DIM)),
                  full((1, LANE)), full((h, 1)), full((1, LANE)), full((h, 1)),
                  full((1, SSM_INNER)), full((1, SSM_INNER)), full((LANE, SSM_INNER))],
        out_specs=row(SSM_INNER),
        out_shape=jax.ShapeDtypeStruct((b * s, SSM_INNER), BF16),
        scratch_shapes=[pltpu.VMEM((L + 8, SSM_CONV_DIM), F32),
                        pltpu.VMEM((SSM_GROUPS, SSM_N, SSM_INNER // SSM_GROUPS), F32),
                        pltpu.VMEM((L, SSM_INNER), F32)],
        compiler_params=_cparams(("parallel", "arbitrary")),
        name="ssd_mixer",
    )(z, xbc, dt, dtT, conv_w, conv_b.reshape(1, -1), pad_l(dt_bias), dt_bias.reshape(h, 1),
      pad_l(a_log), a_log.reshape(h, 1), jnp.repeat(d_skip, SSM_P).reshape(1, -1),
      norm_gain.reshape(1, -1), expand)


ROUTE_ROWS = 8 + MOE_EXPERTS


def _merge_kernel(x_ref, ya_ref, yb_ref, brg_ref, wa_ref, wb_ref, wo_ref, mg_ref, wr_ref, br_ref,
                  x1_ref, h2_ref, eid_ref, ew_ref):
    tm = x_ref.shape[0]
    gate = jax.nn.sigmoid(brg_ref[...].astype(F32))
    merged = gate[:, :D_MODEL] * _dot(ya_ref[...], wa_ref[...]) + gate[:, D_MODEL:] * _dot(yb_ref[...], wb_ref[...])
    x1 = x_ref[...] + _dot(merged.astype(BF16), wo_ref[...])
    x1_ref[...] = x1
    h2 = _rms(x1, mg_ref[...])
    h2_ref[...] = h2

    lg = _dot_nt(wr_ref[...], h2.astype(BF16)) + br_ref[...]
    r = lax.broadcasted_iota(jnp.int32, (ROUTE_ROWS, tm), 0)
    big = ROUTE_ROWS
    is_g = r < MOE_GROUPS
    gmax = jnp.max(jnp.where(is_g, lg, -jnp.inf), axis=0, keepdims=True)
    gsum = jnp.sum(jnp.where(is_g, jnp.exp(lg - gmax), 0.0), axis=0, keepdims=True)
    grp_w = 1.0 / gsum
    grp_i = jnp.min(jnp.where(is_g & (lg == gmax), r, big), axis=0, keepdims=True)
    is_e = (r >= 8) & (((r - 8) // MOE_EPG) == grp_i)
    le = jnp.where(is_e, lg, -jnp.inf)
    emax = jnp.max(le, axis=0, keepdims=True)
    esum = jnp.sum(jnp.where(is_e, jnp.exp(lg - emax), 0.0), axis=0, keepdims=True)
    i1 = jnp.min(jnp.where(le == emax, r, big), axis=0, keepdims=True)
    le2 = jnp.where(r == i1, -jnp.inf, le)
    e2max = jnp.max(le2, axis=0, keepdims=True)
    i2 = jnp.min(jnp.where((le2 == e2max) & is_e & (r != i1), r, big), axis=0, keepdims=True)
    p1 = 1.0 / esum
    p2 = jnp.exp(e2max - emax) / esum
    w1 = grp_w * p1 / (p1 + p2)
    w2 = grp_w * p2 / (p1 + p2)
    r8 = lax.broadcasted_iota(jnp.int32, (8, tm), 0)
    eid_ref[...] = jnp.where(r8 == 0, i1 - 8, jnp.where(r8 == 1, i2 - 8, 0))
    ew_ref[...] = jnp.where(r8 == 0, w1, jnp.where(r8 == 1, w2, 0.0))


def _merge_route(x2, y_a, y_b, brg, w_a, w_b, w_o, moe_gain, w_group, b_group, w_expert, b_expert, tm=512):
    n, d = x2.shape
    tm = min(tm, n)
    wr = jnp.zeros((ROUTE_ROWS, d), F32).at[:MOE_GROUPS].set(w_group.T).at[8:].set(w_expert.T).astype(BF16)
    br = jnp.zeros((ROUTE_ROWS, 1), F32).at[:MOE_GROUPS, 0].set(b_group).at[8:, 0].set(b_expert)
    row = lambda w: pl.BlockSpec((tm, w), lambda i: (i, 0))
    colblk = pl.BlockSpec((8, tm), lambda i: (0, i))
    return pl.pallas_call(
        _merge_kernel,
        grid=(n // tm,),
        in_specs=[row(d), row(Q_W), row(SSM_INNER), row(BRG_W),
                  _const_spec((Q_W, d)), _const_spec((SSM_INNER, d)), _const_spec((d, d)),
                  _const_spec((1, d)), _const_spec((ROUTE_ROWS, d)), _const_spec((ROUTE_ROWS, 1))],
        out_specs=(row(d), row(d), colblk, colblk),
        out_shape=(jax.ShapeDtypeStruct((n, d), F32), jax.ShapeDtypeStruct((n, d), F32),
                   jax.ShapeDtypeStruct((8, n), jnp.int32), jax.ShapeDtypeStruct((8, n), F32)),
        compiler_params=_cparams(("parallel",)),
        name="merge_route",
    )(x2, y_a, y_b, brg, w_a.astype(BF16), w_b.astype(BF16), w_o.astype(BF16), moe_gain.reshape(1, d), wr, br)


def _rank_kernel(eid_ref, rank_ref, cnt_ref, run_ref, tri_ref):
    i = pl.program_id(0)
    tm = eid_ref.shape[1]

    @pl.when(i == 0)
    def _():
        run_ref[...] = jnp.zeros(run_ref.shape, F32)
        earlier = lax.broadcasted_iota(jnp.int32, (tm, tm), 0) < lax.broadcasted_iota(jnp.int32, (tm, tm), 1)
        tri_ref[...] = jnp.where(earlier, 1.0, 0.0).astype(BF16)

    r = lax.broadcasted_iota(jnp.int32, (MOE_EXPERTS, tm), 0)
    oh0 = r == eid_ref[0:1, :]
    oh1 = r == eid_ref[1:2, :]
    oh = jnp.where(oh0 | oh1, 1.0, 0.0)
    prefix = _dot(oh.astype(BF16), tri_ref[...]) + run_ref[:, 0:1]
    rank0 = jnp.sum(jnp.where(oh0, prefix, 0.0), axis=0, keepdims=True)
    rank1 = jnp.sum(jnp.where(oh1, prefix, 0.0), axis=0, keepdims=True)
    r8 = lax.broadcasted_iota(jnp.int32, (8, tm), 0)
    rank_ref[...] = jnp.where(r8 == 0, rank0, jnp.where(r8 == 1, rank1, 0.0)).astype(jnp.int32)
    run_ref[...] = run_ref[...] + jnp.sum(oh, axis=1, keepdims=True)
    cnt_ref[...] = run_ref[...]


def _moe_rank(eid, tm=512):
    n = eid.shape[1]
    tm = min(tm, n)
    return pl.pallas_call(
        _rank_kernel,
        grid=(n // tm,),
        in_specs=[pl.BlockSpec((8, tm), lambda i: (0, i))],
        out_specs=(pl.BlockSpec((8, tm), lambda i: (0, i)),
                   pl.BlockSpec((MOE_EXPERTS, LANE), lambda i: (0, 0))),
        out_shape=(jax.ShapeDtypeStruct((8, n), jnp.int32),
                   jax.ShapeDtypeStruct((MOE_EXPERTS, LANE), F32)),
        scratch_shapes=[pltpu.VMEM((MOE_EXPERTS, LANE), F32), pltpu.VMEM((tm, tm), BF16)],
        compiler_params=_cparams(("arbitrary",)),
        name="moe_rank",
    )(eid)


def _row_copy(src, dst, sem):
    return pltpu.make_async_copy(src, dst, sem)


def _dispatch_kernel(slot_ref, h_ref, xs_in_ref, xs_ref, sem):
    del xs_in_ref
    tm = h_ref.shape[0]

    def issue(n, carry):
        for k in range(MOE_TOP_K):
            _row_copy(h_ref.at[pl.ds(n, 1), :], xs_ref.at[pl.ds(slot_ref[k, n], 1), :], sem).start()
        return carry

    lax.fori_loop(0, tm, issue, 0)

    def drain(n, carry):
        for k in range(MOE_TOP_K):
            _row_copy(h_ref.at[pl.ds(0, 1), :], xs_ref.at[pl.ds(0, 1), :], sem).wait()
        return carry

    lax.fori_loop(0, tm, drain, 0)


def _moe_dispatch(slot, h2, n_slots, tm=256):
    n, d = h2.shape
    tm = min(tm, n)
    xs0 = jnp.zeros((n_slots, d), h2.dtype)
    return pl.pallas_call(
        _dispatch_kernel,
        grid=(n // tm,),
        in_specs=[pl.BlockSpec((MOE_TOP_K, tm), lambda i: (0, i), memory_space=pltpu.SMEM),
                  pl.BlockSpec((tm, d), lambda i: (i, 0)),
                  pl.BlockSpec(memory_space=pl.ANY)],
        out_specs=pl.BlockSpec(memory_space=pl.ANY),
        out_shape=jax.ShapeDtypeStruct((n_slots, d), h2.dtype),
        scratch_shapes=[pltpu.SemaphoreType.DMA],
        input_output_aliases={2: 0},
        compiler_params=_cparams(("arbitrary",)),
        name="moe_dispatch",
    )(slot, h2, xs0)


def _expert_kernel(be_ref, nu_ref, xs_ref, wg_ref, wu_ref, wd_ref, o_ref):
    j = pl.program_id(0)

    @pl.when(j < nu_ref[0])
    def _():
        x = xs_ref[...].astype(BF16)
        act = _silu(_dot(x, wg_ref[0])) * _dot(x, wu_ref[0])
        o_ref[...] = _dot(act.astype(BF16), wd_ref[0])

    @pl.when(j >= nu_ref[0])
    def _():
        o_ref[...] = jnp.zeros(o_ref.shape, o_ref.dtype)


def _moe_experts(block_e, n_used, xs, w_gate, w_up, w_down):
    n_slots, d = xs.shape
    rb = MOE_ROW_BLOCK
    wspec = lambda shp: pl.BlockSpec((1,) + shp, lambda j, be, nu: (be[j], 0, 0))
    return pl.pallas_call(
        _expert_kernel,
        grid_spec=pltpu.PrefetchScalarGridSpec(
            num_scalar_prefetch=2,
            grid=(n_slots // rb,),
            in_specs=[pl.BlockSpec((rb, d), lambda j, be, nu: (j, 0)),
                      wspec((d, MOE_HIDDEN)), wspec((d, MOE_HIDDEN)), wspec((MOE_HIDDEN, d))],
            out_specs=pl.BlockSpec((rb, d), lambda j, be, nu: (j, 0))),
        out_shape=jax.ShapeDtypeStruct((n_slots, d), F32),
        compiler_params=_cparams(("arbitrary",)),
        name="moe_experts",
    )(block_e, n_used, xs, w_gate.astype(BF16), w_up.astype(BF16), w_down.astype(BF16))


def _combine_kernel(slot_ref, ew_ref, x1_ref, p_ref, eo_ref, pg_ref, wg_ref, wp_ref, o_ref, rows_ref, sem):
    tm = x1_ref.shape[0]

    def issue(n, carry):
        for k in range(MOE_TOP_K):
            _row_copy(eo_ref.at[pl.ds(slot_ref[k, n], 1), :], rows_ref.at[k, pl.ds(n, 1), :], sem).start()
        return carry

    lax.fori_loop(0, tm, issue, 0)

    def drain(n, carry):
        for k in range(MOE_TOP_K):
            _row_copy(eo_ref.at[pl.ds(0, 1), :], rows_ref.at[k, pl.ds(0, 1), :], sem).wait()
        return carry

    lax.fori_loop(0, tm, drain, 0)

    x2 = x1_ref[...] + ew_ref[:, 0:1] * rows_ref[0] + ew_ref[:, 1:2] * rows_ref[1]
    h3 = _rms(x2, pg_ref[...]).astype(BF16)
    o_ref[...] = x2 + jax.nn.sigmoid(_dot(h3, wg_ref[...])) * _dot(p_ref[...].astype(BF16), wp_ref[...])


def _moe_combine_ple(slot, ew_cols, x1, p2, expert_out, ple_gain, w_gate, w_proj, tm=256):
    n, d = x1.shape
    tm = min(tm, n)
    row = lambda w: pl.BlockSpec((tm, w), lambda i: (i, 0))
    return pl.pallas_call(
        _combine_kernel,
        grid=(n // tm,),
        in_specs=[pl.BlockSpec((MOE_TOP_K, tm), lambda i: (0, i), memory_space=pltpu.SMEM),
                  row(8), row(d), row(PLE_DIM),
                  pl.BlockSpec(memory_space=pl.ANY),
                  _const_spec((1, d)), _const_spec((d, d)), _const_spec((PLE_DIM, d))],
        out_specs=row(d),
        out_shape=jax.ShapeDtypeStruct((n, d), F32),
        scratch_shapes=[pltpu.VMEM((MOE_TOP_K, tm, d), F32), pltpu.SemaphoreType.DMA],
        compiler_params=_cparams(("arbitrary",)),
        name="moe_combine_ple",
    )(slot, ew_cols, x1, p2, expert_out, ple_gain.reshape(1, d), w_gate.astype(BF16), w_proj.astype(BF16))


def _layer(x2, p2, b, s, mix_norm, w_in, nsa_q_gain, nsa_kc_gain, nsa_ks_gain, nsa_kw_gain,
           cmp_pos_k, cmp_w1_k, cmp_w2_k, cmp_pos_v, cmp_w1_v, cmp_w2_v,
           ssm_conv_w, ssm_conv_b, ssm_dt_bias, ssm_a_log, ssm_d, ssm_norm,
           w_branch_a, w_branch_b, w_out,
           moe_norm, moe_w_group, moe_b_group, moe_w_expert, moe_b_expert,
           moe_w_gate, moe_w_up, moe_w_down, ple_norm, ple_w_gate, ple_w_proj):
    n = b * s
    q, kvc, kvsw, gl, z, xbc, dt, brg = _in_proj(x2, mix_norm, _pack_w_in(w_in))

    cmp_kv, cmp_t = _compress(kvc, jnp.stack([cmp_pos_k, cmp_pos_v]), jnp.stack([cmp_w1_k, cmp_w1_v]),
                              jnp.stack([cmp_w2_k, cmp_w2_v]), nsa_kc_gain, b, s)
    ksel, vsel, kwin, vwin = _kv_prep(kvsw, nsa_ks_gain, nsa_kw_gain, b, s)
    y_a = _nsa_attention(q, gl, cmp_kv, cmp_t, ksel, vsel, kwin, vwin, nsa_q_gain, b, s)
    y_b = _ssd_mixer(z, xbc, dt, ssm_conv_w, ssm_conv_b, ssm_dt_bias, ssm_a_log, ssm_d, ssm_norm, b, s)

    x1, h2, eid, ew = _merge_route(x2, y_a, y_b, brg, w_branch_a, w_branch_b, w_out, moe_norm,
                                   moe_w_group, moe_b_group, moe_w_expert, moe_b_expert)

    rank, cnt = _moe_rank(eid)
    rb = MOE_ROW_BLOCK
    counts = cnt[:, 0].astype(jnp.int32)
    padded = (counts + rb - 1) // rb * rb
    pend = jnp.cumsum(padded)
    pstart = pend - padded
    e_ids = jnp.arange(MOE_EXPERTS, dtype=jnp.int32)
    is_e = eid[None, :MOE_TOP_K] == e_ids[:, None, None]
    slot = rank[:MOE_TOP_K] + jnp.sum(jnp.where(is_e, pstart[:, None, None], 0), axis=0)
    n_blocks = -(-(n * MOE_TOP_K) // rb) + MOE_EXPERTS
    first_row = jnp.arange(n_blocks, dtype=jnp.int32) * rb
    block_e = jnp.minimum(jnp.sum(pend[None, :] <= first_row[:, None], axis=1), MOE_EXPERTS - 1)
    n_used = (pend[-1:] // rb).astype(jnp.int32)

    xs = _moe_dispatch(slot, h2, n_blocks * rb)
    expert_out = _moe_experts(block_e.astype(jnp.int32), n_used, xs, moe_w_gate, moe_w_up, moe_w_down)
    ew_cols = ew.T
    return _moe_combine_ple(slot, ew_cols, x1, p2, expert_out, ple_norm, ple_w_gate, ple_w_proj)


def kernel(x, p, mix_norm, w_in, nsa_q_gain, nsa_kc_gain, nsa_ks_gain, nsa_kw_gain, cmp_pos_k, cmp_w1_k, cmp_w2_k, cmp_pos_v, cmp_w1_v, cmp_w2_v, ssm_conv_w, ssm_conv_b, ssm_dt_bias, ssm_a_log, ssm_d, ssm_norm, w_branch_a, w_branch_b, w_out, moe_norm, moe_w_group, moe_b_group, moe_w_expert, moe_b_expert, moe_w_gate, moe_w_up, moe_w_down, ple_norm, ple_w_gate, ple_w_proj):
    b, s, d = x.shape
    params = (mix_norm, w_in, nsa_q_gain, nsa_kc_gain, nsa_ks_gain, nsa_kw_gain,
              cmp_pos_k, cmp_w1_k, cmp_w2_k, cmp_pos_v, cmp_w1_v, cmp_w2_v,
              ssm_conv_w, ssm_conv_b, ssm_dt_bias, ssm_a_log, ssm_d, ssm_norm,
              w_branch_a, w_branch_b, w_out,
              moe_norm, moe_w_group, moe_b_group, moe_w_expert, moe_b_expert,
              moe_w_gate, moe_w_up, moe_w_down, ple_norm, ple_w_gate, ple_w_proj)
    x2 = x.reshape(b * s, d)
    for i in range(p.shape[0]):
        x2 = _layer(x2, p[i].reshape(b * s, -1), b, s, *(w[i] for w in params))
    return x2.reshape(b, s, d)
```

```python
import functools
import math

import jax
import jax.numpy as jnp
from jax import lax
from jax.experimental import pallas as pl
from jax.experimental.pallas import tpu as pltpu

F32 = jnp.float32
BF16 = jnp.bfloat16

D_MODEL = 1024
RMS_EPS = 1e-6
NEG = -1e30

NSA_HEADS = 8
NSA_GROUPS = 2
NSA_E = NSA_HEADS // NSA_GROUPS
HD = 64
CMP_BLOCK = 32
CMP_STRIDE = 16
CMP_HIDDEN = 128
SEL_BLOCK = 64
TOP_N = 16
WINDOW = 512
FORCE_BONUS = 1e3

SSM_HEADS = 16
SSM_P = 64
SSM_INNER = SSM_HEADS * SSM_P
SSM_GROUPS = 2
SSM_N = 128
SSM_CONV = 4
SSM_CHUNK = 256
SSM_CONV_DIM = SSM_INNER + 2 * SSM_GROUPS * SSM_N

MOE_GROUPS = 4
MOE_EPG = 8
MOE_EXPERTS = MOE_GROUPS * MOE_EPG
MOE_TOP_K = 2
MOE_HIDDEN = 256
MOE_ROW_BLOCK = 256
PLE_DIM = 256

LANE = 128
VMEM_LIMIT = 52 * 1024 * 1024

Q_W = NSA_HEADS * HD
KVC_W = 2 * NSA_GROUPS * HD
KVSW_W = 4 * NSA_GROUPS * HD
GL_W = NSA_GROUPS * LANE
DT_W = LANE
BRG_W = 2 * D_MODEL
SEG_WIDTHS = (Q_W, KVC_W, KVSW_W, GL_W, SSM_INNER, SSM_CONV_DIM, DT_W, BRG_W)
SEG_DTYPES = (BF16, BF16, BF16, F32, BF16, BF16, F32, BF16)
PACKED_W = sum(SEG_WIDTHS)


def _cparams(sem):
    return pltpu.CompilerParams(dimension_semantics=sem, vmem_limit_bytes=VMEM_LIMIT)


def _const_spec(shape):
    n = len(shape)
    return pl.BlockSpec(shape, lambda *_: (0,) * n, pipeline_mode=pl.Buffered(1))


def _rms(xf, gain):
    return xf * lax.rsqrt(jnp.mean(xf * xf, axis=-1, keepdims=True) + RMS_EPS) * gain


def _silu(x):
    return x * jax.nn.sigmoid(x)


def _dot(a, b):
    return jnp.dot(a, b, preferred_element_type=F32)


def _dot_nt(a, b):
    return lax.dot_general(a, b, (((1,), (1,)), ((), ())), preferred_element_type=F32)


def _inproj_kernel(x_ref, g_ref, w_ref, *refs):
    outs, h_scr = refs[:-1], refs[-1]
    h_scr[...] = _rms(x_ref[...], g_ref[...]).astype(BF16)
    off = 0
    for o_ref, width in zip(outs, SEG_WIDTHS):
        for lo in range(0, width, 512):
            hi = min(lo + 512, width)
            o_ref[:, lo:hi] = _dot(h_scr[...], w_ref[:, off + lo:off + hi]).astype(o_ref.dtype)
        off += width


def _pack_w_in(w_in):
    sizes = (Q_W, 128, 128, 128, 128, 128, 128, NSA_HEADS * 3, SSM_INNER, SSM_CONV_DIM, SSM_HEADS, BRG_W)
    offs = [0]
    for s in sizes:
        offs.append(offs[-1] + s)
    seg = lambda i: w_in[:, offs[i]:offs[i + 1]]
    q, kc, vc, ks, vs, kw, vw, ng, z, xbc, dt, brg = (seg(i) for i in range(12))
    d = w_in.shape[0]
    pad = lambda a, w: jnp.concatenate([a, jnp.zeros((d, w - a.shape[1]), a.dtype)], axis=1)
    per_g = NSA_E * 3
    gl = jnp.concatenate([pad(ng[:, g * per_g:(g + 1) * per_g], LANE) for g in range(NSA_GROUPS)], axis=1)
    packed = jnp.concatenate([q, kc, vc, ks, vs, kw, vw, gl, z, xbc, pad(dt, DT_W), brg], axis=1)
    return packed.astype(BF16)


def _in_proj(x2, gain, w_packed, tm=512):
    n, d = x2.shape
    out_shape = tuple(jax.ShapeDtypeStruct((n, w), dt) for w, dt in zip(SEG_WIDTHS, SEG_DTYPES))
    return pl.pallas_call(
        _inproj_kernel,
        grid=(n // tm,),
        in_specs=[pl.BlockSpec((tm, d), lambda i: (i, 0)),
                  _const_spec((1, d)),
                  _const_spec((d, PACKED_W))],
        out_specs=tuple(pl.BlockSpec((tm, w), lambda i: (i, 0)) for w in SEG_WIDTHS),
        out_shape=out_shape,
        scratch_shapes=[pltpu.VMEM((tm, d), BF16)],
        compiler_params=_cparams(("parallel",)),
        name="in_proj",
    )(x2, gain.reshape(1, d), w_packed)


def _compress_kernel(x_ref, w1_ref, w2_ref, w2t_ref, pos_ref, gain_ref, o_ref, ot_ref):
    kv = pl.program_id(1)
    x = x_ref[0, 0, 0]
    half = x.shape[1]
    h1 = _dot(x, w1_ref[0, :half, :])
    h2 = _dot(x, w1_ref[0, half:, :])
    nrow = x.shape[0]
    h2 = pltpu.roll(h2, nrow - 1, 0)
    bias = _dot(pos_ref[0], w1_ref[0])[0:1]
    hid = _silu(h1 + h2 + bias)
    out = _dot(hid.astype(BF16), w2_ref[0])
    normed = _rms(out, gain_ref[...])
    o_ref[0, 0, 0] = jnp.where(kv == 0, normed, out).astype(o_ref.dtype)
    ot_ref[0, 0, 0] = _dot_nt(w2t_ref[0], hid.astype(BF16)).astype(ot_ref.dtype)


def _compress(kvc, pos, w1, w2, kc_gain, b, s):
    nrow = s // CMP_STRIDE
    x = kvc.reshape(b, nrow, CMP_STRIDE, 2, NSA_GROUPS, HD).transpose(0, 3, 4, 1, 2, 5)
    x = x.reshape(b, 2, NSA_GROUPS, nrow, CMP_STRIDE * HD)
    cl = CMP_BLOCK * HD
    pos_flat = jnp.zeros((2, 8, cl), BF16).at[:, 0, :].set(pos.reshape(2, cl).astype(BF16))
    return pl.pallas_call(
        _compress_kernel,
        grid=(b, 2, NSA_GROUPS),
        in_specs=[pl.BlockSpec((1, 1, 1, nrow, CMP_STRIDE * HD), lambda i, k, g: (i, k, g, 0, 0)),
                  pl.BlockSpec((1, cl, CMP_HIDDEN), lambda i, k, g: (k, 0, 0)),
                  pl.BlockSpec((1, CMP_HIDDEN, HD), lambda i, k, g: (k, 0, 0)),
                  pl.BlockSpec((1, HD, CMP_HIDDEN), lambda i, k, g: (k, 0, 0)),
                  pl.BlockSpec((1, 8, cl), lambda i, k, g: (k, 0, 0)),
                  pl.BlockSpec((1, HD), lambda i, k, g: (0, 0))],
        out_specs=(pl.BlockSpec((1, 1, 1, nrow, HD), lambda i, k, g: (i, k, g, 0, 0)),
                   pl.BlockSpec((1, 1, 1, HD, nrow), lambda i, k, g: (i, k, g, 0, 0))),
        out_shape=(jax.ShapeDtypeStruct((b, 2, NSA_GROUPS, nrow, HD), BF16),
                   jax.ShapeDtypeStruct((b, 2, NSA_GROUPS, HD, nrow), BF16)),
        compiler_params=_cparams(("parallel", "parallel", "parallel")),
        name="nsa_compress",
    )(x, w1.astype(BF16), w2.astype(BF16), w2.transpose(0, 2, 1).astype(BF16), pos_flat, kc_gain.reshape(1, HD))


V_ROWS = HD + 16


def _kvprep_kernel(x_ref, gs_ref, gw_ref, ksel_ref, vsel_ref, kwin_ref, vwin_ref):
    ts = x_ref.shape[0]
    base = pl.program_id(1) * ts
    x = x_ref[...].astype(F32)
    n_hot = ksel_ref.shape[-1] - HD
    blk = (base + lax.broadcasted_iota(jnp.int32, (ts, n_hot), 0)) // SEL_BLOCK
    onehot = jnp.where(blk == lax.broadcasted_iota(jnp.int32, (ts, n_hot), 1), 1.0, 0.0)
    ones_row = jnp.where(lax.broadcasted_iota(jnp.int32, (V_ROWS - HD, ts), 0) == 0, 1.0, 0.0)
    vs_t = x[:, LANE:2 * LANE].T
    vw_t = x[:, 3 * LANE:4 * LANE].T
    for g in range(NSA_GROUPS):
        col = lambda j: x[:, j * LANE + g * HD:j * LANE + (g + 1) * HD]
        ksel_ref[0, g] = jnp.concatenate([_rms(col(0), gs_ref[...]), onehot], axis=1).astype(BF16)
        kwin_ref[0, g] = _rms(col(2), gw_ref[...]).astype(BF16)
        vsel_ref[0, g] = jnp.concatenate([vs_t[g * HD:(g + 1) * HD], ones_row], axis=0).astype(BF16)
        vwin_ref[0, g] = jnp.concatenate([vw_t[g * HD:(g + 1) * HD], ones_row], axis=0).astype(BF16)


def _kv_prep(kvsw, ks_gain, kw_gain, b, s, ts=1024):
    ts = min(ts, s)
    g = NSA_GROUPS
    aug_w = max(LANE, HD + s // SEL_BLOCK)
    spec = lambda w: pl.BlockSpec((1, g, ts, w), lambda i, j: (i, 0, j, 0))
    spec_t = pl.BlockSpec((1, g, V_ROWS, ts), lambda i, j: (i, 0, 0, j))
    shape_t = jax.ShapeDtypeStruct((b, g, V_ROWS, s), BF16)
    return pl.pallas_call(
        _kvprep_kernel,
        grid=(b, s // ts),
        in_specs=[pl.BlockSpec((ts, KVSW_W), lambda i, j: (i * (s // ts) + j, 0)),
                  pl.BlockSpec((1, HD), lambda i, j: (0, 0)),
                  pl.BlockSpec((1, HD), lambda i, j: (0, 0))],
        out_specs=(spec(aug_w), spec_t, spec(HD), spec_t),
        out_shape=(jax.ShapeDtypeStruct((b, g, s, aug_w), BF16), shape_t,
                   jax.ShapeDtypeStruct((b, g, s, HD), BF16), shape_t),
        compiler_params=_cparams(("parallel", "parallel")),
        name="nsa_kv_prep",
    )(kvsw, ks_gain.reshape(1, HD), kw_gain.reshape(1, HD))


def _softmax_pv_step(s, v_t, m_ref, acc_ref):
    m_old = m_ref[...]
    m_new = jnp.maximum(m_old, _col_max(s))
    alpha = jnp.exp2(m_old - m_new)
    p = jnp.exp2((s - m_new).astype(BF16))
    acc_ref[...] = alpha * acc_ref[...] + _dot(v_t, p)
    m_ref[...] = m_new


def _col_max(s):
    rows, lanes = s.shape
    return jnp.max(jnp.max(s.reshape(4, rows // 4, lanes), axis=0), axis=0, keepdims=True)


def _nsa_kernel(q_ref, gl_ref, kc_ref, vct_ref, ksel_ref, vsel_ref, kwin_ref, vwin_ref, ovl_ref, qg_ref,
                o_ref, qt_ref, imp_ref, m_ref, acc_ref, oc_ref, s_ref, *, tq):
    i = pl.program_id(2)
    lanes = NSA_E * tq
    n_sel = ovl_ref.shape[0]
    scale = HD ** -0.5 * math.log2(math.e)

    q_t = q_ref[...].astype(F32).T
    for e in range(NSA_E):
        qe = q_t[e * HD:(e + 1) * HD, :]
        inv = lax.rsqrt(jnp.mean(qe * qe, axis=0, keepdims=True) + RMS_EPS)
        qt_ref[0:HD, e * tq:(e + 1) * tq] = (qe * inv * (qg_ref[...] * scale)).astype(BF16)
    q64 = qt_ref[0:HD, :]

    r_in = lax.broadcasted_iota(jnp.int32, (1, lanes), 1) & (tq - 1)
    t_lane = i * tq + r_in

    n_cmp = kc_ref.shape[3]
    s_c = _dot(kc_ref[0, 0, 0], q64)
    c_end = lax.broadcasted_iota(jnp.int32, (n_cmp, 1), 0) * CMP_STRIDE + (CMP_BLOCK - 1)
    s_c = jnp.where(c_end <= t_lane, s_c, NEG)
    e_c = jnp.exp2(s_c - jnp.max(s_c, axis=0, keepdims=True))
    inv_c = jnp.where(t_lane >= CMP_BLOCK - 1, 1.0 / jnp.sum(e_c, axis=0, keepdims=True), 0.0)
    p_c = e_c * inv_c
    oc_ref[...] = _dot(vct_ref[0, 0, 0], p_c.astype(BF16))

    p_sum = p_c[:, 0:tq]
    for e in range(1, NSA_E):
        p_sum = p_sum + p_c[:, e * tq:(e + 1) * tq]
    p_hi = p_sum.astype(BF16)
    p_lo = (p_sum - p_hi.astype(F32)).astype(BF16)
    imp = _dot(ovl_ref[...], p_hi) + _dot(ovl_ref[...], p_lo)
    j_idx = lax.broadcasted_iota(jnp.int32, (n_sel, tq), 0)
    cur = (i * tq + lax.broadcasted_iota(jnp.int32, (n_sel, tq), 1)) // SEL_BLOCK
    forced = (j_idx == 0) | (j_idx == cur) | (j_idx == cur - 1)
    valid = j_idx <= cur
    imp = jnp.where(valid, imp + jnp.where(forced, FORCE_BONUS, 0.0), -jnp.inf)
    imp_ref[...] = imp
    n_grp = n_sel // 8
    grp = [imp[8 * r:8 * r + 8] for r in range(n_grp)]
    j_in = lax.broadcasted_iota(jnp.int32, (8, tq), 0)
    rank = [jnp.zeros((8, tq), F32) for _ in range(n_grp)]
    for j2 in range(n_sel):
        other = imp_ref[j2:j2 + 1, :]
        for r in range(n_grp):
            if 8 * r > j2:
                before = other >= grp[r]
            elif 8 * r + 7 < j2:
                before = other > grp[r]
            else:
                before = (other > grp[r]) | ((other == grp[r]) & (j_in > j2 - 8 * r))
            rank[r] = rank[r] + jnp.where(before, 1.0, 0.0)
    sel = (jnp.concatenate(rank, axis=0) < min(TOP_N, n_sel)) & valid
    sel_bias = jnp.where(sel, 0.0, NEG).astype(BF16)
    for e in range(NSA_E):
        qt_ref[HD:HD + n_sel, e * tq:(e + 1) * tq] = sel_bias
    if HD + n_sel < qt_ref.shape[0]:
        qt_ref[HD + n_sel:, :] = jnp.zeros((qt_ref.shape[0] - HD - n_sel, lanes), BF16)

    key_in = lax.broadcasted_iota(jnp.int32, (tq, 1), 0)
    tile_rows = lambda t: pl.ds(pl.multiple_of(t * tq, tq), tq)

    big = 2 * tq
    s_a = jnp.where(key_in <= r_in, _dot(kwin_ref[0, 0, tile_rows(i), :], q64), NEG)
    t_b = jnp.maximum(i - 1, 0)
    s_b = jnp.where(key_in > jnp.where(i >= 1, -1, big), _dot(kwin_ref[0, 0, tile_rows(t_b), :], q64), NEG)
    t_c = jnp.maximum(i - 2, 0)
    s_c2 = jnp.where(key_in > r_in + jnp.where(i >= 2, 0, big), _dot(kwin_ref[0, 0, tile_rows(t_c), :], q64), NEG)
    m_w = jnp.maximum(jnp.maximum(_col_max(s_a), _col_max(s_b)), _col_max(s_c2))
    acc_w = (_dot(vwin_ref[0, 0, :, tile_rows(i)], jnp.exp2((s_a - m_w).astype(BF16)))
             + _dot(vwin_ref[0, 0, :, tile_rows(t_b)], jnp.exp2((s_b - m_w).astype(BF16)))
             + _dot(vwin_ref[0, 0, :, tile_rows(t_c)], jnp.exp2((s_c2 - m_w).astype(BF16))))
    o_w = acc_w[0:HD, :] * (1.0 / acc_w[HD:HD + 1, :])

    gate = jax.nn.sigmoid(gl_ref[...].T)
    for e in range(NSA_E):
        cs = slice(e * tq, (e + 1) * tq)
        oc_ref[:, cs] = gate[3 * e:3 * e + 1, :] * oc_ref[:, cs] + gate[3 * e + 2:3 * e + 3, :] * o_w[:, cs]

    qk_sel = lambda t: _dot(ksel_ref[0, 0, tile_rows(t), :], qt_ref[...])
    m_ref[...] = jnp.full(m_ref.shape, NEG, F32)
    acc_ref[...] = jnp.zeros(acc_ref.shape, F32)
    s_ref[0] = jnp.where(key_in <= r_in, qk_sel(i), NEG)

    def consume(slot, elem):
        v_tile = tile_rows(jnp.where(elem == 0, i, elem - 1))
        _softmax_pv_step(s_ref[slot], vsel_ref[0, 0, :, v_tile], m_ref, acc_ref)

    def sel_pair(j, carry):
        s_ref[1] = qk_sel(2 * j)
        consume(0, 2 * j)
        s_ref[0] = qk_sel(2 * j + 1)
        consume(1, 2 * j + 1)
        return carry

    lax.fori_loop(0, i // 2, sel_pair, 0)

    @pl.when(i % 2 == 1)
    def _():
        s_ref[1] = qk_sel(i - 1)
        consume(0, i - 1)
        consume(1, i)

    @pl.when(i % 2 == 0)
    def _():
        consume(0, i)

    o_s = acc_ref[0:HD, :] * (1.0 / acc_ref[HD:HD + 1, :])

    ys = []
    for e in range(NSA_E):
        cs = slice(e * tq, (e + 1) * tq)
        ys.append(oc_ref[:, cs] + gate[3 * e + 1:3 * e + 2, :] * o_s[:, cs])
    o_ref[...] = jnp.concatenate(ys, axis=0).T.astype(o_ref.dtype)


def _nsa_attention(q, gl, cmp_kv, cmp_t, ksel, vsel, kwin, vwin, q_gain, b, s, tq=256):
    assert WINDOW == 2 * tq and s % tq == 0
    nq = s // tq
    n_sel = s // SEL_BLOCK
    nrow = s // CMP_STRIDE
    g = NSA_GROUPS
    c0 = jnp.arange(nrow) * CMP_STRIDE
    j0 = jnp.arange(n_sel) * SEL_BLOCK
    ovl = ((c0[None, :] <= j0[:, None] + SEL_BLOCK - 1) & (c0[None, :] + CMP_BLOCK - 1 >= j0[:, None]))
    ovl = ovl.astype(BF16)
    aug_w = max(2 * HD, HD + n_sel)
    assert ksel.shape[-1] == aug_w
    lanes = NSA_E * tq
    kv_spec = lambda w: pl.BlockSpec((1, 1, s, w), lambda bi, gi, i: (bi, gi, 0, 0))
    vt_spec = pl.BlockSpec((1, 1, V_ROWS, s), lambda bi, gi, i: (bi, gi, 0, 0))
    return pl.pallas_call(
        functools.partial(_nsa_kernel, tq=tq),
        grid=(b, g, nq),
        in_specs=[pl.BlockSpec((tq, NSA_E * HD), lambda bi, gi, i: (bi * nq + i, gi)),
                  pl.BlockSpec((tq, LANE), lambda bi, gi, i: (bi * nq + i, gi)),
                  pl.BlockSpec((1, 1, 1, nrow, HD), lambda bi, gi, i: (bi, 0, gi, 0, 0)),
                  pl.BlockSpec((1, 1, 1, HD, nrow), lambda bi, gi, i: (bi, 1, gi, 0, 0)),
                  kv_spec(aug_w), vt_spec, kv_spec(HD), vt_spec,
                  pl.BlockSpec((n_sel, nrow), lambda bi, gi, i: (0, 0)),
                  pl.BlockSpec((HD, 1), lambda bi, gi, i: (0, 0))],
        out_specs=pl.BlockSpec((tq, NSA_E * HD), lambda bi, gi, i: (bi * nq + i, gi)),
        out_shape=jax.ShapeDtypeStruct((b * s, NSA_HEADS * HD), BF16),
        scratch_shapes=[pltpu.VMEM((aug_w, lanes), BF16),
                        pltpu.VMEM((n_sel, tq), F32),
                        pltpu.VMEM((1, lanes), F32),
                        pltpu.VMEM((V_ROWS, lanes), F32),
                        pltpu.VMEM((HD, lanes), F32),
                        pltpu.VMEM((2, tq, lanes), F32)],
        compiler_params=_cparams(("parallel", "parallel", "arbitrary")),
        name="nsa_attention",
    )(q, gl, cmp_kv, cmp_t, ksel, vsel, kwin, vwin, ovl, q_gain.reshape(HD, 1))


def _softplus(x):
    return jnp.maximum(x, 0.0) + jnp.log1p(jnp.exp(-jnp.abs(x)))


def _split_bf16(x, pieces):
    out = []
    for _ in range(pieces - 1):
        hi = x.astype(BF16)
        out.append(hi)
        x = x - hi.astype(F32)
    out.append(x.astype(BF16))
    return out


def _ssd_kernel(z_ref, xbc_ref, dt_ref, dtT_ref, cw_ref, cb_ref, dtb_ref, dtbT_ref, alog_ref, alogT_ref,
                dskip_ref, ng_ref, exp_ref, shift_ref, o_ref, xpad_ref, state_ref, y_ref):
    c = pl.program_id(1)
    L = xbc_ref.shape[0]
    gw = SSM_INNER // SSM_GROUPS
    e_per_g = SSM_HEADS // SSM_GROUPS

    @pl.when(c == 0)
    def _():
        xpad_ref[0:8, :] = jnp.zeros((8, SSM_CONV_DIM), F32)
        state_ref[...] = jnp.zeros(state_ref.shape, F32)

    xb = xbc_ref[...]
    xf = xb.astype(F32)
    shifted = _dot(shift_ref[...], xb)
    acc = cb_ref[...] + cw_ref[SSM_CONV - 1:SSM_CONV, :] * xf
    for k in range(SSM_CONV - 1):
        acc = acc + cw_ref[k:k + 1, :] * shifted[k * L:(k + 1) * L]
    xpad_ref[8:16, :] = xf[0:8]
    head = cb_ref[...] + cw_ref[0:1, :] * xpad_ref[5:13, :]
    for k in range(1, SSM_CONV):
        head = head + cw_ref[k:k + 1, :] * xpad_ref[5 + k:13 + k, :]
    xpad_ref[0:8, :] = xf[L - 8:L]
    xact = _silu(jnp.concatenate([head, acc[8:]], axis=0))
    xs = xact[:, :SSM_INNER]

    li = lax.broadcasted_iota(jnp.int32, (L, L), 0)
    si = lax.broadcasted_iota(jnp.int32, (L, L), 1)
    causal = li >= si
    log2e = math.log2(math.e)
    dt = _softplus(dt_ref[...] + dtb_ref[...])
    da = jnp.concatenate(_split_bf16(dt * (-log2e * jnp.exp(alog_ref[...])), 3), axis=1)
    r = _dot(jnp.where(causal, 1.0, 0.0).astype(BF16), da)
    a_cs = r[:, 0:LANE] + r[:, LANE:2 * LANE] + r[:, 2 * LANE:]
    dtT = _softplus(dtT_ref[0] + dtbT_ref[...])
    daT = jnp.concatenate(_split_bf16(dtT * (-log2e * jnp.exp(alogT_ref[...])), 3), axis=0)
    rT = _dot(daT, jnp.where(li <= si, 1.0, 0.0).astype(BF16))
    nh = SSM_HEADS
    a_csT = rT[0:nh] + rT[nh:2 * nh] + rT[2 * nh:]

    widen = lambda v: _dot(jnp.concatenate(_split_bf16(v, 2), axis=1), exp_ref[...])
    ea_x = widen(jnp.exp2(a_cs))
    din_x = widen(jnp.exp2(a_cs[L - 1:L, :] - a_cs))
    xd = xs * widen(dt)
    xdd = (xd * din_x).astype(BF16)
    xd = xd.astype(BF16)

    half = L // 2
    first_head = lax.broadcasted_iota(jnp.int32, (L, 2 * SSM_P), 1) < SSM_P
    for g in range(SSM_GROUPS):
        gs = slice(g * gw, (g + 1) * gw)
        bg = xact[:, SSM_INNER + g * SSM_N:SSM_INNER + (g + 1) * SSM_N].astype(BF16)
        cg = xact[:, SSM_INNER + (SSM_GROUPS + g) * SSM_N:SSM_INNER + (SSM_GROUPS + g + 1) * SSM_N].astype(BF16)
        cb = _dot_nt(cg, bg)
        st = state_ref[g]
        y_ref[:, gs] = _dot(cg, st.astype(BF16)) * ea_x[:, gs] + xs[:, gs] * dskip_ref[:, gs]
        for pair in range(e_per_g // 2):
            ps = slice(g * gw + pair * 2 * SSM_P, g * gw + (pair + 1) * 2 * SSM_P)
            y_top = jnp.zeros((half, 2 * SSM_P), F32)
            y_bot = jnp.zeros((half, 2 * SSM_P), F32)
            for sub in range(2):
                h = g * e_per_g + 2 * pair + sub
                xh = jnp.where(first_head if sub == 0 else ~first_head, xd[:, ps], jnp.zeros_like(xd[:, ps]))
                diff = a_cs[:, h:h + 1] - a_csT[h:h + 1, :]
                seg_t = jnp.exp2(jnp.where(causal[:half, :half], diff[:half, :half], NEG))
                seg_b = jnp.exp2(jnp.where(causal[half:, :], diff[half:, :], NEG))
                y_top = y_top + _dot((cb[:half, :half] * seg_t).astype(BF16), xh[:half])
                y_bot = y_bot + _dot((cb[half:, :] * seg_b).astype(BF16), xh)
            y_ref[:half, ps] += y_top
            y_ref[half:, ps] += y_bot
        upd = lax.dot_general(bg, xdd[:, gs], (((0,), (0,)), ((), ())), preferred_element_type=F32)
        state_ref[g] = st * ea_x[L - 1:L, gs] + upd

    y = y_ref[...] * _silu(z_ref[...].astype(F32))
    for g in range(SSM_GROUPS):
        gs = slice(g * gw, (g + 1) * gw)
        o_ref[:, gs] = _rms(y[:, gs], ng_ref[:, gs]).astype(o_ref.dtype)


def _ssd_mixer(z, xbc, dt, conv_w, conv_b, dt_bias, a_log, d_skip, norm_gain, b, s):
    L = math.gcd(SSM_CHUNK, s)
    nc = s // L
    h = SSM_HEADS
    dtT = dt[:, :h].reshape(b, s, h).transpose(0, 2, 1)
    pad_l = lambda v: jnp.zeros((1, LANE), F32).at[0, :h].set(v)
    expand = jnp.zeros((LANE, SSM_INNER), F32).at[:h].set(jnp.repeat(jnp.eye(h, dtype=F32), SSM_P, axis=1))
    expand = jnp.concatenate([expand, expand], axis=0).astype(BF16)
    t_out = jnp.arange(L)[:, None]
    t_in = jnp.arange(L)[None, :]
    shift = jnp.concatenate([(t_in == t_out - (SSM_CONV - 1 - k)) for k in range(SSM_CONV - 1)], axis=0).astype(BF16)
    row = lambda w: pl.BlockSpec((L, w), lambda bi, ci: (bi * nc + ci, 0))
    full = lambda shp: pl.BlockSpec(shp, lambda bi, ci: (0,) * len(shp))
    return pl.pallas_call(
        _ssd_kernel,
        grid=(b, nc),
        in_specs=[row(SSM_INNER), row(SSM_CONV_DIM), row(LANE),
                  pl.BlockSpec((1, h, L), lambda bi, ci: (bi, 0, ci)),
                  full((SSM_CONV, SSM_CONV_DIM)), full((1, SSM_CONV_DIM)),
                  full((1, LANE)), full((h, 1)), full((1, LANE)), full((h, 1)),
                  full((1, SSM_INNER)), full((1, SSM_INNER)), full((2 * LANE, SSM_INNER)),
                  full(((SSM_CONV - 1) * L, L))],
        out_specs=row(SSM_INNER),
        out_shape=jax.ShapeDtypeStruct((b * s, SSM_INNER), BF16),
        scratch_shapes=[pltpu.VMEM((16, SSM_CONV_DIM), F32),
                        pltpu.VMEM((SSM_GROUPS, SSM_N, SSM_INNER // SSM_GROUPS), F32),
                        pltpu.VMEM((L, SSM_INNER), F32)],
        compiler_params=_cparams(("parallel", "arbitrary")),
        name="ssd_mixer",
    )(z, xbc, dt, dtT, conv_w, conv_b.reshape(1, -1), pad_l(dt_bias), dt_bias.reshape(h, 1),
      pad_l(a_log), a_log.reshape(h, 1), jnp.repeat(d_skip, SSM_P).reshape(1, -1),
      norm_gain.reshape(1, -1), expand, shift)


ROUTE_ROWS = 8 + MOE_EXPERTS


def _merge_kernel(x_ref, ya_ref, yb_ref, brg_ref, wa_ref, wb_ref, wo_ref, mg_ref, wr_ref, br_ref,
                  x1_ref, h2_ref, eid_ref, ew_ref):
    tm = x_ref.shape[0]
    gate = jax.nn.sigmoid(brg_ref[...].astype(F32))
    merged = gate[:, :D_MODEL] * _dot(ya_ref[...], wa_ref[...]) + gate[:, D_MODEL:] * _dot(yb_ref[...], wb_ref[...])
    x1 = x_ref[...] + _dot(merged.astype(BF16), wo_ref[...])
    x1_ref[...] = x1
    h2 = _rms(x1, mg_ref[...])
    h2_ref[...] = h2

    lg = _dot_nt(wr_ref[...], h2.astype(BF16)) + br_ref[...]
    r = lax.broadcasted_iota(jnp.int32, (ROUTE_ROWS, tm), 0)
    big = ROUTE_ROWS
    is_g = r < MOE_GROUPS
    gmax = jnp.max(jnp.where(is_g, lg, -jnp.inf), axis=0, keepdims=True)
    gsum = jnp.sum(jnp.where(is_g, jnp.exp(lg - gmax), 0.0), axis=0, keepdims=True)
    grp_w = 1.0 / gsum
    grp_i = jnp.min(jnp.where(is_g & (lg == gmax), r, big), axis=0, keepdims=True)
    is_e = (r >= 8) & (((r - 8) // MOE_EPG) == grp_i)
    le = jnp.where(is_e, lg, -jnp.inf)
    emax = jnp.max(le, axis=0, keepdims=True)
    esum = jnp.sum(jnp.where(is_e, jnp.exp(lg - emax), 0.0), axis=0, keepdims=True)
    i1 = jnp.min(jnp.where(le == emax, r, big), axis=0, keepdims=True)
    le2 = jnp.where(r == i1, -jnp.inf, le)
    e2max = jnp.max(le2, axis=0, keepdims=True)
    i2 = jnp.min(jnp.where((le2 == e2max) & is_e & (r != i1), r, big), axis=0, keepdims=True)
    p1 = 1.0 / esum
    p2 = jnp.exp(e2max - emax) / esum
    w1 = grp_w * p1 / (p1 + p2)
    w2 = grp_w * p2 / (p1 + p2)
    r8 = lax.broadcasted_iota(jnp.int32, (8, tm), 0)
    eid_ref[...] = jnp.where(r8 == 0, i1 - 8, jnp.where(r8 == 1, i2 - 8, 0))
    ew_ref[...] = jnp.where(r8 == 0, w1, jnp.where(r8 == 1, w2, 0.0))


def _merge_route(x2, y_a, y_b, brg, w_a, w_b, w_o, moe_gain, w_group, b_group, w_expert, b_expert, tm=512):
    n, d = x2.shape
    tm = min(tm, n)
    wr = jnp.zeros((ROUTE_ROWS, d), F32).at[:MOE_GROUPS].set(w_group.T).at[8:].set(w_expert.T).astype(BF16)
    br = jnp.zeros((ROUTE_ROWS, 1), F32).at[:MOE_GROUPS, 0].set(b_group).at[8:, 0].set(b_expert)
    row = lambda w: pl.BlockSpec((tm, w), lambda i: (i, 0))
    colblk = pl.BlockSpec((8, tm), lambda i: (0, i))
    return pl.pallas_call(
        _merge_kernel,
        grid=(n // tm,),
        in_specs=[row(d), row(Q_W), row(SSM_INNER), row(BRG_W),
                  _const_spec((Q_W, d)), _const_spec((SSM_INNER, d)), _const_spec((d, d)),
                  _const_spec((1, d)), _const_spec((ROUTE_ROWS, d)), _const_spec((ROUTE_ROWS, 1))],
        out_specs=(row(d), row(d), colblk, colblk),
        out_shape=(jax.ShapeDtypeStruct((n, d), F32), jax.ShapeDtypeStruct((n, d), F32),
                   jax.ShapeDtypeStruct((8, n), jnp.int32), jax.ShapeDtypeStruct((8, n), F32)),
        compiler_params=_cparams(("parallel",)),
        name="merge_route",
    )(x2, y_a, y_b, brg, w_a.astype(BF16), w_b.astype(BF16), w_o.astype(BF16), moe_gain.reshape(1, d), wr, br)


def _rank_kernel(eid_ref, rank_ref, cnt_ref, run_ref, tri_ref):
    i = pl.program_id(0)
    tm = eid_ref.shape[1]

    @pl.when(i == 0)
    def _():
        run_ref[...] = jnp.zeros(run_ref.shape, F32)
        earlier = lax.broadcasted_iota(jnp.int32, (tm, tm), 0) < lax.broadcasted_iota(jnp.int32, (tm, tm), 1)
        tri_ref[...] = jnp.where(earlier, 1.0, 0.0).astype(BF16)

    r = lax.broadcasted_iota(jnp.int32, (MOE_EXPERTS, tm), 0)
    oh0 = r == eid_ref[0:1, :]
    oh1 = r == eid_ref[1:2, :]
    oh = jnp.where(oh0 | oh1, 1.0, 0.0)
    prefix = _dot(oh.astype(BF16), tri_ref[...]) + run_ref[:, 0:1]
    rank0 = jnp.sum(jnp.where(oh0, prefix, 0.0), axis=0, keepdims=True)
    rank1 = jnp.sum(jnp.where(oh1, prefix, 0.0), axis=0, keepdims=True)
    r8 = lax.broadcasted_iota(jnp.int32, (8, tm), 0)
    rank_ref[...] = jnp.where(r8 == 0, rank0, jnp.where(r8 == 1, rank1, 0.0)).astype(jnp.int32)
    run_ref[...] = run_ref[...] + jnp.sum(oh, axis=1, keepdims=True)
    cnt_ref[...] = run_ref[...]


def _moe_rank(eid, tm=512):
    n = eid.shape[1]
    tm = min(tm, n)
    return pl.pallas_call(
        _rank_kernel,
        grid=(n // tm,),
        in_specs=[pl.BlockSpec((8, tm), lambda i: (0, i))],
        out_specs=(pl.BlockSpec((8, tm), lambda i: (0, i)),
                   pl.BlockSpec((MOE_EXPERTS, LANE), lambda i: (0, 0))),
        out_shape=(jax.ShapeDtypeStruct((8, n), jnp.int32),
                   jax.ShapeDtypeStruct((MOE_EXPERTS, LANE), F32)),
        scratch_shapes=[pltpu.VMEM((MOE_EXPERTS, LANE), F32), pltpu.VMEM((tm, tm), BF16)],
        compiler_params=_cparams(("arbitrary",)),
        name="moe_rank",
    )(eid)


def _row_copy(src, dst, sem):
    return pltpu.make_async_copy(src, dst, sem)


ROW_UNROLL = 8


def _dispatch_kernel(slot_ref, h_ref, xs_in_ref, xs_ref, sem):
    del xs_in_ref
    tm = h_ref.shape[0]

    def issue(t, carry):
        base = t * ROW_UNROLL
        for u in range(ROW_UNROLL):
            for k in range(MOE_TOP_K):
                _row_copy(h_ref.at[pl.ds(base + u, 1), :], xs_ref.at[pl.ds(slot_ref[k, base + u], 1), :],
                          sem).start(priority=k)
        return carry

    lax.fori_loop(0, tm // ROW_UNROLL, issue, 0)

    def drain(t, carry):
        for _ in range(ROW_UNROLL * MOE_TOP_K):
            _row_copy(h_ref.at[pl.ds(0, 1), :], xs_ref.at[pl.ds(0, 1), :], sem).wait()
        return carry

    lax.fori_loop(0, tm // ROW_UNROLL, drain, 0)


def _moe_dispatch(slot, h2, n_slots, tm=256):
    n, d = h2.shape
    tm = min(tm, n)
    xs0 = jnp.zeros((n_slots, d), h2.dtype)
    return pl.pallas_call(
        _dispatch_kernel,
        grid=(n // tm,),
        in_specs=[pl.BlockSpec((MOE_TOP_K, tm), lambda i: (0, i), memory_space=pltpu.SMEM),
                  pl.BlockSpec((tm, d), lambda i: (i, 0)),
                  pl.BlockSpec(memory_space=pl.ANY)],
        out_specs=pl.BlockSpec(memory_space=pl.ANY),
        out_shape=jax.ShapeDtypeStruct((n_slots, d), h2.dtype),
        scratch_shapes=[pltpu.SemaphoreType.DMA],
        input_output_aliases={2: 0},
        compiler_params=_cparams(("arbitrary",)),
        name="moe_dispatch",
    )(slot, h2, xs0)


def _expert_kernel(be_ref, nu_ref, xs_ref, wg_ref, wu_ref, wd_ref, o_ref):
    j = pl.program_id(0)

    @pl.when(j < nu_ref[0])
    def _():
        x = xs_ref[...].astype(BF16)
        act = _silu(_dot(x, wg_ref[0])) * _dot(x, wu_ref[0])
        o_ref[...] = _dot(act.astype(BF16), wd_ref[0])

    @pl.when(j >= nu_ref[0])
    def _():
        o_ref[...] = jnp.zeros(o_ref.shape, o_ref.dtype)


def _moe_experts(block_e, n_used, xs, w_gate, w_up, w_down):
    n_slots, d = xs.shape
    rb = MOE_ROW_BLOCK
    wspec = lambda shp: pl.BlockSpec((1,) + shp, lambda j, be, nu: (be[j], 0, 0))
    return pl.pallas_call(
        _expert_kernel,
        grid_spec=pltpu.PrefetchScalarGridSpec(
            num_scalar_prefetch=2,
            grid=(n_slots // rb,),
            in_specs=[pl.BlockSpec((rb, d), lambda j, be, nu: (j, 0)),
                      wspec((d, MOE_HIDDEN)), wspec((d, MOE_HIDDEN)), wspec((MOE_HIDDEN, d))],
            out_specs=pl.BlockSpec((rb, d), lambda j, be, nu: (j, 0))),
        out_shape=jax.ShapeDtypeStruct((n_slots, d), F32),
        compiler_params=_cparams(("arbitrary",)),
        name="moe_experts",
    )(block_e, n_used, xs, w_gate.astype(BF16), w_up.astype(BF16), w_down.astype(BF16))


def _combine_kernel(slot_ref, slot_next_ref, ew_ref, x1_ref, p_ref, eo_ref, pg_ref, wg_ref, wp_ref, o_ref,
                    rows_ref, sems):
    i = pl.program_id(0)
    n_steps = pl.num_programs(0)
    tm = x1_ref.shape[0]

    def gather(idx_ref, buf):
        def issue(t, carry):
            base = t * ROW_UNROLL
            for u in range(ROW_UNROLL):
                for k in range(MOE_TOP_K):
                    _row_copy(eo_ref.at[pl.ds(idx_ref[k, base + u], 1), :],
                              rows_ref.at[buf, k, pl.ds(base + u, 1), :], sems.at[buf]).start(priority=k)
            return carry

        lax.fori_loop(0, tm // ROW_UNROLL, issue, 0)

    @pl.when(i == 0)
    def _():
        gather(slot_ref, 0)

    @pl.when(i + 1 < n_steps)
    def _():
        gather(slot_next_ref, (i + 1) % 2)

    buf = i % 2

    def drain(t, carry):
        for _ in range(ROW_UNROLL * MOE_TOP_K):
            _row_copy(eo_ref.at[pl.ds(0, 1), :], rows_ref.at[buf, 0, pl.ds(0, 1), :], sems.at[buf]).wait()
        return carry

    lax.fori_loop(0, tm // ROW_UNROLL, drain, 0)

    x2 = x1_ref[...] + ew_ref[:, 0:1] * rows_ref[buf, 0] + ew_ref[:, 1:2] * rows_ref[buf, 1]
    h3 = _rms(x2, pg_ref[...]).astype(BF16)
    o_ref[...] = x2 + jax.nn.sigmoid(_dot(h3, wg_ref[...])) * _dot(p_ref[...].astype(BF16), wp_ref[...])


def _moe_combine_ple(slot, ew_cols, x1, p2, expert_out, ple_gain, w_gate, w_proj, tm=256):
    n, d = x1.shape
    tm = min(tm, n)
    row = lambda w: pl.BlockSpec((tm, w), lambda i: (i, 0))
    last = n // tm - 1
    return pl.pallas_call(
        _combine_kernel,
        grid=(n // tm,),
        in_specs=[pl.BlockSpec((MOE_TOP_K, tm), lambda i: (0, i), memory_space=pltpu.SMEM),
                  pl.BlockSpec((MOE_TOP_K, tm), lambda i: (0, jnp.minimum(i + 1, last)), memory_space=pltpu.SMEM),
                  row(8), row(d), row(PLE_DIM),
                  pl.BlockSpec(memory_space=pl.ANY),
                  _const_spec((1, d)), _const_spec((d, d)), _const_spec((PLE_DIM, d))],
        out_specs=row(d),
        out_shape=jax.ShapeDtypeStruct((n, d), F32),
        scratch_shapes=[pltpu.VMEM((2, MOE_TOP_K, tm, d), F32), pltpu.SemaphoreType.DMA((2,))],
        compiler_params=_cparams(("arbitrary",)),
        name="moe_combine_ple",
    )(slot, slot, ew_cols, x1, p2, expert_out, ple_gain.reshape(1, d), w_gate.astype(BF16), w_proj.astype(BF16))


def _layer(x2, p2, b, s, mix_norm, w_in, nsa_q_gain, nsa_kc_gain, nsa_ks_gain, nsa_kw_gain,
           cmp_pos_k, cmp_w1_k, cmp_w2_k, cmp_pos_v, cmp_w1_v, cmp_w2_v,
           ssm_conv_w, ssm_conv_b, ssm_dt_bias, ssm_a_log, ssm_d, ssm_norm,
           w_branch_a, w_branch_b, w_out,
           moe_norm, moe_w_group, moe_b_group, moe_w_expert, moe_b_expert,
           moe_w_gate, moe_w_up, moe_w_down, ple_norm, ple_w_gate, ple_w_proj):
    n = b * s
    q, kvc, kvsw, gl, z, xbc, dt, brg = _in_proj(x2, mix_norm, _pack_w_in(w_in))

    cmp_kv, cmp_t = _compress(kvc, jnp.stack([cmp_pos_k, cmp_pos_v]), jnp.stack([cmp_w1_k, cmp_w1_v]),
                              jnp.stack([cmp_w2_k, cmp_w2_v]), nsa_kc_gain, b, s)
    ksel, vsel, kwin, vwin = _kv_prep(kvsw, nsa_ks_gain, nsa_kw_gain, b, s)
    y_a = _nsa_attention(q, gl, cmp_kv, cmp_t, ksel, vsel, kwin, vwin, nsa_q_gain, b, s)
    y_b = _ssd_mixer(z, xbc, dt, ssm_conv_w, ssm_conv_b, ssm_dt_bias, ssm_a_log, ssm_d, ssm_norm, b, s)

    x1, h2, eid, ew = _merge_route(x2, y_a, y_b, brg, w_branch_a, w_branch_b, w_out, moe_norm,
                                   moe_w_group, moe_b_group, moe_w_expert, moe_b_expert)

    rank, cnt = _moe_rank(eid)
    rb = MOE_ROW_BLOCK
    counts = cnt[:, 0].astype(jnp.int32)
    padded = (counts + rb - 1) // rb * rb
    pend = jnp.cumsum(padded)
    pstart = pend - padded
    e_ids = jnp.arange(MOE_EXPERTS, dtype=jnp.int32)
    is_e = eid[None, :MOE_TOP_K] == e_ids[:, None, None]
    slot = rank[:MOE_TOP_K] + jnp.sum(jnp.where(is_e, pstart[:, None, None], 0), axis=0)
    n_blocks = -(-(n * MOE_TOP_K) // rb) + MOE_EXPERTS
    first_row = jnp.arange(n_blocks, dtype=jnp.int32) * rb
    block_e = jnp.minimum(jnp.sum(pend[None, :] <= first_row[:, None], axis=1), MOE_EXPERTS - 1)
    n_used = (pend[-1:] // rb).astype(jnp.int32)

    xs = _moe_dispatch(slot, h2, n_blocks * rb)
    expert_out = _moe_experts(block_e.astype(jnp.int32), n_used, xs, moe_w_gate, moe_w_up, moe_w_down)
    ew_cols = ew.T
    return _moe_combine_ple(slot, ew_cols, x1, p2, expert_out, ple_norm, ple_w_gate, ple_w_proj)


def kernel(x, p, mix_norm, w_in, nsa_q_gain, nsa_kc_gain, nsa_ks_gain, nsa_kw_gain, cmp_pos_k, cmp_w1_k, cmp_w2_k, cmp_pos_v, cmp_w1_v, cmp_w2_v, ssm_conv_w, ssm_conv_b, ssm_dt_bias, ssm_a_log, ssm_d, ssm_norm, w_branch_a, w_branch_b, w_out, moe_norm, moe_w_group, moe_b_group, moe_w_expert, moe_b_expert, moe_w_gate, moe_w_up, moe_w_down, ple_norm, ple_w_gate, ple_w_proj):
    b, s, d = x.shape
    params = (mix_norm, w_in, nsa_q_gain, nsa_kc_gain, nsa_ks_gain, nsa_kw_gain,
              cmp_pos_k, cmp_w1_k, cmp_w2_k, cmp_pos_v, cmp_w1_v, cmp_w2_v,
              ssm_conv_w, ssm_conv_b, ssm_dt_bias, ssm_a_log, ssm_d, ssm_norm,
              w_branch_a, w_branch_b, w_out,
              moe_norm, moe_w_group, moe_b_group, moe_w_expert, moe_b_expert,
              moe_w_gate, moe_w_up, moe_w_down, ple_norm, ple_w_gate, ple_w_proj)
    x2 = x.reshape(b * s, d)
    for i in range(p.shape[0]):
        x2 = _layer(x2, p[i].reshape(b * s, -1), b, s, *(w[i] for w in params))
    return x2.reshape(b, s, d)
```

```python
import functools
import math

import jax
import jax.numpy as jnp
from jax import lax
from jax.experimental import pallas as pl
from jax.experimental.pallas import tpu as pltpu

F32 = jnp.float32
BF16 = jnp.bfloat16

D_MODEL = 1024
RMS_EPS = 1e-6
NEG = -1e30

NSA_HEADS = 8
NSA_GROUPS = 2
NSA_E = NSA_HEADS // NSA_GROUPS
HD = 64
CMP_BLOCK = 32
CMP_STRIDE = 16
CMP_HIDDEN = 128
SEL_BLOCK = 64
TOP_N = 16
WINDOW = 512
FORCE_BONUS = 1e3

SSM_HEADS = 16
SSM_P = 64
SSM_INNER = SSM_HEADS * SSM_P
SSM_GROUPS = 2
SSM_N = 128
SSM_CONV = 4
SSM_CHUNK = 256
SSM_CONV_DIM = SSM_INNER + 2 * SSM_GROUPS * SSM_N

MOE_GROUPS = 4
MOE_EPG = 8
MOE_EXPERTS = MOE_GROUPS * MOE_EPG
MOE_TOP_K = 2
MOE_HIDDEN = 256
MOE_ROW_BLOCK = 256
PLE_DIM = 256

LANE = 128
VMEM_LIMIT = 52 * 1024 * 1024

Q_W = NSA_HEADS * HD
KVC_W = 2 * NSA_GROUPS * HD
KVSW_W = 4 * NSA_GROUPS * HD
GL_W = NSA_GROUPS * LANE
DT_W = LANE
BRG_W = 2 * D_MODEL
SEG_WIDTHS = (Q_W, KVC_W, KVSW_W, GL_W, SSM_INNER, SSM_CONV_DIM, DT_W, BRG_W)
SEG_DTYPES = (BF16, BF16, BF16, F32, BF16, BF16, F32, BF16)
PACKED_W = sum(SEG_WIDTHS)


def _cparams(sem):
    return pltpu.CompilerParams(dimension_semantics=sem, vmem_limit_bytes=VMEM_LIMIT)


def _const_spec(shape):
    n = len(shape)
    return pl.BlockSpec(shape, lambda *_: (0,) * n, pipeline_mode=pl.Buffered(1))


def _rms(xf, gain):
    return xf * lax.rsqrt(jnp.mean(xf * xf, axis=-1, keepdims=True) + RMS_EPS) * gain


def _silu(x):
    return x * jax.nn.sigmoid(x)


def _dot(a, b):
    return jnp.dot(a, b, preferred_element_type=F32)


def _dot_nt(a, b):
    return lax.dot_general(a, b, (((1,), (1,)), ((), ())), preferred_element_type=F32)


KVC_SEG = 1
CHUNK_W = CMP_STRIDE * NSA_GROUPS * HD


def _inproj_kernel(x_ref, g_ref, w_ref, *refs):
    outs, h_scr = refs[:-1], refs[-1]
    tm = x_ref.shape[0]
    h_scr[...] = _rms(x_ref[...], g_ref[...]).astype(BF16)
    off = 0
    for seg, (o_ref, width) in enumerate(zip(outs, SEG_WIDTHS)):
        if seg == KVC_SEG:
            val = _dot(h_scr[...], w_ref[:, off:off + width])
            for j in range(2):
                part = val[:, j * LANE:(j + 1) * LANE].reshape(tm // CMP_STRIDE, CMP_STRIDE, LANE)
                o_ref[:, j * CHUNK_W:(j + 1) * CHUNK_W] = part.reshape(tm // CMP_STRIDE, CHUNK_W).astype(o_ref.dtype)
        else:
            for lo in range(0, width, 512):
                hi = min(lo + 512, width)
                o_ref[:, lo:hi] = _dot(h_scr[...], w_ref[:, off + lo:off + hi]).astype(o_ref.dtype)
        off += width


def _pack_w_in(w_in):
    sizes = (Q_W, 128, 128, 128, 128, 128, 128, NSA_HEADS * 3, SSM_INNER, SSM_CONV_DIM, SSM_HEADS, BRG_W)
    offs = [0]
    for s in sizes:
        offs.append(offs[-1] + s)
    seg = lambda i: w_in[:, offs[i]:offs[i + 1]]
    q, kc, vc, ks, vs, kw, vw, ng, z, xbc, dt, brg = (seg(i) for i in range(12))
    d = w_in.shape[0]
    pad = lambda a, w: jnp.concatenate([a, jnp.zeros((d, w - a.shape[1]), a.dtype)], axis=1)
    per_g = NSA_E * 3
    gl = jnp.concatenate([pad(ng[:, g * per_g:(g + 1) * per_g], LANE) for g in range(NSA_GROUPS)], axis=1)
    packed = jnp.concatenate([q, kc, vc, ks, vs, kw, vw, gl, z, xbc, pad(dt, DT_W), brg], axis=1)
    return packed.astype(BF16)


def _in_proj(x2, gain, w_packed, tm=512):
    n, d = x2.shape
    tm = min(tm, n)
    shapes = [(n, w) for w in SEG_WIDTHS]
    blocks = [(tm, w) for w in SEG_WIDTHS]
    shapes[KVC_SEG] = (n // CMP_STRIDE, 2 * CHUNK_W)
    blocks[KVC_SEG] = (tm // CMP_STRIDE, 2 * CHUNK_W)
    return pl.pallas_call(
        _inproj_kernel,
        grid=(n // tm,),
        in_specs=[pl.BlockSpec((tm, d), lambda i: (i, 0)),
                  _const_spec((1, d)),
                  _const_spec((d, PACKED_W))],
        out_specs=tuple(pl.BlockSpec(blk, lambda i: (i, 0)) for blk in blocks),
        out_shape=tuple(jax.ShapeDtypeStruct(shp, dt) for shp, dt in zip(shapes, SEG_DTYPES)),
        scratch_shapes=[pltpu.VMEM((tm, d), BF16)],
        compiler_params=_cparams(("parallel",)),
        name="in_proj",
    )(x2, gain.reshape(1, d), w_packed)


def _compress_kernel(x_ref, w1x_ref, w1_ref, w2_ref, w2t_ref, pos_ref, gain_ref, o_ref, ot_ref):
    kv = pl.program_id(1)
    x = x_ref[...]
    h1 = _dot(x, w1x_ref[0, 0, 0])
    h2 = _dot(x, w1x_ref[0, 0, 1])
    nrow = x.shape[0]
    h2 = pltpu.roll(h2, nrow - 1, 0)
    bias = _dot(pos_ref[0], w1_ref[0])[0:1]
    hid = _silu(h1 + h2 + bias)
    out = _dot(hid.astype(BF16), w2_ref[0])
    normed = _rms(out, gain_ref[...])
    o_ref[0, 0, 0] = jnp.where(kv == 0, normed, out).astype(o_ref.dtype)
    ot_ref[0, 0, 0] = _dot_nt(w2t_ref[0], hid.astype(BF16)).astype(ot_ref.dtype)


def _compress(kvc, pos, w1, w2, kc_gain, b, s):
    nrow = s // CMP_STRIDE
    cl = CMP_BLOCK * HD
    pos_flat = jnp.zeros((2, 8, cl), BF16).at[:, 0, :].set(pos.reshape(2, cl).astype(BF16))
    w1r = w1.reshape(2, 2, CMP_STRIDE, 1, HD, CMP_HIDDEN)
    own = jnp.eye(NSA_GROUPS, dtype=w1.dtype)
    w1x = w1r[:, None] * own[None, :, None, None, :, None, None]
    w1x = w1x.reshape(2, NSA_GROUPS, 2, CHUNK_W, CMP_HIDDEN).astype(BF16)
    return pl.pallas_call(
        _compress_kernel,
        grid=(b, 2, NSA_GROUPS),
        in_specs=[pl.BlockSpec((nrow, CHUNK_W), lambda i, k, g: (i, k)),
                  pl.BlockSpec((1, 1, 2, CHUNK_W, CMP_HIDDEN), lambda i, k, g: (k, g, 0, 0, 0)),
                  pl.BlockSpec((1, cl, CMP_HIDDEN), lambda i, k, g: (k, 0, 0)),
                  pl.BlockSpec((1, CMP_HIDDEN, HD), lambda i, k, g: (k, 0, 0)),
                  pl.BlockSpec((1, HD, CMP_HIDDEN), lambda i, k, g: (k, 0, 0)),
                  pl.BlockSpec((1, 8, cl), lambda i, k, g: (k, 0, 0)),
                  pl.BlockSpec((1, HD), lambda i, k, g: (0, 0))],
        out_specs=(pl.BlockSpec((1, 1, 1, nrow, HD), lambda i, k, g: (i, k, g, 0, 0)),
                   pl.BlockSpec((1, 1, 1, HD, nrow), lambda i, k, g: (i, k, g, 0, 0))),
        out_shape=(jax.ShapeDtypeStruct((b, 2, NSA_GROUPS, nrow, HD), BF16),
                   jax.ShapeDtypeStruct((b, 2, NSA_GROUPS, HD, nrow), BF16)),
        compiler_params=_cparams(("parallel", "parallel", "parallel")),
        name="nsa_compress",
    )(kvc, w1x, w1.astype(BF16), w2.astype(BF16), w2.transpose(0, 2, 1).astype(BF16), pos_flat,
      kc_gain.reshape(1, HD))


V_ROWS = HD + 16


def _kvprep_kernel(x_ref, gs_ref, gw_ref, ksel_ref, vsel_ref, kwin_ref, vwin_ref):
    ts = x_ref.shape[0]
    base = pl.program_id(1) * ts
    x = x_ref[...].astype(F32)
    n_hot = ksel_ref.shape[-1] - HD
    blk = (base + lax.broadcasted_iota(jnp.int32, (ts, n_hot), 0)) // SEL_BLOCK
    onehot = jnp.where(blk == lax.broadcasted_iota(jnp.int32, (ts, n_hot), 1), 1.0, 0.0)
    ones_row = jnp.where(lax.broadcasted_iota(jnp.int32, (V_ROWS - HD, ts), 0) == 0, 1.0, 0.0)
    vs_t = x[:, LANE:2 * LANE].T
    vw_t = x[:, 3 * LANE:4 * LANE].T
    for g in range(NSA_GROUPS):
        col = lambda j: x[:, j * LANE + g * HD:j * LANE + (g + 1) * HD]
        ksel_ref[0, g] = jnp.concatenate([_rms(col(0), gs_ref[...]), onehot], axis=1).astype(BF16)
        kwin_ref[0, g] = _rms(col(2), gw_ref[...]).astype(BF16)
        vsel_ref[0, g] = jnp.concatenate([vs_t[g * HD:(g + 1) * HD], ones_row], axis=0).astype(BF16)
        vwin_ref[0, g] = jnp.concatenate([vw_t[g * HD:(g + 1) * HD], ones_row], axis=0).astype(BF16)


def _kv_prep(kvsw, ks_gain, kw_gain, b, s, ts=1024):
    ts = min(ts, s)
    g = NSA_GROUPS
    aug_w = max(LANE, HD + s // SEL_BLOCK)
    spec = lambda w: pl.BlockSpec((1, g, ts, w), lambda i, j: (i, 0, j, 0))
    spec_t = pl.BlockSpec((1, g, V_ROWS, ts), lambda i, j: (i, 0, 0, j))
    shape_t = jax.ShapeDtypeStruct((b, g, V_ROWS, s), BF16)
    return pl.pallas_call(
        _kvprep_kernel,
        grid=(b, s // ts),
        in_specs=[pl.BlockSpec((ts, KVSW_W), lambda i, j: (i * (s // ts) + j, 0)),
                  pl.BlockSpec((1, HD), lambda i, j: (0, 0)),
                  pl.BlockSpec((1, HD), lambda i, j: (0, 0))],
        out_specs=(spec(aug_w), spec_t, spec(HD), spec_t),
        out_shape=(jax.ShapeDtypeStruct((b, g, s, aug_w), BF16), shape_t,
                   jax.ShapeDtypeStruct((b, g, s, HD), BF16), shape_t),
        compiler_params=_cparams(("parallel", "parallel")),
        name="nsa_kv_prep",
    )(kvsw, ks_gain.reshape(1, HD), kw_gain.reshape(1, HD))


def _softmax_pv_step(s, v_t, m_ref, acc_ref):
    m_old = m_ref[...]
    m_new = jnp.maximum(m_old, _col_max(s))
    alpha = jnp.exp2(m_old - m_new)
    p = jnp.exp2((s - m_new).astype(BF16))
    acc_ref[...] = alpha * acc_ref[...] + _dot(v_t, p)
    m_ref[...] = m_new


def _col_max(s):
    rows, lanes = s.shape
    return jnp.max(jnp.max(s.reshape(4, rows // 4, lanes), axis=0), axis=0, keepdims=True)


def _nsa_kernel(q_ref, gl_ref, kc_ref, vct_ref, ksel_ref, vsel_ref, kwin_ref, vwin_ref, ovl_ref, qg_ref,
                o_ref, qt_ref, imp_ref, rank_ref, m_ref, acc_ref, oc_ref, s_ref, *, tq):
    i = pl.program_id(2)
    lanes = NSA_E * tq
    n_sel = ovl_ref.shape[0]
    scale = HD ** -0.5 * math.log2(math.e)

    q_t = q_ref[...].astype(F32).T
    for e in range(NSA_E):
        qe = q_t[e * HD:(e + 1) * HD, :]
        inv = lax.rsqrt(jnp.mean(qe * qe, axis=0, keepdims=True) + RMS_EPS)
        qt_ref[0:HD, e * tq:(e + 1) * tq] = (qe * inv * (qg_ref[...] * scale)).astype(BF16)
    q64 = qt_ref[0:HD, :]

    r_in = lax.broadcasted_iota(jnp.int32, (1, lanes), 1) & (tq - 1)
    t_lane = i * tq + r_in

    n_cmp = kc_ref.shape[3]
    s_c = _dot(kc_ref[0, 0, 0], q64)
    c_end = lax.broadcasted_iota(jnp.int32, (n_cmp, 1), 0) * CMP_STRIDE + (CMP_BLOCK - 1)
    s_c = jnp.where(c_end <= t_lane, s_c, NEG)
    e_c = jnp.exp2(s_c - jnp.max(s_c, axis=0, keepdims=True))
    inv_c = jnp.where(t_lane >= CMP_BLOCK - 1, 1.0 / jnp.sum(e_c, axis=0, keepdims=True), 0.0)
    p_c = e_c * inv_c
    oc_ref[...] = _dot(vct_ref[0, 0, 0], p_c.astype(BF16))

    p_sum = p_c[:, 0:tq]
    for e in range(1, NSA_E):
        p_sum = p_sum + p_c[:, e * tq:(e + 1) * tq]
    p_hi = p_sum.astype(BF16)
    p_lo = (p_sum - p_hi.astype(F32)).astype(BF16)
    imp = _dot(ovl_ref[...], p_hi) + _dot(ovl_ref[...], p_lo)
    j_idx = lax.broadcasted_iota(jnp.int32, (n_sel, tq), 0)
    cur = (i * tq + lax.broadcasted_iota(jnp.int32, (n_sel, tq), 1)) // SEL_BLOCK
    forced = (j_idx == 0) | (j_idx == cur) | (j_idx == cur - 1)
    valid = j_idx <= cur
    imp = jnp.where(valid, imp + jnp.where(forced, FORCE_BONUS, 0.0), -jnp.inf)
    imp_ref[...] = imp
    n_grp = n_sel // 8
    grp = [imp[8 * r:8 * r + 8] for r in range(n_grp)]
    j_in = lax.broadcasted_iota(jnp.int32, (8, tq), 0)
    rank_ref[...] = jnp.zeros(rank_ref.shape, F32)
    last_valid = ((i + 1) * tq - 1) // SEL_BLOCK
    for g2 in range(n_grp):
        @pl.when(8 * g2 <= last_valid)
        def _():
            rank = [jnp.zeros((8, tq), F32) for _ in range(n_grp)]
            for j2 in range(8 * g2, 8 * g2 + 8):
                other = imp_ref[j2:j2 + 1, :]
                for r in range(n_grp):
                    if 8 * r > j2:
                        before = other >= grp[r]
                    elif 8 * r + 7 < j2:
                        before = other > grp[r]
                    else:
                        before = (other > grp[r]) | ((other == grp[r]) & (j_in > j2 - 8 * r))
                    rank[r] = rank[r] + jnp.where(before, 1.0, 0.0)
            rank_ref[...] += jnp.concatenate(rank, axis=0)
    sel = (rank_ref[...] < min(TOP_N, n_sel)) & valid
    sel_bias = jnp.where(sel, 0.0, NEG).astype(BF16)
    for e in range(NSA_E):
        qt_ref[HD:HD + n_sel, e * tq:(e + 1) * tq] = sel_bias
    if HD + n_sel < qt_ref.shape[0]:
        qt_ref[HD + n_sel:, :] = jnp.zeros((qt_ref.shape[0] - HD - n_sel, lanes), BF16)

    key_in = lax.broadcasted_iota(jnp.int32, (tq, 1), 0)
    tile_rows = lambda t: pl.ds(pl.multiple_of(t * tq, tq), tq)

    big = 2 * tq
    s_a = jnp.where(key_in <= r_in, _dot(kwin_ref[0, 0, tile_rows(i), :], q64), NEG)
    t_b = jnp.maximum(i - 1, 0)
    s_b = jnp.where(key_in > jnp.where(i >= 1, -1, big), _dot(kwin_ref[0, 0, tile_rows(t_b), :], q64), NEG)
    t_c = jnp.maximum(i - 2, 0)
    s_c2 = jnp.where(key_in > r_in + jnp.where(i >= 2, 0, big), _dot(kwin_ref[0, 0, tile_rows(t_c), :], q64), NEG)
    m_w = jnp.maximum(jnp.maximum(_col_max(s_a), _col_max(s_b)), _col_max(s_c2))
    acc_w = (_dot(vwin_ref[0, 0, :, tile_rows(i)], jnp.exp2((s_a - m_w).astype(BF16)))
             + _dot(vwin_ref[0, 0, :, tile_rows(t_b)], jnp.exp2((s_b - m_w).astype(BF16)))
             + _dot(vwin_ref[0, 0, :, tile_rows(t_c)], jnp.exp2((s_c2 - m_w).astype(BF16))))
    o_w = acc_w[0:HD, :] * (1.0 / acc_w[HD:HD + 1, :])

    gate = jax.nn.sigmoid(gl_ref[...].T)
    for e in range(NSA_E):
        cs = slice(e * tq, (e + 1) * tq)
        oc_ref[:, cs] = gate[3 * e:3 * e + 1, :] * oc_ref[:, cs] + gate[3 * e + 2:3 * e + 3, :] * o_w[:, cs]

    qk_sel = lambda t: _dot(ksel_ref[0, 0, tile_rows(t), :], qt_ref[...])
    m_ref[...] = jnp.full(m_ref.shape, NEG, F32)
    acc_ref[...] = jnp.zeros(acc_ref.shape, F32)
    s_ref[0] = jnp.where(key_in <= r_in, qk_sel(i), NEG)

    def consume(slot, elem):
        v_tile = tile_rows(jnp.where(elem == 0, i, elem - 1))
        _softmax_pv_step(s_ref[slot], vsel_ref[0, 0, :, v_tile], m_ref, acc_ref)

    def sel_pair(j, carry):
        s_ref[1] = qk_sel(2 * j)
        consume(0, 2 * j)
        s_ref[0] = qk_sel(2 * j + 1)
        consume(1, 2 * j + 1)
        return carry

    lax.fori_loop(0, i // 2, sel_pair, 0)

    @pl.when(i % 2 == 1)
    def _():
        s_ref[1] = qk_sel(i - 1)
        consume(0, i - 1)
        consume(1, i)

    @pl.when(i % 2 == 0)
    def _():
        consume(0, i)

    o_s = acc_ref[0:HD, :] * (1.0 / acc_ref[HD:HD + 1, :])

    ys = []
    for e in range(NSA_E):
        cs = slice(e * tq, (e + 1) * tq)
        ys.append(oc_ref[:, cs] + gate[3 * e + 1:3 * e + 2, :] * o_s[:, cs])
    o_ref[...] = jnp.concatenate(ys, axis=0).T.astype(o_ref.dtype)


def _nsa_attention(q, gl, cmp_kv, cmp_t, ksel, vsel, kwin, vwin, q_gain, b, s, tq=256):
    assert WINDOW == 2 * tq and s % tq == 0
    nq = s // tq
    n_sel = s // SEL_BLOCK
    nrow = s // CMP_STRIDE
    g = NSA_GROUPS
    c0 = jnp.arange(nrow) * CMP_STRIDE
    j0 = jnp.arange(n_sel) * SEL_BLOCK
    ovl = ((c0[None, :] <= j0[:, None] + SEL_BLOCK - 1) & (c0[None, :] + CMP_BLOCK - 1 >= j0[:, None]))
    ovl = ovl.astype(BF16)
    aug_w = max(2 * HD, HD + n_sel)
    assert ksel.shape[-1] == aug_w
    lanes = NSA_E * tq
    kv_spec = lambda w: pl.BlockSpec((1, 1, s, w), lambda bi, gi, i: (bi, gi, 0, 0))
    vt_spec = pl.BlockSpec((1, 1, V_ROWS, s), lambda bi, gi, i: (bi, gi, 0, 0))
    return pl.pallas_call(
        functools.partial(_nsa_kernel, tq=tq),
        grid=(b, g, nq),
        in_specs=[pl.BlockSpec((tq, NSA_E * HD), lambda bi, gi, i: (bi * nq + i, gi)),
                  pl.BlockSpec((tq, LANE), lambda bi, gi, i: (bi * nq + i, gi)),
                  pl.BlockSpec((1, 1, 1, nrow, HD), lambda bi, gi, i: (bi, 0, gi, 0, 0)),
                  pl.BlockSpec((1, 1, 1, HD, nrow), lambda bi, gi, i: (bi, 1, gi, 0, 0)),
                  kv_spec(aug_w), vt_spec, kv_spec(HD), vt_spec,
                  pl.BlockSpec((n_sel, nrow), lambda bi, gi, i: (0, 0)),
                  pl.BlockSpec((HD, 1), lambda bi, gi, i: (0, 0))],
        out_specs=pl.BlockSpec((tq, NSA_E * HD), lambda bi, gi, i: (bi * nq + i, gi)),
        out_shape=jax.ShapeDtypeStruct((b * s, NSA_HEADS * HD), BF16),
        scratch_shapes=[pltpu.VMEM((aug_w, lanes), BF16),
                        pltpu.VMEM((n_sel, tq), F32),
                        pltpu.VMEM((n_sel, tq), F32),
                        pltpu.VMEM((1, lanes), F32),
                        pltpu.VMEM((V_ROWS, lanes), F32),
                        pltpu.VMEM((HD, lanes), F32),
                        pltpu.VMEM((2, tq, lanes), F32)],
        compiler_params=_cparams(("parallel", "parallel", "arbitrary")),
        name="nsa_attention",
    )(q, gl, cmp_kv, cmp_t, ksel, vsel, kwin, vwin, ovl, q_gain.reshape(HD, 1))


def _softplus(x):
    return jnp.maximum(x, 0.0) + jnp.log1p(jnp.exp(-jnp.abs(x)))


def _split_bf16(x, pieces):
    out = []
    for _ in range(pieces - 1):
        hi = x.astype(BF16)
        out.append(hi)
        x = x - hi.astype(F32)
    out.append(x.astype(BF16))
    return out


def _ssd_kernel(z_ref, xbc_ref, dt_ref, dtT_ref, cw_ref, cb_ref, dtb_ref, dtbT_ref, alog_ref, alogT_ref,
                dskip_ref, ng_ref, exp_ref, shift_ref, o_ref, xpad_ref, state_ref, y_ref):
    c = pl.program_id(1)
    L = xbc_ref.shape[0]
    gw = SSM_INNER // SSM_GROUPS
    e_per_g = SSM_HEADS // SSM_GROUPS

    @pl.when(c == 0)
    def _():
        xpad_ref[0:8, :] = jnp.zeros((8, SSM_CONV_DIM), F32)
        state_ref[...] = jnp.zeros(state_ref.shape, F32)

    xb = xbc_ref[...]
    xf = xb.astype(F32)
    shifted = _dot(shift_ref[...], xb)
    acc = cb_ref[...] + cw_ref[SSM_CONV - 1:SSM_CONV, :] * xf
    for k in range(SSM_CONV - 1):
        acc = acc + cw_ref[k:k + 1, :] * shifted[k * L:(k + 1) * L]
    xpad_ref[8:16, :] = xf[0:8]
    head = cb_ref[...] + cw_ref[0:1, :] * xpad_ref[5:13, :]
    for k in range(1, SSM_CONV):
        head = head + cw_ref[k:k + 1, :] * xpad_ref[5 + k:13 + k, :]
    xpad_ref[0:8, :] = xf[L - 8:L]
    xact = _silu(jnp.concatenate([head, acc[8:]], axis=0))
    xs = xact[:, :SSM_INNER]

    li = lax.broadcasted_iota(jnp.int32, (L, L), 0)
    si = lax.broadcasted_iota(jnp.int32, (L, L), 1)
    causal = li >= si
    log2e = math.log2(math.e)
    dt = _softplus(dt_ref[...] + dtb_ref[...])
    da = jnp.concatenate(_split_bf16(dt * (-log2e * jnp.exp(alog_ref[...])), 3), axis=1)
    r = _dot(jnp.where(causal, 1.0, 0.0).astype(BF16), da)
    a_cs = r[:, 0:LANE] + r[:, LANE:2 * LANE] + r[:, 2 * LANE:]
    dtT = _softplus(dtT_ref[0] + dtbT_ref[...])
    daT = jnp.concatenate(_split_bf16(dtT * (-log2e * jnp.exp(alogT_ref[...])), 3), axis=0)
    rT = _dot(daT, jnp.where(li <= si, 1.0, 0.0).astype(BF16))
    nh = SSM_HEADS
    a_csT = rT[0:nh] + rT[nh:2 * nh] + rT[2 * nh:]

    widen = lambda v: _dot(jnp.concatenate(_split_bf16(v, 2), axis=1), exp_ref[...])
    ea_x = widen(jnp.exp2(a_cs))
    din_x = widen(jnp.exp2(a_cs[L - 1:L, :] - a_cs))
    xd = xs * widen(dt)
    xdd = (xd * din_x).astype(BF16)
    xd = xd.astype(BF16)

    half = L // 2
    first_head = lax.broadcasted_iota(jnp.int32, (L, 2 * SSM_P), 1) < SSM_P
    for g in range(SSM_GROUPS):
        gs = slice(g * gw, (g + 1) * gw)
        bg = xact[:, SSM_INNER + g * SSM_N:SSM_INNER + (g + 1) * SSM_N].astype(BF16)
        cg = xact[:, SSM_INNER + (SSM_GROUPS + g) * SSM_N:SSM_INNER + (SSM_GROUPS + g + 1) * SSM_N].astype(BF16)
        cb = _dot_nt(cg, bg)
        st = state_ref[g]
        y_ref[:, gs] = _dot(cg, st.astype(BF16)) * ea_x[:, gs] + xs[:, gs] * dskip_ref[:, gs]
        for pair in range(e_per_g // 2):
            ps = slice(g * gw + pair * 2 * SSM_P, g * gw + (pair + 1) * 2 * SSM_P)
            y_top = jnp.zeros((half, 2 * SSM_P), F32)
            y_bot = jnp.zeros((half, 2 * SSM_P), F32)
            for sub in range(2):
                h = g * e_per_g + 2 * pair + sub
                xh = jnp.where(first_head if sub == 0 else ~first_head, xd[:, ps], jnp.zeros_like(xd[:, ps]))
                diff = a_cs[:, h:h + 1] - a_csT[h:h + 1, :]
                seg_t = jnp.exp2(jnp.where(causal[:half, :half], diff[:half, :half], NEG))
                seg_b = jnp.exp2(jnp.where(causal[half:, :], diff[half:, :], NEG))
                y_top = y_top + _dot((cb[:half, :half] * seg_t).astype(BF16), xh[:half])
                y_bot = y_bot + _dot((cb[half:, :] * seg_b).astype(BF16), xh)
            y_ref[:half, ps] += y_top
            y_ref[half:, ps] += y_bot
        upd = lax.dot_general(bg, xdd[:, gs], (((0,), (0,)), ((), ())), preferred_element_type=F32)
        state_ref[g] = st * ea_x[L - 1:L, gs] + upd

    y = y_ref[...] * _silu(z_ref[...].astype(F32))
    for g in range(SSM_GROUPS):
        gs = slice(g * gw, (g + 1) * gw)
        o_ref[:, gs] = _rms(y[:, gs], ng_ref[:, gs]).astype(o_ref.dtype)


def _ssd_mixer(z, xbc, dt, conv_w, conv_b, dt_bias, a_log, d_skip, norm_gain, b, s):
    L = math.gcd(SSM_CHUNK, s)
    nc = s // L
    h = SSM_HEADS
    dtT = dt[:, :h].reshape(b, s, h).transpose(0, 2, 1)
    pad_l = lambda v: jnp.zeros((1, LANE), F32).at[0, :h].set(v)
    expand = jnp.zeros((LANE, SSM_INNER), F32).at[:h].set(jnp.repeat(jnp.eye(h, dtype=F32), SSM_P, axis=1))
    expand = jnp.concatenate([expand, expand], axis=0).astype(BF16)
    t_out = jnp.arange(L)[:, None]
    t_in = jnp.arange(L)[None, :]
    shift = jnp.concatenate([(t_in == t_out - (SSM_CONV - 1 - k)) for k in range(SSM_CONV - 1)], axis=0).astype(BF16)
    row = lambda w: pl.BlockSpec((L, w), lambda bi, ci: (bi * nc + ci, 0))
    full = lambda shp: pl.BlockSpec(shp, lambda bi, ci: (0,) * len(shp))
    return pl.pallas_call(
        _ssd_kernel,
        grid=(b, nc),
        in_specs=[row(SSM_INNER), row(SSM_CONV_DIM), row(LANE),
                  pl.BlockSpec((1, h, L), lambda bi, ci: (bi, 0, ci)),
                  full((SSM_CONV, SSM_CONV_DIM)), full((1, SSM_CONV_DIM)),
                  full((1, LANE)), full((h, 1)), full((1, LANE)), full((h, 1)),
                  full((1, SSM_INNER)), full((1, SSM_INNER)), full((2 * LANE, SSM_INNER)),
                  full(((SSM_CONV - 1) * L, L))],
        out_specs=row(SSM_INNER),
        out_shape=jax.ShapeDtypeStruct((b * s, SSM_INNER), BF16),
        scratch_shapes=[pltpu.VMEM((16, SSM_CONV_DIM), F32),
                        pltpu.VMEM((SSM_GROUPS, SSM_N, SSM_INNER // SSM_GROUPS), F32),
                        pltpu.VMEM((L, SSM_INNER), F32)],
        compiler_params=_cparams(("parallel", "arbitrary")),
        name="ssd_mixer",
    )(z, xbc, dt, dtT, conv_w, conv_b.reshape(1, -1), pad_l(dt_bias), dt_bias.reshape(h, 1),
      pad_l(a_log), a_log.reshape(h, 1), jnp.repeat(d_skip, SSM_P).reshape(1, -1),
      norm_gain.reshape(1, -1), expand, shift)


ROUTE_ROWS = 8 + MOE_EXPERTS


def _merge_kernel(x_ref, ya_ref, yb_ref, brg_ref, wa_ref, wb_ref, wo_ref, mg_ref, wr_ref, br_ref,
                  x1_ref, h2_ref, eid_ref, ew_ref):
    tm = x_ref.shape[0]
    gate = jax.nn.sigmoid(brg_ref[...].astype(F32))
    merged = gate[:, :D_MODEL] * _dot(ya_ref[...], wa_ref[...]) + gate[:, D_MODEL:] * _dot(yb_ref[...], wb_ref[...])
    x1 = x_ref[...] + _dot(merged.astype(BF16), wo_ref[...])
    x1_ref[...] = x1
    h2 = _rms(x1, mg_ref[...])
    _rows_to_tiles(h2, h2_ref)

    lg = _dot_nt(wr_ref[...], h2.astype(BF16)) + br_ref[...]
    r = lax.broadcasted_iota(jnp.int32, (ROUTE_ROWS, tm), 0)
    big = ROUTE_ROWS
    is_g = r < MOE_GROUPS
    gmax = jnp.max(jnp.where(is_g, lg, -jnp.inf), axis=0, keepdims=True)
    gsum = jnp.sum(jnp.where(is_g, jnp.exp(lg - gmax), 0.0), axis=0, keepdims=True)
    grp_w = 1.0 / gsum
    grp_i = jnp.min(jnp.where(is_g & (lg == gmax), r, big), axis=0, keepdims=True)
    is_e = (r >= 8) & (((r - 8) // MOE_EPG) == grp_i)
    le = jnp.where(is_e, lg, -jnp.inf)
    emax = jnp.max(le, axis=0, keepdims=True)
    esum = jnp.sum(jnp.where(is_e, jnp.exp(lg - emax), 0.0), axis=0, keepdims=True)
    i1 = jnp.min(jnp.where(le == emax, r, big), axis=0, keepdims=True)
    le2 = jnp.where(r == i1, -jnp.inf, le)
    e2max = jnp.max(le2, axis=0, keepdims=True)
    i2 = jnp.min(jnp.where((le2 == e2max) & is_e & (r != i1), r, big), axis=0, keepdims=True)
    p1 = 1.0 / esum
    p2 = jnp.exp(e2max - emax) / esum
    w1 = grp_w * p1 / (p1 + p2)
    w2 = grp_w * p2 / (p1 + p2)
    r8 = lax.broadcasted_iota(jnp.int32, (8, tm), 0)
    eid_ref[...] = jnp.where(r8 == 0, i1 - 8, jnp.where(r8 == 1, i2 - 8, 0))
    ew_ref[...] = jnp.where(r8 == 0, w1, jnp.where(r8 == 1, w2, 0.0))


def _merge_route(x2, y_a, y_b, brg, w_a, w_b, w_o, moe_gain, w_group, b_group, w_expert, b_expert, tm=512):
    n, d = x2.shape
    tm = min(tm, n)
    wr = jnp.zeros((ROUTE_ROWS, d), F32).at[:MOE_GROUPS].set(w_group.T).at[8:].set(w_expert.T).astype(BF16)
    br = jnp.zeros((ROUTE_ROWS, 1), F32).at[:MOE_GROUPS, 0].set(b_group).at[8:, 0].set(b_expert)
    row = lambda w: pl.BlockSpec((tm, w), lambda i: (i, 0))
    colblk = pl.BlockSpec((8, tm), lambda i: (0, i))
    return pl.pallas_call(
        _merge_kernel,
        grid=(n // tm,),
        in_specs=[row(d), row(Q_W), row(SSM_INNER), row(BRG_W),
                  _const_spec((Q_W, d)), _const_spec((SSM_INNER, d)), _const_spec((d, d)),
                  _const_spec((1, d)), _const_spec((ROUTE_ROWS, d)), _const_spec((ROUTE_ROWS, 1))],
        out_specs=(row(d), pl.BlockSpec((tm, ROW_SUB, LANE), lambda i: (i, 0, 0)), colblk, colblk),
        out_shape=(jax.ShapeDtypeStruct((n, d), F32), jax.ShapeDtypeStruct((n, ROW_SUB, LANE), F32),
                   jax.ShapeDtypeStruct((8, n), jnp.int32), jax.ShapeDtypeStruct((8, n), F32)),
        compiler_params=_cparams(("parallel",)),
        name="merge_route",
    )(x2, y_a, y_b, brg, w_a.astype(BF16), w_b.astype(BF16), w_o.astype(BF16), moe_gain.reshape(1, d), wr, br)


def _rank_kernel(eid_ref, rank_ref, cnt_ref, run_ref, tri_ref):
    i = pl.program_id(0)
    tm = eid_ref.shape[1]

    @pl.when(i == 0)
    def _():
        run_ref[...] = jnp.zeros(run_ref.shape, F32)
        earlier = lax.broadcasted_iota(jnp.int32, (tm, tm), 0) < lax.broadcasted_iota(jnp.int32, (tm, tm), 1)
        tri_ref[...] = jnp.where(earlier, 1.0, 0.0).astype(BF16)

    r = lax.broadcasted_iota(jnp.int32, (MOE_EXPERTS, tm), 0)
    oh0 = r == eid_ref[0:1, :]
    oh1 = r == eid_ref[1:2, :]
    oh = jnp.where(oh0 | oh1, 1.0, 0.0)
    prefix = _dot(oh.astype(BF16), tri_ref[...]) + run_ref[:, 0:1]
    rank0 = jnp.sum(jnp.where(oh0, prefix, 0.0), axis=0, keepdims=True)
    rank1 = jnp.sum(jnp.where(oh1, prefix, 0.0), axis=0, keepdims=True)
    r8 = lax.broadcasted_iota(jnp.int32, (8, tm), 0)
    rank_ref[...] = jnp.where(r8 == 0, rank0, jnp.where(r8 == 1, rank1, 0.0)).astype(jnp.int32)
    run_ref[...] = run_ref[...] + jnp.sum(oh, axis=1, keepdims=True)
    cnt_ref[...] = run_ref[...]


def _moe_rank(eid, tm=512):
    n = eid.shape[1]
    tm = min(tm, n)
    return pl.pallas_call(
        _rank_kernel,
        grid=(n // tm,),
        in_specs=[pl.BlockSpec((8, tm), lambda i: (0, i))],
        out_specs=(pl.BlockSpec((8, tm), lambda i: (0, i)),
                   pl.BlockSpec((MOE_EXPERTS, LANE), lambda i: (0, 0))),
        out_shape=(jax.ShapeDtypeStruct((8, n), jnp.int32),
                   jax.ShapeDtypeStruct((MOE_EXPERTS, LANE), F32)),
        scratch_shapes=[pltpu.VMEM((MOE_EXPERTS, LANE), F32), pltpu.VMEM((tm, tm), BF16)],
        compiler_params=_cparams(("arbitrary",)),
        name="moe_rank",
    )(eid)


def _row_copy(src, dst, sem):
    return pltpu.make_async_copy(src, dst, sem)


ROW_SUB = D_MODEL // LANE
ROW_UNROLL = 8


def _rows_to_tiles(x, ref):
    ref[...] = x.reshape(ref.shape)


def _tiles_to_rows(ref):
    x = ref[...]
    return x.reshape(x.shape[0], ROW_SUB * LANE)


def _dispatch_kernel(pad_lo_ref, pad_hi_ref, slot_ref, h_ref, xs_ref, zero_ref, sem, pad_sem):
    tm = h_ref.shape[0]

    @pl.when(pl.program_id(0) == 0)
    def _():
        zero_ref[...] = jnp.zeros(zero_ref.shape, zero_ref.dtype)

        def per_expert(e, carry):
            def start(r, c):
                _row_copy(zero_ref.at[0], xs_ref.at[r], pad_sem).start()
                return c

            def wait(r, c):
                _row_copy(zero_ref.at[0], xs_ref.at[0], pad_sem).wait()
                return c

            lax.fori_loop(pad_lo_ref[e], pad_hi_ref[e], start, 0)
            lax.fori_loop(pad_lo_ref[e], pad_hi_ref[e], wait, 0)
            return carry

        lax.fori_loop(0, pad_lo_ref.shape[0], per_expert, 0)

    def issue(t, carry):
        for u in range(ROW_UNROLL):
            n = t * ROW_UNROLL + u
            for k in range(MOE_TOP_K):
                _row_copy(h_ref.at[n], xs_ref.at[slot_ref[k, n]], sem).start(priority=k)
        return carry

    lax.fori_loop(0, tm // ROW_UNROLL, issue, 0)

    def drain(t, carry):
        for _ in range(ROW_UNROLL * MOE_TOP_K):
            _row_copy(h_ref.at[0], xs_ref.at[0], sem).wait()
        return carry

    lax.fori_loop(0, tm // ROW_UNROLL, drain, 0)


def _moe_dispatch(pad_lo, pad_hi, slot, h2, n_slots, tm=256):
    n = h2.shape[0]
    tm = min(tm, n)
    return pl.pallas_call(
        _dispatch_kernel,
        grid_spec=pltpu.PrefetchScalarGridSpec(
            num_scalar_prefetch=2,
            grid=(n // tm,),
            in_specs=[pl.BlockSpec((MOE_TOP_K, tm), lambda i, lo, hi: (0, i), memory_space=pltpu.SMEM),
                      pl.BlockSpec((tm, ROW_SUB, LANE), lambda i, lo, hi: (i, 0, 0))],
            out_specs=pl.BlockSpec(memory_space=pl.ANY),
            scratch_shapes=[pltpu.VMEM((1, ROW_SUB, LANE), h2.dtype),
                            pltpu.SemaphoreType.DMA, pltpu.SemaphoreType.DMA]),
        out_shape=jax.ShapeDtypeStruct((n_slots, ROW_SUB, LANE), h2.dtype),
        compiler_params=_cparams(("arbitrary",)),
        name="moe_dispatch",
    )(pad_lo, pad_hi, slot, h2)


def _expert_kernel(be_ref, nu_ref, xs_ref, wg_ref, wu_ref, wd_ref, o_ref):
    j = pl.program_id(0)

    @pl.when(j < nu_ref[0])
    def _():
        x = _tiles_to_rows(xs_ref).astype(BF16)
        act = _silu(_dot(x, wg_ref[0])) * _dot(x, wu_ref[0])
        _rows_to_tiles(_dot(act.astype(BF16), wd_ref[0]), o_ref)

    @pl.when(j >= nu_ref[0])
    def _():
        o_ref[...] = jnp.zeros(o_ref.shape, o_ref.dtype)


def _moe_experts(block_e, n_used, xs, w_gate, w_up, w_down):
    n_slots = xs.shape[0]
    d = D_MODEL
    rb = MOE_ROW_BLOCK
    wspec = lambda shp: pl.BlockSpec((1,) + shp, lambda j, be, nu: (be[j], 0, 0))
    rows = pl.BlockSpec((rb, ROW_SUB, LANE), lambda j, be, nu: (j, 0, 0))
    return pl.pallas_call(
        _expert_kernel,
        grid_spec=pltpu.PrefetchScalarGridSpec(
            num_scalar_prefetch=2,
            grid=(n_slots // rb,),
            in_specs=[rows, wspec((d, MOE_HIDDEN)), wspec((d, MOE_HIDDEN)), wspec((MOE_HIDDEN, d))],
            out_specs=rows),
        out_shape=jax.ShapeDtypeStruct(xs.shape, F32),
        compiler_params=_cparams(("arbitrary",)),
        name="moe_experts",
    )(block_e, n_used, xs, w_gate.astype(BF16), w_up.astype(BF16), w_down.astype(BF16))


def _combine_kernel(slot_ref, slot_next_ref, ew_ref, x1_ref, p_ref, eo_ref, pg_ref, wg_ref, wp_ref, o_ref,
                    rows_ref, sems):
    i = pl.program_id(0)
    n_steps = pl.num_programs(0)
    tm = x1_ref.shape[0]

    def gather(idx_ref, buf):
        def issue(t, carry):
            for u in range(ROW_UNROLL):
                n = t * ROW_UNROLL + u
                for k in range(MOE_TOP_K):
                    _row_copy(eo_ref.at[idx_ref[k, n]], rows_ref.at[buf, k, n], sems.at[buf]).start(priority=k)
            return carry

        lax.fori_loop(0, tm // ROW_UNROLL, issue, 0)

    @pl.when(i == 0)
    def _():
        gather(slot_ref, 0)

    @pl.when(i + 1 < n_steps)
    def _():
        gather(slot_next_ref, (i + 1) % 2)

    buf = i % 2

    def drain(t, carry):
        for _ in range(ROW_UNROLL * MOE_TOP_K):
            _row_copy(eo_ref.at[0], rows_ref.at[buf, 0, 0], sems.at[buf]).wait()
        return carry

    lax.fori_loop(0, tm // ROW_UNROLL, drain, 0)

    rows = lambda k: _tiles_to_rows(rows_ref.at[buf, k])
    x2 = x1_ref[...] + ew_ref[:, 0:1] * rows(0) + ew_ref[:, 1:2] * rows(1)
    h3 = _rms(x2, pg_ref[...]).astype(BF16)
    o_ref[...] = x2 + jax.nn.sigmoid(_dot(h3, wg_ref[...])) * _dot(p_ref[...].astype(BF16), wp_ref[...])


def _moe_combine_ple(slot, ew_cols, x1, p2, expert_out, ple_gain, w_gate, w_proj, tm=256):
    n, d = x1.shape
    tm = min(tm, n)
    row = lambda w: pl.BlockSpec((tm, w), lambda i: (i, 0))
    last = n // tm - 1
    idx = lambda fn: pl.BlockSpec((MOE_TOP_K, tm), fn, memory_space=pltpu.SMEM)
    return pl.pallas_call(
        _combine_kernel,
        grid=(n // tm,),
        in_specs=[idx(lambda i: (0, i)), idx(lambda i: (0, jnp.minimum(i + 1, last))),
                  row(8), row(d), row(PLE_DIM),
                  pl.BlockSpec(memory_space=pl.ANY),
                  _const_spec((1, d)), _const_spec((d, d)), _const_spec((PLE_DIM, d))],
        out_specs=row(d),
        out_shape=jax.ShapeDtypeStruct((n, d), F32),
        scratch_shapes=[pltpu.VMEM((2, MOE_TOP_K, tm, ROW_SUB, LANE), F32),
                        pltpu.SemaphoreType.DMA((2,))],
        compiler_params=_cparams(("arbitrary",)),
        name="moe_combine_ple",
    )(slot, slot, ew_cols, x1, p2, expert_out, ple_gain.reshape(1, d), w_gate.astype(BF16), w_proj.astype(BF16))


def _layer(x2, p2, b, s, mix_norm, w_in, nsa_q_gain, nsa_kc_gain, nsa_ks_gain, nsa_kw_gain,
           cmp_pos_k, cmp_w1_k, cmp_w2_k, cmp_pos_v, cmp_w1_v, cmp_w2_v,
           ssm_conv_w, ssm_conv_b, ssm_dt_bias, ssm_a_log, ssm_d, ssm_norm,
           w_branch_a, w_branch_b, w_out,
           moe_norm, moe_w_group, moe_b_group, moe_w_expert, moe_b_expert,
           moe_w_gate, moe_w_up, moe_w_down, ple_norm, ple_w_gate, ple_w_proj):
    n = b * s
    q, kvc, kvsw, gl, z, xbc, dt, brg = _in_proj(x2, mix_norm, _pack_w_in(w_in))

    cmp_kv, cmp_t = _compress(kvc, jnp.stack([cmp_pos_k, cmp_pos_v]), jnp.stack([cmp_w1_k, cmp_w1_v]),
                              jnp.stack([cmp_w2_k, cmp_w2_v]), nsa_kc_gain, b, s)
    ksel, vsel, kwin, vwin = _kv_prep(kvsw, nsa_ks_gain, nsa_kw_gain, b, s)
    y_a = _nsa_attention(q, gl, cmp_kv, cmp_t, ksel, vsel, kwin, vwin, nsa_q_gain, b, s)
    y_b = _ssd_mixer(z, xbc, dt, ssm_conv_w, ssm_conv_b, ssm_dt_bias, ssm_a_log, ssm_d, ssm_norm, b, s)

    x1, h2, eid, ew = _merge_route(x2, y_a, y_b, brg, w_branch_a, w_branch_b, w_out, moe_norm,
                                   moe_w_group, moe_b_group, moe_w_expert, moe_b_expert)

    rank, cnt = _moe_rank(eid)
    rb = MOE_ROW_BLOCK
    counts = cnt[:, 0].astype(jnp.int32)
    padded = (counts + rb - 1) // rb * rb
    pend = jnp.cumsum(padded)
    pstart = pend - padded
    e_ids = jnp.arange(MOE_EXPERTS, dtype=jnp.int32)
    is_e = eid[None, :MOE_TOP_K] == e_ids[:, None, None]
    slot = rank[:MOE_TOP_K] + jnp.sum(jnp.where(is_e, pstart[:, None, None], 0), axis=0)
    n_blocks = -(-(n * MOE_TOP_K) // rb) + MOE_EXPERTS
    first_row = jnp.arange(n_blocks, dtype=jnp.int32) * rb
    block_e = jnp.minimum(jnp.sum(pend[None, :] <= first_row[:, None], axis=1), MOE_EXPERTS - 1)
    n_used = (pend[-1:] // rb).astype(jnp.int32)

    n_slots = n_blocks * rb
    pad_lo = jnp.concatenate([pstart + counts, pend[-1:]])
    pad_hi = jnp.concatenate([pend, jnp.full((1,), n_slots, pend.dtype)])
    xs = _moe_dispatch(pad_lo, pad_hi, slot, h2, n_slots)
    expert_out = _moe_experts(block_e.astype(jnp.int32), n_used, xs, moe_w_gate, moe_w_up, moe_w_down)
    ew_cols = ew.T
    return _moe_combine_ple(slot, ew_cols, x1, p2, expert_out, ple_norm, ple_w_gate, ple_w_proj)


def kernel(x, p, mix_norm, w_in, nsa_q_gain, nsa_kc_gain, nsa_ks_gain, nsa_kw_gain, cmp_pos_k, cmp_w1_k, cmp_w2_k, cmp_pos_v, cmp_w1_v, cmp_w2_v, ssm_conv_w, ssm_conv_b, ssm_dt_bias, ssm_a_log, ssm_d, ssm_norm, w_branch_a, w_branch_b, w_out, moe_norm, moe_w_group, moe_b_group, moe_w_expert, moe_b_expert, moe_w_gate, moe_w_up, moe_w_down, ple_norm, ple_w_gate, ple_w_proj):
    b, s, d = x.shape
    params = (mix_norm, w_in, nsa_q_gain, nsa_kc_gain, nsa_ks_gain, nsa_kw_gain,
              cmp_pos_k, cmp_w1_k, cmp_w2_k, cmp_pos_v, cmp_w1_v, cmp_w2_v,
              ssm_conv_w, ssm_conv_b, ssm_dt_bias, ssm_a_log, ssm_d, ssm_norm,
              w_branch_a, w_branch_b, w_out,
              moe_norm, moe_w_group, moe_b_group, moe_w_expert, moe_b_expert,
              moe_w_gate, moe_w_up, moe_w_down, ple_norm, ple_w_gate, ple_w_proj)
    x2 = x.reshape(b * s, d)
    for i in range(p.shape[0]):
        x2 = _layer(x2, p[i].reshape(b * s, -1), b, s, *(w[i] for w in params))
    return x2.reshape(b, s, d)
```

```python
import functools
import math

import jax
import jax.numpy as jnp
from jax import lax
from jax.experimental import pallas as pl
from jax.experimental.pallas import tpu as pltpu

F32 = jnp.float32
BF16 = jnp.bfloat16

D_MODEL = 1024
RMS_EPS = 1e-6
NEG = -1e30

NSA_HEADS = 8
NSA_GROUPS = 2
NSA_E = NSA_HEADS // NSA_GROUPS
HD = 64
CMP_BLOCK = 32
CMP_STRIDE = 16
CMP_HIDDEN = 128
SEL_BLOCK = 64
TOP_N = 16
WINDOW = 512
FORCE_BONUS = 1e3

SSM_HEADS = 16
SSM_P = 64
SSM_INNER = SSM_HEADS * SSM_P
SSM_GROUPS = 2
SSM_N = 128
SSM_CONV = 4
SSM_CHUNK = 256
SSM_CONV_DIM = SSM_INNER + 2 * SSM_GROUPS * SSM_N

MOE_GROUPS = 4
MOE_EPG = 8
MOE_EXPERTS = MOE_GROUPS * MOE_EPG
MOE_TOP_K = 2
MOE_HIDDEN = 256
MOE_ROW_BLOCK = 256
PLE_DIM = 256

LANE = 128
VMEM_LIMIT = 52 * 1024 * 1024

Q_W = NSA_HEADS * HD
KVC_W = 2 * NSA_GROUPS * HD
KVSW_W = 4 * NSA_GROUPS * HD
GL_W = NSA_GROUPS * LANE
DT_W = LANE
BRG_W = 2 * D_MODEL
SEG_WIDTHS = (Q_W, KVC_W, KVSW_W, GL_W, SSM_INNER, SSM_CONV_DIM, DT_W, BRG_W)
SEG_DTYPES = (BF16, BF16, BF16, F32, BF16, BF16, F32, BF16)
PACKED_W = sum(SEG_WIDTHS)


def _cparams(sem):
    return pltpu.CompilerParams(dimension_semantics=sem, vmem_limit_bytes=VMEM_LIMIT)


def _const_spec(shape):
    n = len(shape)
    return pl.BlockSpec(shape, lambda *_: (0,) * n, pipeline_mode=pl.Buffered(1))


def _rms(xf, gain):
    return xf * lax.rsqrt(jnp.mean(xf * xf, axis=-1, keepdims=True) + RMS_EPS) * gain


def _silu(x):
    return x * jax.nn.sigmoid(x)


def _dot(a, b):
    return jnp.dot(a, b, preferred_element_type=F32)


def _dot_nt(a, b):
    return lax.dot_general(a, b, (((1,), (1,)), ((), ())), preferred_element_type=F32)


KVC_SEG = 1
CHUNK_W = CMP_STRIDE * NSA_GROUPS * HD


def _inproj_kernel(x_ref, g_ref, w_ref, *refs):
    outs, h_scr = refs[:-1], refs[-1]
    tm = x_ref.shape[0]
    h_scr[...] = _rms(x_ref[...], g_ref[...]).astype(BF16)
    off = 0
    for seg, (o_ref, width) in enumerate(zip(outs, SEG_WIDTHS)):
        if seg == KVC_SEG:
            val = _dot(h_scr[...], w_ref[:, off:off + width])
            for j in range(2):
                part = val[:, j * LANE:(j + 1) * LANE].reshape(tm // CMP_STRIDE, CMP_STRIDE, LANE)
                o_ref[:, j * CHUNK_W:(j + 1) * CHUNK_W] = part.reshape(tm // CMP_STRIDE, CHUNK_W).astype(o_ref.dtype)
        else:
            for lo in range(0, width, 512):
                hi = min(lo + 512, width)
                o_ref[:, lo:hi] = _dot(h_scr[...], w_ref[:, off + lo:off + hi]).astype(o_ref.dtype)
        off += width


def _pack_w_in(w_in):
    sizes = (Q_W, 128, 128, 128, 128, 128, 128, NSA_HEADS * 3, SSM_INNER, SSM_CONV_DIM, SSM_HEADS, BRG_W)
    offs = [0]
    for s in sizes:
        offs.append(offs[-1] + s)
    seg = lambda i: w_in[:, offs[i]:offs[i + 1]]
    q, kc, vc, ks, vs, kw, vw, ng, z, xbc, dt, brg = (seg(i) for i in range(12))
    d = w_in.shape[0]
    pad = lambda a, w: jnp.concatenate([a, jnp.zeros((d, w - a.shape[1]), a.dtype)], axis=1)
    per_g = NSA_E * 3
    gl = jnp.concatenate([pad(ng[:, g * per_g:(g + 1) * per_g], LANE) for g in range(NSA_GROUPS)], axis=1)
    packed = jnp.concatenate([q, kc, vc, ks, vs, kw, vw, gl, z, xbc, pad(dt, DT_W), brg], axis=1)
    return packed.astype(BF16)


def _in_proj(x2, gain, w_packed, tm=512):
    n, d = x2.shape
    tm = min(tm, n)
    shapes = [(n, w) for w in SEG_WIDTHS]
    blocks = [(tm, w) for w in SEG_WIDTHS]
    shapes[KVC_SEG] = (n // CMP_STRIDE, 2 * CHUNK_W)
    blocks[KVC_SEG] = (tm // CMP_STRIDE, 2 * CHUNK_W)
    return pl.pallas_call(
        _inproj_kernel,
        grid=(n // tm,),
        in_specs=[pl.BlockSpec((tm, d), lambda i: (i, 0)),
                  _const_spec((1, d)),
                  _const_spec((d, PACKED_W))],
        out_specs=tuple(pl.BlockSpec(blk, lambda i: (i, 0)) for blk in blocks),
        out_shape=tuple(jax.ShapeDtypeStruct(shp, dt) for shp, dt in zip(shapes, SEG_DTYPES)),
        scratch_shapes=[pltpu.VMEM((tm, d), BF16)],
        compiler_params=_cparams(("parallel",)),
        name="in_proj",
    )(x2, gain.reshape(1, d), w_packed)


def _compress_kernel(x_ref, w1x_ref, w1_ref, w2_ref, w2t_ref, pos_ref, gain_ref, o_ref, ot_ref):
    kv = pl.program_id(1)
    x = x_ref[...]
    h1 = _dot(x, w1x_ref[0, 0, 0])
    h2 = _dot(x, w1x_ref[0, 0, 1])
    nrow = x.shape[0]
    h2 = pltpu.roll(h2, nrow - 1, 0)
    bias = _dot(pos_ref[0], w1_ref[0])[0:1]
    hid = _silu(h1 + h2 + bias)
    out = _dot(hid.astype(BF16), w2_ref[0])
    normed = _rms(out, gain_ref[...])
    o_ref[0, 0, 0] = jnp.where(kv == 0, normed, out).astype(o_ref.dtype)
    ot_ref[0, 0, 0] = _dot_nt(w2t_ref[0], hid.astype(BF16)).astype(ot_ref.dtype)


def _compress(kvc, pos, w1, w2, kc_gain, b, s):
    nrow = s // CMP_STRIDE
    cl = CMP_BLOCK * HD
    pos_flat = jnp.zeros((2, 8, cl), BF16).at[:, 0, :].set(pos.reshape(2, cl).astype(BF16))
    w1r = w1.reshape(2, 2, CMP_STRIDE, 1, HD, CMP_HIDDEN)
    own = jnp.eye(NSA_GROUPS, dtype=w1.dtype)
    w1x = w1r[:, None] * own[None, :, None, None, :, None, None]
    w1x = w1x.reshape(2, NSA_GROUPS, 2, CHUNK_W, CMP_HIDDEN).astype(BF16)
    return pl.pallas_call(
        _compress_kernel,
        grid=(b, 2, NSA_GROUPS),
        in_specs=[pl.BlockSpec((nrow, CHUNK_W), lambda i, k, g: (i, k)),
                  pl.BlockSpec((1, 1, 2, CHUNK_W, CMP_HIDDEN), lambda i, k, g: (k, g, 0, 0, 0)),
                  pl.BlockSpec((1, cl, CMP_HIDDEN), lambda i, k, g: (k, 0, 0)),
                  pl.BlockSpec((1, CMP_HIDDEN, HD), lambda i, k, g: (k, 0, 0)),
                  pl.BlockSpec((1, HD, CMP_HIDDEN), lambda i, k, g: (k, 0, 0)),
                  pl.BlockSpec((1, 8, cl), lambda i, k, g: (k, 0, 0)),
                  pl.BlockSpec((1, HD), lambda i, k, g: (0, 0))],
        out_specs=(pl.BlockSpec((1, 1, 1, nrow, HD), lambda i, k, g: (i, k, g, 0, 0)),
                   pl.BlockSpec((1, 1, 1, HD, nrow), lambda i, k, g: (i, k, g, 0, 0))),
        out_shape=(jax.ShapeDtypeStruct((b, 2, NSA_GROUPS, nrow, HD), BF16),
                   jax.ShapeDtypeStruct((b, 2, NSA_GROUPS, HD, nrow), BF16)),
        compiler_params=_cparams(("parallel", "parallel", "parallel")),
        name="nsa_compress",
    )(kvc, w1x, w1.astype(BF16), w2.astype(BF16), w2.transpose(0, 2, 1).astype(BF16), pos_flat,
      kc_gain.reshape(1, HD))


V_ROWS = HD + 16


def _kvprep_kernel(x_ref, gs_ref, gw_ref, ksel_ref, vsel_ref, kwin_ref, vwin_ref):
    ts = x_ref.shape[0]
    base = pl.program_id(1) * ts
    x = x_ref[...].astype(F32)
    n_hot = ksel_ref.shape[-1] - HD
    blk = (base + lax.broadcasted_iota(jnp.int32, (ts, n_hot), 0)) // SEL_BLOCK
    onehot = jnp.where(blk == lax.broadcasted_iota(jnp.int32, (ts, n_hot), 1), 1.0, 0.0)
    ones_row = jnp.where(lax.broadcasted_iota(jnp.int32, (V_ROWS - HD, ts), 0) == 0, 1.0, 0.0)
    vs_t = x[:, LANE:2 * LANE].T
    vw_t = x[:, 3 * LANE:4 * LANE].T
    for g in range(NSA_GROUPS):
        col = lambda j: x[:, j * LANE + g * HD:j * LANE + (g + 1) * HD]
        ksel_ref[0, g] = jnp.concatenate([_rms(col(0), gs_ref[...]), onehot], axis=1).astype(BF16)
        kwin_ref[0, g] = _rms(col(2), gw_ref[...]).astype(BF16)
        vsel_ref[0, g] = jnp.concatenate([vs_t[g * HD:(g + 1) * HD], ones_row], axis=0).astype(BF16)
        vwin_ref[0, g] = jnp.concatenate([vw_t[g * HD:(g + 1) * HD], ones_row], axis=0).astype(BF16)


def _kv_prep(kvsw, ks_gain, kw_gain, b, s, ts=1024):
    ts = min(ts, s)
    g = NSA_GROUPS
    aug_w = max(LANE, HD + s // SEL_BLOCK)
    spec = lambda w: pl.BlockSpec((1, g, ts, w), lambda i, j: (i, 0, j, 0))
    spec_t = pl.BlockSpec((1, g, V_ROWS, ts), lambda i, j: (i, 0, 0, j))
    shape_t = jax.ShapeDtypeStruct((b, g, V_ROWS, s), BF16)
    return pl.pallas_call(
        _kvprep_kernel,
        grid=(b, s // ts),
        in_specs=[pl.BlockSpec((ts, KVSW_W), lambda i, j: (i * (s // ts) + j, 0)),
                  pl.BlockSpec((1, HD), lambda i, j: (0, 0)),
                  pl.BlockSpec((1, HD), lambda i, j: (0, 0))],
        out_specs=(spec(aug_w), spec_t, spec(HD), spec_t),
        out_shape=(jax.ShapeDtypeStruct((b, g, s, aug_w), BF16), shape_t,
                   jax.ShapeDtypeStruct((b, g, s, HD), BF16), shape_t),
        compiler_params=_cparams(("parallel", "parallel")),
        name="nsa_kv_prep",
    )(kvsw, ks_gain.reshape(1, HD), kw_gain.reshape(1, HD))


def _softmax_pv_step(s, v_t, m_ref, acc_ref):
    m_old = m_ref[...]
    m_new = jnp.maximum(m_old, _col_max(s))
    alpha = jnp.exp2(m_old - m_new)
    p = jnp.exp2((s - m_new).astype(BF16))
    acc_ref[...] = alpha * acc_ref[...] + _dot(v_t, p)
    m_ref[...] = m_new


def _col_max(s):
    rows, lanes = s.shape
    return jnp.max(jnp.max(s.reshape(4, rows // 4, lanes), axis=0), axis=0, keepdims=True)


def _nsa_kernel(q_ref, gl_ref, kc_ref, vct_ref, ksel_ref, vsel_ref, kwin_ref, vwin_ref, ovl_ref, qg_ref,
                o_ref, qt_ref, imp_ref, m_ref, acc_ref, oc_ref, s_ref, w_ref, *, tq):
    i = pl.program_id(2)
    lanes = NSA_E * tq
    n_sel = ovl_ref.shape[0]
    scale = HD ** -0.5 * math.log2(math.e)

    q_t = q_ref[...].astype(F32).T
    for e in range(NSA_E):
        qe = q_t[e * HD:(e + 1) * HD, :]
        inv = lax.rsqrt(jnp.mean(qe * qe, axis=0, keepdims=True) + RMS_EPS)
        qt_ref[0:HD, e * tq:(e + 1) * tq] = (qe * inv * (qg_ref[...] * scale)).astype(BF16)
    q64 = qt_ref[0:HD, :]

    r_in = lax.broadcasted_iota(jnp.int32, (1, lanes), 1) & (tq - 1)
    t_lane = i * tq + r_in
    key_in = lax.broadcasted_iota(jnp.int32, (tq, 1), 0)
    tile_rows = lambda t: pl.ds(pl.multiple_of(t * tq, tq), tq)

    big = 2 * tq
    win_tiles = (i, jnp.maximum(i - 1, 0), jnp.maximum(i - 2, 0))
    win_masks = (lambda: key_in <= r_in,
                 lambda: key_in > jnp.where(i >= 1, -1, big),
                 lambda: key_in > r_in + jnp.where(i >= 2, 0, big))

    def window_scores(w):
        w_ref[w] = jnp.where(win_masks[w](), _dot(kwin_ref[0, 0, tile_rows(win_tiles[w]), :], q64), NEG)

    n_cmp = kc_ref.shape[3]
    s_c = _dot(kc_ref[0, 0, 0], q64)
    c_end = lax.broadcasted_iota(jnp.int32, (n_cmp, 1), 0) * CMP_STRIDE + (CMP_BLOCK - 1)
    s_c = jnp.where(c_end <= t_lane, s_c, NEG)
    e_c = jnp.exp2(s_c - jnp.max(s_c, axis=0, keepdims=True))
    inv_c = jnp.where(t_lane >= CMP_BLOCK - 1, 1.0 / jnp.sum(e_c, axis=0, keepdims=True), 0.0)
    p_c = e_c * inv_c
    oc_ref[...] = _dot(vct_ref[0, 0, 0], p_c.astype(BF16))

    p_sum = p_c[:, 0:tq]
    for e in range(1, NSA_E):
        p_sum = p_sum + p_c[:, e * tq:(e + 1) * tq]
    p_hi = p_sum.astype(BF16)
    p_lo = (p_sum - p_hi.astype(F32)).astype(BF16)
    imp = _dot(ovl_ref[...], p_hi) + _dot(ovl_ref[...], p_lo)
    j_idx = lax.broadcasted_iota(jnp.int32, (n_sel, tq), 0)
    cur = (i * tq + lax.broadcasted_iota(jnp.int32, (n_sel, tq), 1)) // SEL_BLOCK
    forced = (j_idx == 0) | (j_idx == cur) | (j_idx == cur - 1)
    valid = j_idx <= cur
    imp = jnp.where(valid, imp + jnp.where(forced, FORCE_BONUS, 0.0), -jnp.inf)
    imp_ref[...] = imp
    n_grp = n_sel // 8
    grp = [imp[8 * r:8 * r + 8] for r in range(n_grp)]
    j_in = lax.broadcasted_iota(jnp.int32, (8, tq), 0)
    rank = [jnp.zeros((8, tq), F32) for _ in range(n_grp)]
    for j2 in range(n_sel):
        if j2 % -(-n_sel // 3) == 0:
            window_scores(j2 // -(-n_sel // 3))
        other = imp_ref[j2:j2 + 1, :]
        for r in range(n_grp):
            if 8 * r > j2:
                before = other >= grp[r]
            elif 8 * r + 7 < j2:
                before = other > grp[r]
            else:
                before = (other > grp[r]) | ((other == grp[r]) & (j_in > j2 - 8 * r))
            rank[r] = rank[r] + jnp.where(before, 1.0, 0.0)
    sel = (jnp.concatenate(rank, axis=0) < min(TOP_N, n_sel)) & valid
    sel_bias = jnp.where(sel, 0.0, NEG).astype(BF16)
    for e in range(NSA_E):
        qt_ref[HD:HD + n_sel, e * tq:(e + 1) * tq] = sel_bias
    if HD + n_sel < qt_ref.shape[0]:
        qt_ref[HD + n_sel:, :] = jnp.zeros((qt_ref.shape[0] - HD - n_sel, lanes), BF16)

    m_w = jnp.maximum(jnp.maximum(_col_max(w_ref[0]), _col_max(w_ref[1])), _col_max(w_ref[2]))
    acc_w = _dot(vwin_ref[0, 0, :, tile_rows(win_tiles[0])], jnp.exp2((w_ref[0] - m_w).astype(BF16)))
    for w in range(1, 3):
        acc_w = acc_w + _dot(vwin_ref[0, 0, :, tile_rows(win_tiles[w])], jnp.exp2((w_ref[w] - m_w).astype(BF16)))
    o_w = acc_w[0:HD, :] * (1.0 / acc_w[HD:HD + 1, :])

    gate = jax.nn.sigmoid(gl_ref[...].T)
    for e in range(NSA_E):
        cs = slice(e * tq, (e + 1) * tq)
        oc_ref[:, cs] = gate[3 * e:3 * e + 1, :] * oc_ref[:, cs] + gate[3 * e + 2:3 * e + 3, :] * o_w[:, cs]

    qk_sel = lambda t: _dot(ksel_ref[0, 0, tile_rows(t), :], qt_ref[...])
    m_ref[...] = jnp.full(m_ref.shape, NEG, F32)
    acc_ref[...] = jnp.zeros(acc_ref.shape, F32)
    s_ref[0] = jnp.where(key_in <= r_in, qk_sel(i), NEG)

    def consume(slot, elem):
        v_tile = tile_rows(jnp.where(elem == 0, i, elem - 1))
        _softmax_pv_step(s_ref[slot], vsel_ref[0, 0, :, v_tile], m_ref, acc_ref)

    def sel_pair(j, carry):
        s_ref[1] = qk_sel(2 * j)
        consume(0, 2 * j)
        s_ref[0] = qk_sel(2 * j + 1)
        consume(1, 2 * j + 1)
        return carry

    lax.fori_loop(0, i // 2, sel_pair, 0)

    @pl.when(i % 2 == 1)
    def _():
        s_ref[1] = qk_sel(i - 1)
        consume(0, i - 1)
        consume(1, i)

    @pl.when(i % 2 == 0)
    def _():
        consume(0, i)

    o_s = acc_ref[0:HD, :] * (1.0 / acc_ref[HD:HD + 1, :])

    ys = []
    for e in range(NSA_E):
        cs = slice(e * tq, (e + 1) * tq)
        ys.append(oc_ref[:, cs] + gate[3 * e + 1:3 * e + 2, :] * o_s[:, cs])
    o_ref[...] = jnp.concatenate(ys, axis=0).T.astype(o_ref.dtype)


def _nsa_attention(q, gl, cmp_kv, cmp_t, ksel, vsel, kwin, vwin, q_gain, b, s, tq=256):
    assert WINDOW == 2 * tq and s % tq == 0
    nq = s // tq
    n_sel = s // SEL_BLOCK
    nrow = s // CMP_STRIDE
    g = NSA_GROUPS
    c0 = jnp.arange(nrow) * CMP_STRIDE
    j0 = jnp.arange(n_sel) * SEL_BLOCK
    ovl = ((c0[None, :] <= j0[:, None] + SEL_BLOCK - 1) & (c0[None, :] + CMP_BLOCK - 1 >= j0[:, None]))
    ovl = ovl.astype(BF16)
    aug_w = max(2 * HD, HD + n_sel)
    assert ksel.shape[-1] == aug_w
    lanes = NSA_E * tq
    kv_spec = lambda w: pl.BlockSpec((1, 1, s, w), lambda bi, gi, i: (bi, gi, 0, 0))
    vt_spec = pl.BlockSpec((1, 1, V_ROWS, s), lambda bi, gi, i: (bi, gi, 0, 0))
    return pl.pallas_call(
        functools.partial(_nsa_kernel, tq=tq),
        grid=(b, g, nq),
        in_specs=[pl.BlockSpec((tq, NSA_E * HD), lambda bi, gi, i: (bi * nq + i, gi)),
                  pl.BlockSpec((tq, LANE), lambda bi, gi, i: (bi * nq + i, gi)),
                  pl.BlockSpec((1, 1, 1, nrow, HD), lambda bi, gi, i: (bi, 0, gi, 0, 0)),
                  pl.BlockSpec((1, 1, 1, HD, nrow), lambda bi, gi, i: (bi, 1, gi, 0, 0)),
                  kv_spec(aug_w), vt_spec, kv_spec(HD), vt_spec,
                  pl.BlockSpec((n_sel, nrow), lambda bi, gi, i: (0, 0)),
                  pl.BlockSpec((HD, 1), lambda bi, gi, i: (0, 0))],
        out_specs=pl.BlockSpec((tq, NSA_E * HD), lambda bi, gi, i: (bi * nq + i, gi)),
        out_shape=jax.ShapeDtypeStruct((b * s, NSA_HEADS * HD), BF16),
        scratch_shapes=[pltpu.VMEM((aug_w, lanes), BF16),
                        pltpu.VMEM((n_sel, tq), F32),
                        pltpu.VMEM((1, lanes), F32),
                        pltpu.VMEM((V_ROWS, lanes), F32),
                        pltpu.VMEM((HD, lanes), F32),
                        pltpu.VMEM((2, tq, lanes), F32),
                        pltpu.VMEM((3, tq, lanes), F32)],
        compiler_params=_cparams(("parallel", "parallel", "arbitrary")),
        name="nsa_attention",
    )(q, gl, cmp_kv, cmp_t, ksel, vsel, kwin, vwin, ovl, q_gain.reshape(HD, 1))


def _softplus(x):
    return jnp.maximum(x, 0.0) + jnp.log1p(jnp.exp(-jnp.abs(x)))


def _split_bf16(x, pieces):
    out = []
    for _ in range(pieces - 1):
        hi = x.astype(BF16)
        out.append(hi)
        x = x - hi.astype(F32)
    out.append(x.astype(BF16))
    return out


def _ssd_kernel(z_ref, xbc_ref, dt_ref, dtT_ref, cw_ref, cb_ref, dtb_ref, dtbT_ref, alog_ref, alogT_ref,
                dskip_ref, ng_ref, exp_ref, shift_ref, o_ref, xpad_ref, state_ref, y_ref):
    c = pl.program_id(1)
    L = xbc_ref.shape[0]
    gw = SSM_INNER // SSM_GROUPS
    e_per_g = SSM_HEADS // SSM_GROUPS

    @pl.when(c == 0)
    def _():
        xpad_ref[0:8, :] = jnp.zeros((8, SSM_CONV_DIM), F32)
        state_ref[...] = jnp.zeros(state_ref.shape, F32)

    xb = xbc_ref[...]
    xf = xb.astype(F32)
    shifted = _dot(shift_ref[...], xb)
    acc = cb_ref[...] + cw_ref[SSM_CONV - 1:SSM_CONV, :] * xf
    for k in range(SSM_CONV - 1):
        acc = acc + cw_ref[k:k + 1, :] * shifted[k * L:(k + 1) * L]
    xpad_ref[8:16, :] = xf[0:8]
    head = cb_ref[...] + cw_ref[0:1, :] * xpad_ref[5:13, :]
    for k in range(1, SSM_CONV):
        head = head + cw_ref[k:k + 1, :] * xpad_ref[5 + k:13 + k, :]
    xpad_ref[0:8, :] = xf[L - 8:L]
    xact = _silu(jnp.concatenate([head, acc[8:]], axis=0))
    xs = xact[:, :SSM_INNER]

    li = lax.broadcasted_iota(jnp.int32, (L, L), 0)
    si = lax.broadcasted_iota(jnp.int32, (L, L), 1)
    causal = li >= si
    log2e = math.log2(math.e)
    dt = _softplus(dt_ref[...] + dtb_ref[...])
    da = jnp.concatenate(_split_bf16(dt * (-log2e * jnp.exp(alog_ref[...])), 3), axis=1)
    r = _dot(jnp.where(causal, 1.0, 0.0).astype(BF16), da)
    a_cs = r[:, 0:LANE] + r[:, LANE:2 * LANE] + r[:, 2 * LANE:]
    dtT = _softplus(dtT_ref[0] + dtbT_ref[...])
    daT = jnp.concatenate(_split_bf16(dtT * (-log2e * jnp.exp(alogT_ref[...])), 3), axis=0)
    rT = _dot(daT, jnp.where(li <= si, 1.0, 0.0).astype(BF16))
    nh = SSM_HEADS
    a_csT = rT[0:nh] + rT[nh:2 * nh] + rT[2 * nh:]

    widen = lambda v: _dot(jnp.concatenate(_split_bf16(v, 2), axis=1), exp_ref[...])
    ea_x = widen(jnp.exp2(a_cs))
    din_x = widen(jnp.exp2(a_cs[L - 1:L, :] - a_cs))
    xd = xs * widen(dt)
    xdd = (xd * din_x).astype(BF16)
    xd = xd.astype(BF16)

    half = L // 2
    first_head = lax.broadcasted_iota(jnp.int32, (L, 2 * SSM_P), 1) < SSM_P
    for g in range(SSM_GROUPS):
        gs = slice(g * gw, (g + 1) * gw)
        bg = xact[:, SSM_INNER + g * SSM_N:SSM_INNER + (g + 1) * SSM_N].astype(BF16)
        cg = xact[:, SSM_INNER + (SSM_GROUPS + g) * SSM_N:SSM_INNER + (SSM_GROUPS + g + 1) * SSM_N].astype(BF16)
        cb = _dot_nt(cg, bg)
        st = state_ref[g]
        y_ref[:, gs] = _dot(cg, st.astype(BF16)) * ea_x[:, gs] + xs[:, gs] * dskip_ref[:, gs]
        for pair in range(e_per_g // 2):
            ps = slice(g * gw + pair * 2 * SSM_P, g * gw + (pair + 1) * 2 * SSM_P)
            y_top = jnp.zeros((half, 2 * SSM_P), F32)
            y_bot = jnp.zeros((half, 2 * SSM_P), F32)
            for sub in range(2):
                h = g * e_per_g + 2 * pair + sub
                xh = jnp.where(first_head if sub == 0 else ~first_head, xd[:, ps], jnp.zeros_like(xd[:, ps]))
                diff = a_cs[:, h:h + 1] - a_csT[h:h + 1, :]
                seg_t = jnp.exp2(jnp.where(causal[:half, :half], diff[:half, :half], NEG))
                seg_b = jnp.exp2(jnp.where(causal[half:, :], diff[half:, :], NEG))
                y_top = y_top + _dot((cb[:half, :half] * seg_t).astype(BF16), xh[:half])
                y_bot = y_bot + _dot((cb[half:, :] * seg_b).astype(BF16), xh)
            y_ref[:half, ps] += y_top
            y_ref[half:, ps] += y_bot
        upd = lax.dot_general(bg, xdd[:, gs], (((0,), (0,)), ((), ())), preferred_element_type=F32)
        state_ref[g] = st * ea_x[L - 1:L, gs] + upd

    y = y_ref[...] * _silu(z_ref[...].astype(F32))
    for g in range(SSM_GROUPS):
        gs = slice(g * gw, (g + 1) * gw)
        o_ref[:, gs] = _rms(y[:, gs], ng_ref[:, gs]).astype(o_ref.dtype)


def _ssd_mixer(z, xbc, dt, conv_w, conv_b, dt_bias, a_log, d_skip, norm_gain, b, s):
    L = math.gcd(SSM_CHUNK, s)
    nc = s // L
    h = SSM_HEADS
    dtT = dt[:, :h].reshape(b, s, h).transpose(0, 2, 1)
    pad_l = lambda v: jnp.zeros((1, LANE), F32).at[0, :h].set(v)
    expand = jnp.zeros((LANE, SSM_INNER), F32).at[:h].set(jnp.repeat(jnp.eye(h, dtype=F32), SSM_P, axis=1))
    expand = jnp.concatenate([expand, expand], axis=0).astype(BF16)
    t_out = jnp.arange(L)[:, None]
    t_in = jnp.arange(L)[None, :]
    shift = jnp.concatenate([(t_in == t_out - (SSM_CONV - 1 - k)) for k in range(SSM_CONV - 1)], axis=0).astype(BF16)
    row = lambda w: pl.BlockSpec((L, w), lambda bi, ci: (bi * nc + ci, 0))
    full = lambda shp: pl.BlockSpec(shp, lambda bi, ci: (0,) * len(shp))
    return pl.pallas_call(
        _ssd_kernel,
        grid=(b, nc),
        in_specs=[row(SSM_INNER), row(SSM_CONV_DIM), row(LANE),
                  pl.BlockSpec((1, h, L), lambda bi, ci: (bi, 0, ci)),
                  full((SSM_CONV, SSM_CONV_DIM)), full((1, SSM_CONV_DIM)),
                  full((1, LANE)), full((h, 1)), full((1, LANE)), full((h, 1)),
                  full((1, SSM_INNER)), full((1, SSM_INNER)), full((2 * LANE, SSM_INNER)),
                  full(((SSM_CONV - 1) * L, L))],
        out_specs=row(SSM_INNER),
        out_shape=jax.ShapeDtypeStruct((b * s, SSM_INNER), BF16),
        scratch_shapes=[pltpu.VMEM((16, SSM_CONV_DIM), F32),
                        pltpu.VMEM((SSM_GROUPS, SSM_N, SSM_INNER // SSM_GROUPS), F32),
                        pltpu.VMEM((L, SSM_INNER), F32)],
        compiler_params=_cparams(("parallel", "arbitrary")),
        name="ssd_mixer",
    )(z, xbc, dt, dtT, conv_w, conv_b.reshape(1, -1), pad_l(dt_bias), dt_bias.reshape(h, 1),
      pad_l(a_log), a_log.reshape(h, 1), jnp.repeat(d_skip, SSM_P).reshape(1, -1),
      norm_gain.reshape(1, -1), expand, shift)


ROUTE_ROWS = 8 + MOE_EXPERTS


def _merge_kernel(x_ref, ya_ref, yb_ref, brg_ref, wa_ref, wb_ref, wo_ref, mg_ref, wr_ref, br_ref,
                  x1_ref, h2_ref, eid_ref, ew_ref):
    tm = x_ref.shape[0]
    gate = jax.nn.sigmoid(brg_ref[...].astype(F32))
    merged = gate[:, :D_MODEL] * _dot(ya_ref[...], wa_ref[...]) + gate[:, D_MODEL:] * _dot(yb_ref[...], wb_ref[...])
    x1 = x_ref[...] + _dot(merged.astype(BF16), wo_ref[...])
    x1_ref[...] = x1
    h2 = _rms(x1, mg_ref[...])
    _rows_to_tiles(h2, h2_ref)

    lg = _dot_nt(wr_ref[...], h2.astype(BF16)) + br_ref[...]
    r = lax.broadcasted_iota(jnp.int32, (ROUTE_ROWS, tm), 0)
    big = ROUTE_ROWS
    is_g = r < MOE_GROUPS
    gmax = jnp.max(jnp.where(is_g, lg, -jnp.inf), axis=0, keepdims=True)
    gsum = jnp.sum(jnp.where(is_g, jnp.exp(lg - gmax), 0.0), axis=0, keepdims=True)
    grp_w = 1.0 / gsum
    grp_i = jnp.min(jnp.where(is_g & (lg == gmax), r, big), axis=0, keepdims=True)
    is_e = (r >= 8) & (((r - 8) // MOE_EPG) == grp_i)
    le = jnp.where(is_e, lg, -jnp.inf)
    emax = jnp.max(le, axis=0, keepdims=True)
    esum = jnp.sum(jnp.where(is_e, jnp.exp(lg - emax), 0.0), axis=0, keepdims=True)
    i1 = jnp.min(jnp.where(le == emax, r, big), axis=0, keepdims=True)
    le2 = jnp.where(r == i1, -jnp.inf, le)
    e2max = jnp.max(le2, axis=0, keepdims=True)
    i2 = jnp.min(jnp.where((le2 == e2max) & is_e & (r != i1), r, big), axis=0, keepdims=True)
    p1 = 1.0 / esum
    p2 = jnp.exp(e2max - emax) / esum
    w1 = grp_w * p1 / (p1 + p2)
    w2 = grp_w * p2 / (p1 + p2)
    r8 = lax.broadcasted_iota(jnp.int32, (8, tm), 0)
    eid_ref[...] = jnp.where(r8 == 0, i1 - 8, jnp.where(r8 == 1, i2 - 8, 0))
    ew_ref[...] = jnp.where(r8 == 0, w1, jnp.where(r8 == 1, w2, 0.0))


def _merge_route(x2, y_a, y_b, brg, w_a, w_b, w_o, moe_gain, w_group, b_group, w_expert, b_expert, tm=512):
    n, d = x2.shape
    tm = min(tm, n)
    wr = jnp.zeros((ROUTE_ROWS, d), F32).at[:MOE_GROUPS].set(w_group.T).at[8:].set(w_expert.T).astype(BF16)
    br = jnp.zeros((ROUTE_ROWS, 1), F32).at[:MOE_GROUPS, 0].set(b_group).at[8:, 0].set(b_expert)
    row = lambda w: pl.BlockSpec((tm, w), lambda i: (i, 0))
    colblk = pl.BlockSpec((8, tm), lambda i: (0, i))
    return pl.pallas_call(
        _merge_kernel,
        grid=(n // tm,),
        in_specs=[row(d), row(Q_W), row(SSM_INNER), row(BRG_W),
                  _const_spec((Q_W, d)), _const_spec((SSM_INNER, d)), _const_spec((d, d)),
                  _const_spec((1, d)), _const_spec((ROUTE_ROWS, d)), _const_spec((ROUTE_ROWS, 1))],
        out_specs=(row(d), pl.BlockSpec((tm, ROW_SUB, LANE), lambda i: (i, 0, 0)), colblk, colblk),
        out_shape=(jax.ShapeDtypeStruct((n, d), F32), jax.ShapeDtypeStruct((n, ROW_SUB, LANE), ROW_DTYPE),
                   jax.ShapeDtypeStruct((8, n), jnp.int32), jax.ShapeDtypeStruct((8, n), F32)),
        compiler_params=_cparams(("parallel",)),
        name="merge_route",
    )(x2, y_a, y_b, brg, w_a.astype(BF16), w_b.astype(BF16), w_o.astype(BF16), moe_gain.reshape(1, d), wr, br)


def _rank_kernel(eid_ref, rank_ref, cnt_ref, run_ref, tri_ref):
    i = pl.program_id(0)
    tm = eid_ref.shape[1]

    @pl.when(i == 0)
    def _():
        run_ref[...] = jnp.zeros(run_ref.shape, F32)
        earlier = lax.broadcasted_iota(jnp.int32, (tm, tm), 0) < lax.broadcasted_iota(jnp.int32, (tm, tm), 1)
        tri_ref[...] = jnp.where(earlier, 1.0, 0.0).astype(BF16)

    r = lax.broadcasted_iota(jnp.int32, (MOE_EXPERTS, tm), 0)
    oh0 = r == eid_ref[0:1, :]
    oh1 = r == eid_ref[1:2, :]
    oh = jnp.where(oh0 | oh1, 1.0, 0.0)
    prefix = _dot(oh.astype(BF16), tri_ref[...]) + run_ref[:, 0:1]
    rank0 = jnp.sum(jnp.where(oh0, prefix, 0.0), axis=0, keepdims=True)
    rank1 = jnp.sum(jnp.where(oh1, prefix, 0.0), axis=0, keepdims=True)
    r8 = lax.broadcasted_iota(jnp.int32, (8, tm), 0)
    rank_ref[...] = jnp.where(r8 == 0, rank0, jnp.where(r8 == 1, rank1, 0.0)).astype(jnp.int32)
    run_ref[...] = run_ref[...] + jnp.sum(oh, axis=1, keepdims=True)
    cnt_ref[...] = run_ref[...]


def _moe_rank(eid, tm=512):
    n = eid.shape[1]
    tm = min(tm, n)
    return pl.pallas_call(
        _rank_kernel,
        grid=(n // tm,),
        in_specs=[pl.BlockSpec((8, tm), lambda i: (0, i))],
        out_specs=(pl.BlockSpec((8, tm), lambda i: (0, i)),
                   pl.BlockSpec((MOE_EXPERTS, LANE), lambda i: (0, 0))),
        out_shape=(jax.ShapeDtypeStruct((8, n), jnp.int32),
                   jax.ShapeDtypeStruct((MOE_EXPERTS, LANE), F32)),
        scratch_shapes=[pltpu.VMEM((MOE_EXPERTS, LANE), F32), pltpu.VMEM((tm, tm), BF16)],
        compiler_params=_cparams(("arbitrary",)),
        name="moe_rank",
    )(eid)


def _row_copy(src, dst, sem):
    return pltpu.make_async_copy(src, dst, sem)


ROW_SUB = D_MODEL // LANE
ROW_DTYPE = BF16
ROW_UNROLL = 8


def _rows_to_tiles(x, ref):
    ref[...] = x.astype(ref.dtype).reshape(ref.shape)


def _tiles_to_rows(ref):
    x = ref[...]
    return x.reshape(x.shape[0], ROW_SUB * LANE)


def _dispatch_kernel(pad_lo_ref, pad_hi_ref, slot_ref, h_ref, xs_ref, zero_ref, sem, pad_sem):
    tm = h_ref.shape[0]

    @pl.when(pl.program_id(0) == 0)
    def _():
        zero_ref[...] = jnp.zeros(zero_ref.shape, zero_ref.dtype)

        def per_expert(e, carry):
            def start(r, c):
                _row_copy(zero_ref.at[0], xs_ref.at[r], pad_sem).start()
                return c

            def wait(r, c):
                _row_copy(zero_ref.at[0], xs_ref.at[0], pad_sem).wait()
                return c

            lax.fori_loop(pad_lo_ref[e], pad_hi_ref[e], start, 0)
            lax.fori_loop(pad_lo_ref[e], pad_hi_ref[e], wait, 0)
            return carry

        lax.fori_loop(0, pad_lo_ref.shape[0], per_expert, 0)

    def issue(t, carry):
        for u in range(ROW_UNROLL):
            n = t * ROW_UNROLL + u
            for k in range(MOE_TOP_K):
                _row_copy(h_ref.at[n], xs_ref.at[slot_ref[k, n]], sem).start(priority=k)
        return carry

    lax.fori_loop(0, tm // ROW_UNROLL, issue, 0)

    def drain(t, carry):
        for _ in range(ROW_UNROLL * MOE_TOP_K):
            _row_copy(h_ref.at[0], xs_ref.at[0], sem).wait()
        return carry

    lax.fori_loop(0, tm // ROW_UNROLL, drain, 0)


def _moe_dispatch(pad_lo, pad_hi, slot, h2, n_slots, tm=256):
    n = h2.shape[0]
    tm = min(tm, n)
    return pl.pallas_call(
        _dispatch_kernel,
        grid_spec=pltpu.PrefetchScalarGridSpec(
            num_scalar_prefetch=2,
            grid=(n // tm,),
            in_specs=[pl.BlockSpec((MOE_TOP_K, tm), lambda i, lo, hi: (0, i), memory_space=pltpu.SMEM),
                      pl.BlockSpec((tm, ROW_SUB, LANE), lambda i, lo, hi: (i, 0, 0))],
            out_specs=pl.BlockSpec(memory_space=pl.ANY),
            scratch_shapes=[pltpu.VMEM((1, ROW_SUB, LANE), h2.dtype),
                            pltpu.SemaphoreType.DMA, pltpu.SemaphoreType.DMA]),
        out_shape=jax.ShapeDtypeStruct((n_slots, ROW_SUB, LANE), h2.dtype),
        compiler_params=_cparams(("arbitrary",)),
        name="moe_dispatch",
    )(pad_lo, pad_hi, slot, h2)


def _expert_kernel(be_ref, nu_ref, xs_ref, wg_ref, wu_ref, wd_ref, o_ref):
    j = pl.program_id(0)

    @pl.when(j < nu_ref[0])
    def _():
        x = _tiles_to_rows(xs_ref).astype(BF16)
        act = _silu(_dot(x, wg_ref[0])) * _dot(x, wu_ref[0])
        _rows_to_tiles(_dot(act.astype(BF16), wd_ref[0]), o_ref)

    @pl.when(j >= nu_ref[0])
    def _():
        o_ref[...] = jnp.zeros(o_ref.shape, o_ref.dtype)


def _moe_experts(block_e, n_used, xs, w_gate, w_up, w_down):
    n_slots = xs.shape[0]
    d = D_MODEL
    rb = MOE_ROW_BLOCK
    wspec = lambda shp: pl.BlockSpec((1,) + shp, lambda j, be, nu: (be[j], 0, 0))
    rows = pl.BlockSpec((rb, ROW_SUB, LANE), lambda j, be, nu: (j, 0, 0))
    return pl.pallas_call(
        _expert_kernel,
        grid_spec=pltpu.PrefetchScalarGridSpec(
            num_scalar_prefetch=2,
            grid=(n_slots // rb,),
            in_specs=[rows, wspec((d, MOE_HIDDEN)), wspec((d, MOE_HIDDEN)), wspec((MOE_HIDDEN, d))],
            out_specs=rows),
        out_shape=jax.ShapeDtypeStruct(xs.shape, ROW_DTYPE),
        compiler_params=_cparams(("arbitrary",)),
        name="moe_experts",
    )(block_e, n_used, xs, w_gate.astype(BF16), w_up.astype(BF16), w_down.astype(BF16))


def _combine_kernel(slot_ref, slot_next_ref, ew_ref, x1_ref, p_ref, eo_ref, pg_ref, wg_ref, wp_ref, o_ref,
                    rows_ref, sems):
    i = pl.program_id(0)
    n_steps = pl.num_programs(0)
    tm = x1_ref.shape[0]

    def gather(idx_ref, buf):
        def issue(t, carry):
            for u in range(ROW_UNROLL):
                n = t * ROW_UNROLL + u
                for k in range(MOE_TOP_K):
                    _row_copy(eo_ref.at[idx_ref[k, n]], rows_ref.at[buf, k, n], sems.at[buf]).start(priority=k)
            return carry

        lax.fori_loop(0, tm // ROW_UNROLL, issue, 0)

    @pl.when(i == 0)
    def _():
        gather(slot_ref, 0)

    @pl.when(i + 1 < n_steps)
    def _():
        gather(slot_next_ref, (i + 1) % 2)

    buf = i % 2

    def drain(t, carry):
        for _ in range(ROW_UNROLL * MOE_TOP_K):
            _row_copy(eo_ref.at[0], rows_ref.at[buf, 0, 0], sems.at[buf]).wait()
        return carry

    lax.fori_loop(0, tm // ROW_UNROLL, drain, 0)

    rows = lambda k: _tiles_to_rows(rows_ref.at[buf, k]).astype(F32)
    x2 = x1_ref[...] + ew_ref[:, 0:1] * rows(0) + ew_ref[:, 1:2] * rows(1)
    h3 = _rms(x2, pg_ref[...]).astype(BF16)
    o_ref[...] = x2 + jax.nn.sigmoid(_dot(h3, wg_ref[...])) * _dot(p_ref[...].astype(BF16), wp_ref[...])


def _moe_combine_ple(slot, ew_cols, x1, p2, expert_out, ple_gain, w_gate, w_proj, tm=256):
    n, d = x1.shape
    tm = min(tm, n)
    row = lambda w: pl.BlockSpec((tm, w), lambda i: (i, 0))
    last = n // tm - 1
    idx = lambda fn: pl.BlockSpec((MOE_TOP_K, tm), fn, memory_space=pltpu.SMEM)
    return pl.pallas_call(
        _combine_kernel,
        grid=(n // tm,),
        in_specs=[idx(lambda i: (0, i)), idx(lambda i: (0, jnp.minimum(i + 1, last))),
                  row(8), row(d), row(PLE_DIM),
                  pl.BlockSpec(memory_space=pl.ANY),
                  _const_spec((1, d)), _const_spec((d, d)), _const_spec((PLE_DIM, d))],
        out_specs=row(d),
        out_shape=jax.ShapeDtypeStruct((n, d), F32),
        scratch_shapes=[pltpu.VMEM((2, MOE_TOP_K, tm, ROW_SUB, LANE), ROW_DTYPE),
                        pltpu.SemaphoreType.DMA((2,))],
        compiler_params=_cparams(("arbitrary",)),
        name="moe_combine_ple",
    )(slot, slot, ew_cols, x1, p2, expert_out, ple_gain.reshape(1, d), w_gate.astype(BF16), w_proj.astype(BF16))


def _layer(x2, p2, b, s, mix_norm, w_in, nsa_q_gain, nsa_kc_gain, nsa_ks_gain, nsa_kw_gain,
           cmp_pos_k, cmp_w1_k, cmp_w2_k, cmp_pos_v, cmp_w1_v, cmp_w2_v,
           ssm_conv_w, ssm_conv_b, ssm_dt_bias, ssm_a_log, ssm_d, ssm_norm,
           w_branch_a, w_branch_b, w_out,
           moe_norm, moe_w_group, moe_b_group, moe_w_expert, moe_b_expert,
           moe_w_gate, moe_w_up, moe_w_down, ple_norm, ple_w_gate, ple_w_proj):
    n = b * s
    q, kvc, kvsw, gl, z, xbc, dt, brg = _in_proj(x2, mix_norm, _pack_w_in(w_in))

    cmp_kv, cmp_t = _compress(kvc, jnp.stack([cmp_pos_k, cmp_pos_v]), jnp.stack([cmp_w1_k, cmp_w1_v]),
                              jnp.stack([cmp_w2_k, cmp_w2_v]), nsa_kc_gain, b, s)
    ksel, vsel, kwin, vwin = _kv_prep(kvsw, nsa_ks_gain, nsa_kw_gain, b, s)
    y_a = _nsa_attention(q, gl, cmp_kv, cmp_t, ksel, vsel, kwin, vwin, nsa_q_gain, b, s)
    y_b = _ssd_mixer(z, xbc, dt, ssm_conv_w, ssm_conv_b, ssm_dt_bias, ssm_a_log, ssm_d, ssm_norm, b, s)

    x1, h2, eid, ew = _merge_route(x2, y_a, y_b, brg, w_branch_a, w_branch_b, w_out, moe_norm,
                                   moe_w_group, moe_b_group, moe_w_expert, moe_b_expert)

    rank, cnt = _moe_rank(eid)
    rb = MOE_ROW_BLOCK
    counts = cnt[:, 0].astype(jnp.int32)
    padded = (counts + rb - 1) // rb * rb
    pend = jnp.cumsum(padded)
    pstart = pend - padded
    e_ids = jnp.arange(MOE_EXPERTS, dtype=jnp.int32)
    is_e = eid[None, :MOE_TOP_K] == e_ids[:, None, None]
    slot = rank[:MOE_TOP_K] + jnp.sum(jnp.where(is_e, pstart[:, None, None], 0), axis=0)
    n_blocks = -(-(n * MOE_TOP_K) // rb) + MOE_EXPERTS
    first_row = jnp.arange(n_blocks, dtype=jnp.int32) * rb
    block_e = jnp.minimum(jnp.sum(pend[None, :] <= first_row[:, None], axis=1), MOE_EXPERTS - 1)
    n_used = (pend[-1:] // rb).astype(jnp.int32)

    n_slots = n_blocks * rb
    pad_lo = jnp.concatenate([pstart + counts, pend[-1:]])
    pad_hi = jnp.concatenate([pend, jnp.full((1,), n_slots, pend.dtype)])
    xs = _moe_dispatch(pad_lo, pad_hi, slot, h2, n_slots)
    expert_out = _moe_experts(block_e.astype(jnp.int32), n_used, xs, moe_w_gate, moe_w_up, moe_w_down)
    ew_cols = ew.T
    return _moe_combine_ple(slot, ew_cols, x1, p2, expert_out, ple_norm, ple_w_gate, ple_w_proj)


def kernel(x, p, mix_norm, w_in, nsa_q_gain, nsa_kc_gain, nsa_ks_gain, nsa_kw_gain, cmp_pos_k, cmp_w1_k, cmp_w2_k, cmp_pos_v, cmp_w1_v, cmp_w2_v, ssm_conv_w, ssm_conv_b, ssm_dt_bias, ssm_a_log, ssm_d, ssm_norm, w_branch_a, w_branch_b, w_out, moe_norm, moe_w_group, moe_b_group, moe_w_expert, moe_b_expert, moe_w_gate, moe_w_up, moe_w_down, ple_norm, ple_w_gate, ple_w_proj):
    b, s, d = x.shape
    params = (mix_norm, w_in, nsa_q_gain, nsa_kc_gain, nsa_ks_gain, nsa_kw_gain,
              cmp_pos_k, cmp_w1_k, cmp_w2_k, cmp_pos_v, cmp_w1_v, cmp_w2_v,
              ssm_conv_w, ssm_conv_b, ssm_dt_bias, ssm_a_log, ssm_d, ssm_norm,
              w_branch_a, w_branch_b, w_out,
              moe_norm, moe_w_group, moe_b_group, moe_w_expert, moe_b_expert,
              moe_w_gate, moe_w_up, moe_w_down, ple_norm, ple_w_gate, ple_w_proj)
    x2 = x.reshape(b * s, d)
    for i in range(p.shape[0]):
        x2 = _layer(x2, p[i].reshape(b * s, -1), b, s, *(w[i] for w in params))
    return x2.reshape(b, s, d)
```

```python
import functools
import math

import jax
import jax.numpy as jnp
from jax import lax
from jax.experimental import pallas as pl
from jax.experimental.pallas import tpu as pltpu

F32 = jnp.float32
BF16 = jnp.bfloat16

D_MODEL = 1024
RMS_EPS = 1e-6
NEG = -1e30

NSA_HEADS = 8
NSA_GROUPS = 2
NSA_E = NSA_HEADS // NSA_GROUPS
HD = 64
CMP_BLOCK = 32
CMP_STRIDE = 16
CMP_HIDDEN = 128
SEL_BLOCK = 64
TOP_N = 16
WINDOW = 512
FORCE_BONUS = 1e3

SSM_HEADS = 16
SSM_P = 64
SSM_INNER = SSM_HEADS * SSM_P
SSM_GROUPS = 2
SSM_N = 128
SSM_CONV = 4
SSM_CHUNK = 256
SSM_CONV_DIM = SSM_INNER + 2 * SSM_GROUPS * SSM_N

MOE_GROUPS = 4
MOE_EPG = 8
MOE_EXPERTS = MOE_GROUPS * MOE_EPG
MOE_TOP_K = 2
MOE_HIDDEN = 256
MOE_ROW_BLOCK = 256
PLE_DIM = 256

LANE = 128
VMEM_LIMIT = 52 * 1024 * 1024

Q_W = NSA_HEADS * HD
KVC_W = 2 * NSA_GROUPS * HD
KVSW_W = 4 * NSA_GROUPS * HD
GL_W = NSA_GROUPS * LANE
DT_W = LANE
BRG_W = 2 * D_MODEL
SEG_WIDTHS = (Q_W, KVC_W, KVSW_W, GL_W, SSM_INNER, SSM_CONV_DIM, DT_W, BRG_W)
SEG_DTYPES = (BF16, BF16, BF16, F32, BF16, BF16, F32, BF16)
PACKED_W = sum(SEG_WIDTHS)


def _cparams(sem):
    return pltpu.CompilerParams(dimension_semantics=sem, vmem_limit_bytes=VMEM_LIMIT)


def _const_spec(shape):
    n = len(shape)
    return pl.BlockSpec(shape, lambda *_: (0,) * n, pipeline_mode=pl.Buffered(1))


def _rms(xf, gain):
    return xf * lax.rsqrt(jnp.mean(xf * xf, axis=-1, keepdims=True) + RMS_EPS) * gain


def _silu(x):
    return x * jax.nn.sigmoid(x)


def _dot(a, b):
    return jnp.dot(a, b, preferred_element_type=F32)


def _dot_nt(a, b):
    return lax.dot_general(a, b, (((1,), (1,)), ((), ())), preferred_element_type=F32)


KVC_SEG = 1
CHUNK_W = CMP_STRIDE * NSA_GROUPS * HD


def _inproj_kernel(x_ref, g_ref, w_ref, *refs):
    outs, h_scr = refs[:-1], refs[-1]
    tm = x_ref.shape[0]
    h_scr[...] = _rms(x_ref[...], g_ref[...]).astype(BF16)
    off = 0
    for seg, (o_ref, width) in enumerate(zip(outs, SEG_WIDTHS)):
        if seg == KVC_SEG:
            val = _dot(h_scr[...], w_ref[:, off:off + width])
            for j in range(2):
                part = val[:, j * LANE:(j + 1) * LANE].reshape(tm // CMP_STRIDE, CMP_STRIDE, LANE)
                o_ref[:, j * CHUNK_W:(j + 1) * CHUNK_W] = part.reshape(tm // CMP_STRIDE, CHUNK_W).astype(o_ref.dtype)
        else:
            for lo in range(0, width, 512):
                hi = min(lo + 512, width)
                o_ref[:, lo:hi] = _dot(h_scr[...], w_ref[:, off + lo:off + hi]).astype(o_ref.dtype)
        off += width


def _pack_w_in(w_in):
    sizes = (Q_W, 128, 128, 128, 128, 128, 128, NSA_HEADS * 3, SSM_INNER, SSM_CONV_DIM, SSM_HEADS, BRG_W)
    offs = [0]
    for s in sizes:
        offs.append(offs[-1] + s)
    seg = lambda i: w_in[:, offs[i]:offs[i + 1]]
    q, kc, vc, ks, vs, kw, vw, ng, z, xbc, dt, brg = (seg(i) for i in range(12))
    d = w_in.shape[0]
    pad = lambda a, w: jnp.concatenate([a, jnp.zeros((d, w - a.shape[1]), a.dtype)], axis=1)
    per_g = NSA_E * 3
    gl = jnp.concatenate([pad(ng[:, g * per_g:(g + 1) * per_g], LANE) for g in range(NSA_GROUPS)], axis=1)
    packed = jnp.concatenate([q, kc, vc, ks, vs, kw, vw, gl, z, xbc, pad(dt, DT_W), brg], axis=1)
    return packed.astype(BF16)


def _in_proj(x2, gain, w_packed, tm=512):
    n, d = x2.shape
    tm = min(tm, n)
    shapes = [(n, w) for w in SEG_WIDTHS]
    blocks = [(tm, w) for w in SEG_WIDTHS]
    shapes[KVC_SEG] = (n // CMP_STRIDE, 2 * CHUNK_W)
    blocks[KVC_SEG] = (tm // CMP_STRIDE, 2 * CHUNK_W)
    return pl.pallas_call(
        _inproj_kernel,
        grid=(n // tm,),
        in_specs=[pl.BlockSpec((tm, d), lambda i: (i, 0)),
                  _const_spec((1, d)),
                  _const_spec((d, PACKED_W))],
        out_specs=tuple(pl.BlockSpec(blk, lambda i: (i, 0)) for blk in blocks),
        out_shape=tuple(jax.ShapeDtypeStruct(shp, dt) for shp, dt in zip(shapes, SEG_DTYPES)),
        scratch_shapes=[pltpu.VMEM((tm, d), BF16)],
        compiler_params=_cparams(("parallel",)),
        name="in_proj",
    )(x2, gain.reshape(1, d), w_packed)


def _compress_kernel(x_ref, w1x_ref, w1_ref, w2_ref, w2t_ref, pos_ref, gain_ref, o_ref, ot_ref):
    kv = pl.program_id(1)
    x = x_ref[...]
    h1 = _dot(x, w1x_ref[0, 0, 0])
    h2 = _dot(x, w1x_ref[0, 0, 1])
    nrow = x.shape[0]
    h2 = pltpu.roll(h2, nrow - 1, 0)
    bias = _dot(pos_ref[0], w1_ref[0])[0:1]
    hid = _silu(h1 + h2 + bias)
    out = _dot(hid.astype(BF16), w2_ref[0])
    normed = _rms(out, gain_ref[...])
    o_ref[0, 0, 0] = jnp.where(kv == 0, normed, out).astype(o_ref.dtype)
    ot_ref[0, 0, 0] = _dot_nt(w2t_ref[0], hid.astype(BF16)).astype(ot_ref.dtype)


def _compress(kvc, pos, w1, w2, kc_gain, b, s):
    nrow = s // CMP_STRIDE
    cl = CMP_BLOCK * HD
    pos_flat = jnp.zeros((2, 8, cl), BF16).at[:, 0, :].set(pos.reshape(2, cl).astype(BF16))
    w1r = w1.reshape(2, 2, CMP_STRIDE, 1, HD, CMP_HIDDEN)
    own = jnp.eye(NSA_GROUPS, dtype=w1.dtype)
    w1x = w1r[:, None] * own[None, :, None, None, :, None, None]
    w1x = w1x.reshape(2, NSA_GROUPS, 2, CHUNK_W, CMP_HIDDEN).astype(BF16)
    return pl.pallas_call(
        _compress_kernel,
        grid=(b, 2, NSA_GROUPS),
        in_specs=[pl.BlockSpec((nrow, CHUNK_W), lambda i, k, g: (i, k)),
                  pl.BlockSpec((1, 1, 2, CHUNK_W, CMP_HIDDEN), lambda i, k, g: (k, g, 0, 0, 0)),
                  pl.BlockSpec((1, cl, CMP_HIDDEN), lambda i, k, g: (k, 0, 0)),
                  pl.BlockSpec((1, CMP_HIDDEN, HD), lambda i, k, g: (k, 0, 0)),
                  pl.BlockSpec((1, HD, CMP_HIDDEN), lambda i, k, g: (k, 0, 0)),
                  pl.BlockSpec((1, 8, cl), lambda i, k, g: (k, 0, 0)),
                  pl.BlockSpec((1, HD), lambda i, k, g: (0, 0))],
        out_specs=(pl.BlockSpec((1, 1, 1, nrow, HD), lambda i, k, g: (i, k, g, 0, 0)),
                   pl.BlockSpec((1, 1, 1, HD, nrow), lambda i, k, g: (i, k, g, 0, 0))),
        out_shape=(jax.ShapeDtypeStruct((b, 2, NSA_GROUPS, nrow, HD), BF16),
                   jax.ShapeDtypeStruct((b, 2, NSA_GROUPS, HD, nrow), BF16)),
        compiler_params=_cparams(("parallel", "parallel", "parallel")),
        name="nsa_compress",
    )(kvc, w1x, w1.astype(BF16), w2.astype(BF16), w2.transpose(0, 2, 1).astype(BF16), pos_flat,
      kc_gain.reshape(1, HD))


V_ROWS = HD + 16


def _kvprep_kernel(x_ref, gs_ref, gw_ref, ksel_ref, vsel_ref, kwin_ref, vwin_ref):
    ts = x_ref.shape[0]
    base = pl.program_id(1) * ts
    x = x_ref[...].astype(F32)
    n_hot = ksel_ref.shape[-1] - HD
    blk = (base + lax.broadcasted_iota(jnp.int32, (ts, n_hot), 0)) // SEL_BLOCK
    onehot = jnp.where(blk == lax.broadcasted_iota(jnp.int32, (ts, n_hot), 1), 1.0, 0.0)
    ones_row = jnp.where(lax.broadcasted_iota(jnp.int32, (V_ROWS - HD, ts), 0) == 0, 1.0, 0.0)
    vs_t = x[:, LANE:2 * LANE].T
    vw_t = x[:, 3 * LANE:4 * LANE].T
    for g in range(NSA_GROUPS):
        col = lambda j: x[:, j * LANE + g * HD:j * LANE + (g + 1) * HD]
        ksel_ref[0, g] = jnp.concatenate([_rms(col(0), gs_ref[...]), onehot], axis=1).astype(BF16)
        kwin_ref[0, g] = _rms(col(2), gw_ref[...]).astype(BF16)
        vsel_ref[0, g] = jnp.concatenate([vs_t[g * HD:(g + 1) * HD], ones_row], axis=0).astype(BF16)
        vwin_ref[0, g] = jnp.concatenate([vw_t[g * HD:(g + 1) * HD], ones_row], axis=0).astype(BF16)


def _kv_prep(kvsw, ks_gain, kw_gain, b, s, ts=1024):
    ts = min(ts, s)
    g = NSA_GROUPS
    aug_w = max(LANE, HD + s // SEL_BLOCK)
    spec = lambda w: pl.BlockSpec((1, g, ts, w), lambda i, j: (i, 0, j, 0))
    spec_t = pl.BlockSpec((1, g, V_ROWS, ts), lambda i, j: (i, 0, 0, j))
    shape_t = jax.ShapeDtypeStruct((b, g, V_ROWS, s), BF16)
    return pl.pallas_call(
        _kvprep_kernel,
        grid=(b, s // ts),
        in_specs=[pl.BlockSpec((ts, KVSW_W), lambda i, j: (i * (s // ts) + j, 0)),
                  pl.BlockSpec((1, HD), lambda i, j: (0, 0)),
                  pl.BlockSpec((1, HD), lambda i, j: (0, 0))],
        out_specs=(spec(aug_w), spec_t, spec(HD), spec_t),
        out_shape=(jax.ShapeDtypeStruct((b, g, s, aug_w), BF16), shape_t,
                   jax.ShapeDtypeStruct((b, g, s, HD), BF16), shape_t),
        compiler_params=_cparams(("parallel", "parallel")),
        name="nsa_kv_prep",
    )(kvsw, ks_gain.reshape(1, HD), kw_gain.reshape(1, HD))


def _softmax_pv_step(s, v_t, m_ref, acc_ref):
    m_old = m_ref[...]
    m_new = jnp.maximum(m_old, _col_max(s))
    alpha = jnp.exp2(m_old - m_new)
    p = jnp.exp2((s - m_new).astype(BF16))
    acc_ref[...] = alpha * acc_ref[...] + _dot(v_t, p)
    m_ref[...] = m_new


def _col_max(s):
    rows, lanes = s.shape
    return jnp.max(jnp.max(s.reshape(4, rows // 4, lanes), axis=0), axis=0, keepdims=True)


def _nsa_kernel(q_ref, gl_ref, kc_ref, vct_ref, ksel_ref, vsel_ref, kwin_ref, vwin_ref, ovl_ref, qg_ref,
                o_ref, qt_ref, imp_ref, m_ref, acc_ref, oc_ref, s_ref, w_ref, *, tq):
    i = pl.program_id(2)
    lanes = NSA_E * tq
    n_sel = ovl_ref.shape[0]
    scale = HD ** -0.5 * math.log2(math.e)

    q_t = q_ref[...].astype(F32).T
    for e in range(NSA_E):
        qe = q_t[e * HD:(e + 1) * HD, :]
        inv = lax.rsqrt(jnp.mean(qe * qe, axis=0, keepdims=True) + RMS_EPS)
        qt_ref[0:HD, e * tq:(e + 1) * tq] = (qe * inv * (qg_ref[...] * scale)).astype(BF16)
    q64 = qt_ref[0:HD, :]

    r_in = lax.broadcasted_iota(jnp.int32, (1, lanes), 1) & (tq - 1)
    t_lane = i * tq + r_in
    key_in = lax.broadcasted_iota(jnp.int32, (tq, 1), 0)
    tile_rows = lambda t: pl.ds(pl.multiple_of(t * tq, tq), tq)

    big = 2 * tq
    win_tiles = (i, jnp.maximum(i - 1, 0), jnp.maximum(i - 2, 0))
    win_masks = (lambda: key_in <= r_in,
                 lambda: key_in > jnp.where(i >= 1, -1, big),
                 lambda: key_in > r_in + jnp.where(i >= 2, 0, big))

    def window_scores(w):
        w_ref[w] = jnp.where(win_masks[w](), _dot(kwin_ref[0, 0, tile_rows(win_tiles[w]), :], q64), NEG)

    n_cmp = kc_ref.shape[3]
    s_c = _dot(kc_ref[0, 0, 0], q64)
    c_end = lax.broadcasted_iota(jnp.int32, (n_cmp, 1), 0) * CMP_STRIDE + (CMP_BLOCK - 1)
    s_c = jnp.where(c_end <= t_lane, s_c, NEG)
    e_c = jnp.exp2(s_c - jnp.max(s_c, axis=0, keepdims=True))
    inv_c = jnp.where(t_lane >= CMP_BLOCK - 1, 1.0 / jnp.sum(e_c, axis=0, keepdims=True), 0.0)
    p_c = e_c * inv_c
    oc_ref[...] = _dot(vct_ref[0, 0, 0], p_c.astype(BF16))

    p_sum = p_c[:, 0:tq]
    for e in range(1, NSA_E):
        p_sum = p_sum + p_c[:, e * tq:(e + 1) * tq]
    p_hi = p_sum.astype(BF16)
    p_lo = (p_sum - p_hi.astype(F32)).astype(BF16)
    imp = _dot(ovl_ref[...], p_hi) + _dot(ovl_ref[...], p_lo)
    j_idx = lax.broadcasted_iota(jnp.int32, (n_sel, tq), 0)
    cur = (i * tq + lax.broadcasted_iota(jnp.int32, (n_sel, tq), 1)) // SEL_BLOCK
    forced = (j_idx == 0) | (j_idx == cur) | (j_idx == cur - 1)
    valid = j_idx <= cur
    imp = jnp.where(valid, imp + jnp.where(forced, FORCE_BONUS, 0.0), -jnp.inf)
    imp_ref[...] = imp
    n_grp = n_sel // 8
    grp = [imp[8 * r:8 * r + 8] for r in range(n_grp)]
    j_in = lax.broadcasted_iota(jnp.int32, (8, tq), 0)
    rank = [jnp.zeros((8, tq), F32) for _ in range(n_grp)]
    for j2 in range(n_sel):
        if j2 % -(-n_sel // 3) == 0:
            window_scores(j2 // -(-n_sel // 3))
        other = imp_ref[j2:j2 + 1, :]
        for r in range(n_grp):
            if 8 * r > j2:
                before = other >= grp[r]
            elif 8 * r + 7 < j2:
                before = other > grp[r]
            else:
                before = (other > grp[r]) | ((other == grp[r]) & (j_in > j2 - 8 * r))
            rank[r] = rank[r] + jnp.where(before, 1.0, 0.0)
    sel = (jnp.concatenate(rank, axis=0) < min(TOP_N, n_sel)) & valid
    sel_bias = jnp.where(sel, 0.0, NEG).astype(BF16)
    for e in range(NSA_E):
        qt_ref[HD:HD + n_sel, e * tq:(e + 1) * tq] = sel_bias
    if HD + n_sel < qt_ref.shape[0]:
        qt_ref[HD + n_sel:, :] = jnp.zeros((qt_ref.shape[0] - HD - n_sel, lanes), BF16)

    m_w = jnp.maximum(jnp.maximum(_col_max(w_ref[0]), _col_max(w_ref[1])), _col_max(w_ref[2]))
    acc_w = _dot(vwin_ref[0, 0, :, tile_rows(win_tiles[0])], jnp.exp2((w_ref[0] - m_w).astype(BF16)))
    for w in range(1, 3):
        acc_w = acc_w + _dot(vwin_ref[0, 0, :, tile_rows(win_tiles[w])], jnp.exp2((w_ref[w] - m_w).astype(BF16)))
    o_w = acc_w[0:HD, :] * (1.0 / acc_w[HD:HD + 1, :])

    gate = jax.nn.sigmoid(gl_ref[...].T)
    for e in range(NSA_E):
        cs = slice(e * tq, (e + 1) * tq)
        oc_ref[:, cs] = gate[3 * e:3 * e + 1, :] * oc_ref[:, cs] + gate[3 * e + 2:3 * e + 3, :] * o_w[:, cs]

    qk_sel = lambda t: _dot(ksel_ref[0, 0, tile_rows(t), :], qt_ref[...])
    m_ref[...] = jnp.full(m_ref.shape, NEG, F32)
    acc_ref[...] = jnp.zeros(acc_ref.shape, F32)
    s_ref[0] = jnp.where(key_in <= r_in, qk_sel(i), NEG)

    def consume(slot, elem):
        v_tile = tile_rows(jnp.where(elem == 0, i, elem - 1))
        _softmax_pv_step(s_ref[slot], vsel_ref[0, 0, :, v_tile], m_ref, acc_ref)

    def sel_pair(j, carry):
        s_ref[1] = qk_sel(2 * j)
        consume(0, 2 * j)
        s_ref[0] = qk_sel(2 * j + 1)
        consume(1, 2 * j + 1)
        return carry

    lax.fori_loop(0, i // 2, sel_pair, 0)

    @pl.when(i % 2 == 1)
    def _():
        s_ref[1] = qk_sel(i - 1)
        consume(0, i - 1)
        consume(1, i)

    @pl.when(i % 2 == 0)
    def _():
        consume(0, i)

    o_s = acc_ref[0:HD, :] * (1.0 / acc_ref[HD:HD + 1, :])

    ys = []
    for e in range(NSA_E):
        cs = slice(e * tq, (e + 1) * tq)
        ys.append(oc_ref[:, cs] + gate[3 * e + 1:3 * e + 2, :] * o_s[:, cs])
    o_ref[...] = jnp.concatenate(ys, axis=0).T.astype(o_ref.dtype)


def _nsa_attention(q, gl, cmp_kv, cmp_t, ksel, vsel, kwin, vwin, q_gain, b, s, tq=256):
    assert WINDOW == 2 * tq and s % tq == 0
    nq = s // tq
    n_sel = s // SEL_BLOCK
    nrow = s // CMP_STRIDE
    g = NSA_GROUPS
    c0 = jnp.arange(nrow) * CMP_STRIDE
    j0 = jnp.arange(n_sel) * SEL_BLOCK
    ovl = ((c0[None, :] <= j0[:, None] + SEL_BLOCK - 1) & (c0[None, :] + CMP_BLOCK - 1 >= j0[:, None]))
    ovl = ovl.astype(BF16)
    aug_w = max(2 * HD, HD + n_sel)
    assert ksel.shape[-1] == aug_w
    lanes = NSA_E * tq
    kv_spec = lambda w: pl.BlockSpec((1, 1, s, w), lambda bi, gi, i: (bi, gi, 0, 0))
    vt_spec = pl.BlockSpec((1, 1, V_ROWS, s), lambda bi, gi, i: (bi, gi, 0, 0))
    return pl.pallas_call(
        functools.partial(_nsa_kernel, tq=tq),
        grid=(b, g, nq),
        in_specs=[pl.BlockSpec((tq, NSA_E * HD), lambda bi, gi, i: (bi * nq + i, gi)),
                  pl.BlockSpec((tq, LANE), lambda bi, gi, i: (bi * nq + i, gi)),
                  pl.BlockSpec((1, 1, 1, nrow, HD), lambda bi, gi, i: (bi, 0, gi, 0, 0)),
                  pl.BlockSpec((1, 1, 1, HD, nrow), lambda bi, gi, i: (bi, 1, gi, 0, 0)),
                  kv_spec(aug_w), vt_spec, kv_spec(HD), vt_spec,
                  pl.BlockSpec((n_sel, nrow), lambda bi, gi, i: (0, 0)),
                  pl.BlockSpec((HD, 1), lambda bi, gi, i: (0, 0))],
        out_specs=pl.BlockSpec((tq, NSA_E * HD), lambda bi, gi, i: (bi * nq + i, gi)),
        out_shape=jax.ShapeDtypeStruct((b * s, NSA_HEADS * HD), BF16),
        scratch_shapes=[pltpu.VMEM((aug_w, lanes), BF16),
                        pltpu.VMEM((n_sel, tq), F32),
                        pltpu.VMEM((1, lanes), F32),
                        pltpu.VMEM((V_ROWS, lanes), F32),
                        pltpu.VMEM((HD, lanes), F32),
                        pltpu.VMEM((2, tq, lanes), F32),
                        pltpu.VMEM((3, tq, lanes), F32)],
        compiler_params=_cparams(("parallel", "parallel", "arbitrary")),
        name="nsa_attention",
    )(q, gl, cmp_kv, cmp_t, ksel, vsel, kwin, vwin, ovl, q_gain.reshape(HD, 1))


def _softplus(x):
    return jnp.maximum(x, 0.0) + jnp.log1p(jnp.exp(-jnp.abs(x)))


def _split_bf16(x, pieces):
    out = []
    for _ in range(pieces - 1):
        hi = x.astype(BF16)
        out.append(hi)
        x = x - hi.astype(F32)
    out.append(x.astype(BF16))
    return out


def _ssd_kernel(z_ref, xbc_ref, dt_ref, dtT_ref, cw_ref, cb_ref, dtb_ref, dtbT_ref, alog_ref, alogT_ref,
                dskip_ref, ng_ref, exp_ref, shift_ref, o_ref, xpad_ref, state_ref, y_ref):
    c = pl.program_id(1)
    L = xbc_ref.shape[0]
    gw = SSM_INNER // SSM_GROUPS
    e_per_g = SSM_HEADS // SSM_GROUPS

    @pl.when(c == 0)
    def _():
        xpad_ref[0:8, :] = jnp.zeros((8, SSM_CONV_DIM), F32)
        state_ref[...] = jnp.zeros(state_ref.shape, F32)

    xb = xbc_ref[...]
    xf = xb.astype(F32)
    shifted = _dot(shift_ref[...], xb)
    acc = cb_ref[...] + cw_ref[SSM_CONV - 1:SSM_CONV, :] * xf
    for k in range(SSM_CONV - 1):
        acc = acc + cw_ref[k:k + 1, :] * shifted[k * L:(k + 1) * L]
    xpad_ref[8:16, :] = xf[0:8]
    head = cb_ref[...] + cw_ref[0:1, :] * xpad_ref[5:13, :]
    for k in range(1, SSM_CONV):
        head = head + cw_ref[k:k + 1, :] * xpad_ref[5 + k:13 + k, :]
    xpad_ref[0:8, :] = xf[L - 8:L]
    xact = _silu(jnp.concatenate([head, acc[8:]], axis=0))
    xs = xact[:, :SSM_INNER]

    li = lax.broadcasted_iota(jnp.int32, (L, L), 0)
    si = lax.broadcasted_iota(jnp.int32, (L, L), 1)
    causal = li >= si
    log2e = math.log2(math.e)
    dt = _softplus(dt_ref[...] + dtb_ref[...])
    da = jnp.concatenate(_split_bf16(dt * (-log2e * jnp.exp(alog_ref[...])), 3), axis=1)
    r = _dot(jnp.where(causal, 1.0, 0.0).astype(BF16), da)
    a_cs = r[:, 0:LANE] + r[:, LANE:2 * LANE] + r[:, 2 * LANE:]
    dtT = _softplus(dtT_ref[0] + dtbT_ref[...])
    daT = jnp.concatenate(_split_bf16(dtT * (-log2e * jnp.exp(alogT_ref[...])), 3), axis=0)
    rT = _dot(daT, jnp.where(li <= si, 1.0, 0.0).astype(BF16))
    nh = SSM_HEADS
    a_csT = rT[0:nh] + rT[nh:2 * nh] + rT[2 * nh:]

    widen = lambda v: _dot(jnp.concatenate(_split_bf16(v, 2), axis=1), exp_ref[...])
    ea_x = widen(jnp.exp2(a_cs))
    din_x = widen(jnp.exp2(a_cs[L - 1:L, :] - a_cs))
    xd = xs * widen(dt)
    xdd = (xd * din_x).astype(BF16)
    xd = xd.astype(BF16)

    half = L // 2
    first_head = lax.broadcasted_iota(jnp.int32, (L, 2 * SSM_P), 1) < SSM_P
    for g in range(SSM_GROUPS):
        gs = slice(g * gw, (g + 1) * gw)
        bg = xact[:, SSM_INNER + g * SSM_N:SSM_INNER + (g + 1) * SSM_N].astype(BF16)
        cg = xact[:, SSM_INNER + (SSM_GROUPS + g) * SSM_N:SSM_INNER + (SSM_GROUPS + g + 1) * SSM_N].astype(BF16)
        cb = _dot_nt(cg, bg)
        st = state_ref[g]
        y_ref[:, gs] = _dot(cg, st.astype(BF16)) * ea_x[:, gs] + xs[:, gs] * dskip_ref[:, gs]
        for pair in range(e_per_g // 2):
            ps = slice(g * gw + pair * 2 * SSM_P, g * gw + (pair + 1) * 2 * SSM_P)
            y_top = jnp.zeros((half, 2 * SSM_P), F32)
            y_bot = jnp.zeros((half, 2 * SSM_P), F32)
            for sub in range(2):
                h = g * e_per_g + 2 * pair + sub
                xh = jnp.where(first_head if sub == 0 else ~first_head, xd[:, ps], jnp.zeros_like(xd[:, ps]))
                diff = a_cs[:, h:h + 1] - a_csT[h:h + 1, :]
                seg_t = jnp.exp2(jnp.where(causal[:half, :half], diff[:half, :half], NEG))
                seg_b = jnp.exp2(jnp.where(causal[half:, :], diff[half:, :], NEG))
                y_top = y_top + _dot((cb[:half, :half] * seg_t).astype(BF16), xh[:half])
                y_bot = y_bot + _dot((cb[half:, :] * seg_b).astype(BF16), xh)
            y_ref[:half, ps] += y_top
            y_ref[half:, ps] += y_bot
        upd = lax.dot_general(bg, xdd[:, gs], (((0,), (0,)), ((), ())), preferred_element_type=F32)
        state_ref[g] = st * ea_x[L - 1:L, gs] + upd

    y = y_ref[...] * _silu(z_ref[...].astype(F32))
    for g in range(SSM_GROUPS):
        gs = slice(g * gw, (g + 1) * gw)
        o_ref[:, gs] = _rms(y[:, gs], ng_ref[:, gs]).astype(o_ref.dtype)


def _ssd_mixer(z, xbc, dt, conv_w, conv_b, dt_bias, a_log, d_skip, norm_gain, b, s):
    L = math.gcd(SSM_CHUNK, s)
    nc = s // L
    h = SSM_HEADS
    dtT = dt[:, :h].reshape(b, s, h).transpose(0, 2, 1)
    pad_l = lambda v: jnp.zeros((1, LANE), F32).at[0, :h].set(v)
    expand = jnp.zeros((LANE, SSM_INNER), F32).at[:h].set(jnp.repeat(jnp.eye(h, dtype=F32), SSM_P, axis=1))
    expand = jnp.concatenate([expand, expand], axis=0).astype(BF16)
    t_out = jnp.arange(L)[:, None]
    t_in = jnp.arange(L)[None, :]
    shift = jnp.concatenate([(t_in == t_out - (SSM_CONV - 1 - k)) for k in range(SSM_CONV - 1)], axis=0).astype(BF16)
    row = lambda w: pl.BlockSpec((L, w), lambda bi, ci: (bi * nc + ci, 0))
    full = lambda shp: pl.BlockSpec(shp, lambda bi, ci: (0,) * len(shp))
    return pl.pallas_call(
        _ssd_kernel,
        grid=(b, nc),
        in_specs=[row(SSM_INNER), row(SSM_CONV_DIM), row(LANE),
                  pl.BlockSpec((1, h, L), lambda bi, ci: (bi, 0, ci)),
                  full((SSM_CONV, SSM_CONV_DIM)), full((1, SSM_CONV_DIM)),
                  full((1, LANE)), full((h, 1)), full((1, LANE)), full((h, 1)),
                  full((1, SSM_INNER)), full((1, SSM_INNER)), full((2 * LANE, SSM_INNER)),
                  full(((SSM_CONV - 1) * L, L))],
        out_specs=row(SSM_INNER),
        out_shape=jax.ShapeDtypeStruct((b * s, SSM_INNER), BF16),
        scratch_shapes=[pltpu.VMEM((16, SSM_CONV_DIM), F32),
                        pltpu.VMEM((SSM_GROUPS, SSM_N, SSM_INNER // SSM_GROUPS), F32),
                        pltpu.VMEM((L, SSM_INNER), F32)],
        compiler_params=_cparams(("parallel", "arbitrary")),
        name="ssd_mixer",
    )(z, xbc, dt, dtT, conv_w, conv_b.reshape(1, -1), pad_l(dt_bias), dt_bias.reshape(h, 1),
      pad_l(a_log), a_log.reshape(h, 1), jnp.repeat(d_skip, SSM_P).reshape(1, -1),
      norm_gain.reshape(1, -1), expand, shift)


ROUTE_ROWS = 8 + MOE_EXPERTS


def _merge_kernel(x_ref, ya_ref, yb_ref, brg_ref, wa_ref, wb_ref, wo_ref, mg_ref, wr_ref, br_ref,
                  x1_ref, h2_ref, eid_ref, ew_ref):
    tm = x_ref.shape[0]
    gate = jax.nn.sigmoid(brg_ref[...].astype(F32))
    merged = gate[:, :D_MODEL] * _dot(ya_ref[...], wa_ref[...]) + gate[:, D_MODEL:] * _dot(yb_ref[...], wb_ref[...])
    x1 = x_ref[...] + _dot(merged.astype(BF16), wo_ref[...])
    x1_ref[...] = x1
    h2 = _rms(x1, mg_ref[...])
    _rows_to_tiles(h2, h2_ref)

    lg = _dot_nt(wr_ref[...], h2.astype(BF16)) + br_ref[...]
    r = lax.broadcasted_iota(jnp.int32, (ROUTE_ROWS, tm), 0)
    big = ROUTE_ROWS
    is_g = r < MOE_GROUPS
    gmax = jnp.max(jnp.where(is_g, lg, -jnp.inf), axis=0, keepdims=True)
    gsum = jnp.sum(jnp.where(is_g, jnp.exp(lg - gmax), 0.0), axis=0, keepdims=True)
    grp_w = 1.0 / gsum
    grp_i = jnp.min(jnp.where(is_g & (lg == gmax), r, big), axis=0, keepdims=True)
    is_e = (r >= 8) & (((r - 8) // MOE_EPG) == grp_i)
    le = jnp.where(is_e, lg, -jnp.inf)
    emax = jnp.max(le, axis=0, keepdims=True)
    esum = jnp.sum(jnp.where(is_e, jnp.exp(lg - emax), 0.0), axis=0, keepdims=True)
    i1 = jnp.min(jnp.where(le == emax, r, big), axis=0, keepdims=True)
    le2 = jnp.where(r == i1, -jnp.inf, le)
    e2max = jnp.max(le2, axis=0, keepdims=True)
    i2 = jnp.min(jnp.where((le2 == e2max) & is_e & (r != i1), r, big), axis=0, keepdims=True)
    p1 = 1.0 / esum
    p2 = jnp.exp(e2max - emax) / esum
    w1 = grp_w * p1 / (p1 + p2)
    w2 = grp_w * p2 / (p1 + p2)
    r8 = lax.broadcasted_iota(jnp.int32, (8, tm), 0)
    eid_ref[...] = jnp.where(r8 == 0, i1 - 8, jnp.where(r8 == 1, i2 - 8, 0))
    ew_ref[...] = jnp.where(r8 == 0, w1, jnp.where(r8 == 1, w2, 0.0))


def _merge_route(x2, y_a, y_b, brg, w_a, w_b, w_o, moe_gain, w_group, b_group, w_expert, b_expert, tm=512):
    n, d = x2.shape
    tm = min(tm, n)
    wr = jnp.zeros((ROUTE_ROWS, d), F32).at[:MOE_GROUPS].set(w_group.T).at[8:].set(w_expert.T).astype(BF16)
    br = jnp.zeros((ROUTE_ROWS, 1), F32).at[:MOE_GROUPS, 0].set(b_group).at[8:, 0].set(b_expert)
    row = lambda w: pl.BlockSpec((tm, w), lambda i: (i, 0))
    colblk = pl.BlockSpec((8, tm), lambda i: (0, i))
    return pl.pallas_call(
        _merge_kernel,
        grid=(n // tm,),
        in_specs=[row(d), row(Q_W), row(SSM_INNER), row(BRG_W),
                  _const_spec((Q_W, d)), _const_spec((SSM_INNER, d)), _const_spec((d, d)),
                  _const_spec((1, d)), _const_spec((ROUTE_ROWS, d)), _const_spec((ROUTE_ROWS, 1))],
        out_specs=(row(d), pl.BlockSpec((tm, ROW_SUB, LANE), lambda i: (i, 0, 0)), colblk, colblk),
        out_shape=(jax.ShapeDtypeStruct((n, d), F32), jax.ShapeDtypeStruct((n, ROW_SUB, LANE), ROW_DTYPE),
                   jax.ShapeDtypeStruct((8, n), jnp.int32), jax.ShapeDtypeStruct((8, n), F32)),
        compiler_params=_cparams(("parallel",)),
        name="merge_route",
    )(x2, y_a, y_b, brg, w_a.astype(BF16), w_b.astype(BF16), w_o.astype(BF16), moe_gain.reshape(1, d), wr, br)


def _rank_kernel(eid_ref, rank_ref, cnt_ref, run_ref, tri_ref):
    i = pl.program_id(0)
    tm = eid_ref.shape[1]

    @pl.when(i == 0)
    def _():
        run_ref[...] = jnp.zeros(run_ref.shape, F32)
        earlier = lax.broadcasted_iota(jnp.int32, (tm, tm), 0) < lax.broadcasted_iota(jnp.int32, (tm, tm), 1)
        tri_ref[...] = jnp.where(earlier, 1.0, 0.0).astype(BF16)

    r = lax.broadcasted_iota(jnp.int32, (MOE_EXPERTS, tm), 0)
    oh0 = r == eid_ref[0:1, :]
    oh1 = r == eid_ref[1:2, :]
    oh = jnp.where(oh0 | oh1, 1.0, 0.0)
    prefix = _dot(oh.astype(BF16), tri_ref[...]) + run_ref[:, 0:1]
    rank0 = jnp.sum(jnp.where(oh0, prefix, 0.0), axis=0, keepdims=True)
    rank1 = jnp.sum(jnp.where(oh1, prefix, 0.0), axis=0, keepdims=True)
    r8 = lax.broadcasted_iota(jnp.int32, (8, tm), 0)
    rank_ref[...] = jnp.where(r8 == 0, rank0, jnp.where(r8 == 1, rank1, 0.0)).astype(jnp.int32)
    run_ref[...] = run_ref[...] + jnp.sum(oh, axis=1, keepdims=True)
    cnt_ref[...] = run_ref[...]


def _moe_rank(eid, tm=512):
    n = eid.shape[1]
    tm = min(tm, n)
    return pl.pallas_call(
        _rank_kernel,
        grid=(n // tm,),
        in_specs=[pl.BlockSpec((8, tm), lambda i: (0, i))],
        out_specs=(pl.BlockSpec((8, tm), lambda i: (0, i)),
                   pl.BlockSpec((MOE_EXPERTS, LANE), lambda i: (0, 0))),
        out_shape=(jax.ShapeDtypeStruct((8, n), jnp.int32),
                   jax.ShapeDtypeStruct((MOE_EXPERTS, LANE), F32)),
        scratch_shapes=[pltpu.VMEM((MOE_EXPERTS, LANE), F32), pltpu.VMEM((tm, tm), BF16)],
        compiler_params=_cparams(("arbitrary",)),
        name="moe_rank",
    )(eid)


def _row_copy(src, dst, sem):
    return pltpu.make_async_copy(src, dst, sem)


ROW_SUB = D_MODEL // LANE
ROW_DTYPE = BF16
ROW_UNROLL = 8


def _rows_to_tiles(x, ref):
    ref[...] = x.astype(ref.dtype).reshape(ref.shape)


def _tiles_to_rows(ref):
    x = ref[...]
    return x.reshape(x.shape[0], ROW_SUB * LANE)


def _dispatch_kernel(slot_ref, h_ref, xs_in_ref, xs_ref, stage_ref, sems):
    del xs_in_ref
    i = pl.program_id(0)
    tm = h_ref.shape[0]
    buf = i % 2
    stage_ref[buf] = h_ref[...]

    def issue(t, carry):
        for u in range(ROW_UNROLL):
            n = t * ROW_UNROLL + u
            for k in range(MOE_TOP_K):
                _row_copy(stage_ref.at[buf, n], xs_ref.at[slot_ref[k, n]], sems.at[buf]).start(priority=k)
        return carry

    lax.fori_loop(0, tm // ROW_UNROLL, issue, 0)

    def drain(b):
        def body(t, carry):
            for _ in range(ROW_UNROLL * MOE_TOP_K):
                _row_copy(stage_ref.at[b, 0], xs_ref.at[0], sems.at[b]).wait()
            return carry

        lax.fori_loop(0, tm // ROW_UNROLL, body, 0)

    @pl.when(i > 0)
    def _():
        drain(1 - buf)

    @pl.when(i == pl.num_programs(0) - 1)
    def _():
        drain(buf)


def _moe_dispatch(slot, h2, n_slots, tm=256):
    n = h2.shape[0]
    tm = min(tm, n)
    xs0 = jnp.zeros((n_slots, ROW_SUB, LANE), h2.dtype)
    return pl.pallas_call(
        _dispatch_kernel,
        grid=(n // tm,),
        in_specs=[pl.BlockSpec((MOE_TOP_K, tm), lambda i: (0, i), memory_space=pltpu.SMEM),
                  pl.BlockSpec((tm, ROW_SUB, LANE), lambda i: (i, 0, 0)),
                  pl.BlockSpec(memory_space=pl.ANY)],
        out_specs=pl.BlockSpec(memory_space=pl.ANY),
        out_shape=jax.ShapeDtypeStruct(xs0.shape, h2.dtype),
        scratch_shapes=[pltpu.VMEM((2, tm, ROW_SUB, LANE), h2.dtype), pltpu.SemaphoreType.DMA((2,))],
        input_output_aliases={2: 0},
        compiler_params=_cparams(("arbitrary",)),
        name="moe_dispatch",
    )(slot, h2, xs0)


def _expert_kernel(be_ref, nu_ref, xs_ref, wg_ref, wu_ref, wd_ref, o_ref):
    j = pl.program_id(0)

    @pl.when(j < nu_ref[0])
    def _():
        x = _tiles_to_rows(xs_ref).astype(BF16)
        act = _silu(_dot(x, wg_ref[0])) * _dot(x, wu_ref[0])
        _rows_to_tiles(_dot(act.astype(BF16), wd_ref[0]), o_ref)

    @pl.when(j >= nu_ref[0])
    def _():
        o_ref[...] = jnp.zeros(o_ref.shape, o_ref.dtype)


def _moe_experts(block_e, n_used, xs, w_gate, w_up, w_down):
    n_slots = xs.shape[0]
    d = D_MODEL
    rb = MOE_ROW_BLOCK
    wspec = lambda shp: pl.BlockSpec((1,) + shp, lambda j, be, nu: (be[j], 0, 0))
    rows = pl.BlockSpec((rb, ROW_SUB, LANE), lambda j, be, nu: (j, 0, 0))
    return pl.pallas_call(
        _expert_kernel,
        grid_spec=pltpu.PrefetchScalarGridSpec(
            num_scalar_prefetch=2,
            grid=(n_slots // rb,),
            in_specs=[rows, wspec((d, MOE_HIDDEN)), wspec((d, MOE_HIDDEN)), wspec((MOE_HIDDEN, d))],
            out_specs=rows),
        out_shape=jax.ShapeDtypeStruct(xs.shape, ROW_DTYPE),
        compiler_params=_cparams(("arbitrary",)),
        name="moe_experts",
    )(block_e, n_used, xs, w_gate.astype(BF16), w_up.astype(BF16), w_down.astype(BF16))


def _combine_kernel(slot_ref, slot_next_ref, ew_ref, x1_ref, p_ref, eo_ref, pg_ref, wg_ref, wp_ref, o_ref,
                    rows_ref, sems):
    i = pl.program_id(0)
    n_steps = pl.num_programs(0)
    tm = x1_ref.shape[0]

    def gather(idx_ref, buf):
        def issue(t, carry):
            for u in range(ROW_UNROLL):
                n = t * ROW_UNROLL + u
                for k in range(MOE_TOP_K):
                    _row_copy(eo_ref.at[idx_ref[k, n]], rows_ref.at[buf, k, n], sems.at[buf]).start(priority=k)
            return carry

        lax.fori_loop(0, tm // ROW_UNROLL, issue, 0)

    @pl.when(i == 0)
    def _():
        gather(slot_ref, 0)

    @pl.when(i + 1 < n_steps)
    def _():
        gather(slot_next_ref, (i + 1) % 2)

    buf = i % 2

    def drain(t, carry):
        for _ in range(ROW_UNROLL * MOE_TOP_K):
            _row_copy(eo_ref.at[0], rows_ref.at[buf, 0, 0], sems.at[buf]).wait()
        return carry

    lax.fori_loop(0, tm // ROW_UNROLL, drain, 0)

    rows = lambda k: _tiles_to_rows(rows_ref.at[buf, k]).astype(F32)
    x2 = x1_ref[...] + ew_ref[:, 0:1] * rows(0) + ew_ref[:, 1:2] * rows(1)
    h3 = _rms(x2, pg_ref[...]).astype(BF16)
    o_ref[...] = x2 + jax.nn.sigmoid(_dot(h3, wg_ref[...])) * _dot(p_ref[...].astype(BF16), wp_ref[...])


def _moe_combine_ple(slot, ew_cols, x1, p2, expert_out, ple_gain, w_gate, w_proj, tm=256):
    n, d = x1.shape
    tm = min(tm, n)
    row = lambda w: pl.BlockSpec((tm, w), lambda i: (i, 0))
    last = n // tm - 1
    idx = lambda fn: pl.BlockSpec((MOE_TOP_K, tm), fn, memory_space=pltpu.SMEM)
    return pl.pallas_call(
        _combine_kernel,
        grid=(n // tm,),
        in_specs=[idx(lambda i: (0, i)), idx(lambda i: (0, jnp.minimum(i + 1, last))),
                  row(8), row(d), row(PLE_DIM),
                  pl.BlockSpec(memory_space=pl.ANY),
                  _const_spec((1, d)), _const_spec((d, d)), _const_spec((PLE_DIM, d))],
        out_specs=row(d),
        out_shape=jax.ShapeDtypeStruct((n, d), F32),
        scratch_shapes=[pltpu.VMEM((2, MOE_TOP_K, tm, ROW_SUB, LANE), ROW_DTYPE),
                        pltpu.SemaphoreType.DMA((2,))],
        compiler_params=_cparams(("arbitrary",)),
        name="moe_combine_ple",
    )(slot, slot, ew_cols, x1, p2, expert_out, ple_gain.reshape(1, d), w_gate.astype(BF16), w_proj.astype(BF16))


def _layer(x2, p2, b, s, mix_norm, w_in, nsa_q_gain, nsa_kc_gain, nsa_ks_gain, nsa_kw_gain,
           cmp_pos_k, cmp_w1_k, cmp_w2_k, cmp_pos_v, cmp_w1_v, cmp_w2_v,
           ssm_conv_w, ssm_conv_b, ssm_dt_bias, ssm_a_log, ssm_d, ssm_norm,
           w_branch_a, w_branch_b, w_out,
           moe_norm, moe_w_group, moe_b_group, moe_w_expert, moe_b_expert,
           moe_w_gate, moe_w_up, moe_w_down, ple_norm, ple_w_gate, ple_w_proj):
    n = b * s
    q, kvc, kvsw, gl, z, xbc, dt, brg = _in_proj(x2, mix_norm, _pack_w_in(w_in))

    cmp_kv, cmp_t = _compress(kvc, jnp.stack([cmp_pos_k, cmp_pos_v]), jnp.stack([cmp_w1_k, cmp_w1_v]),
                              jnp.stack([cmp_w2_k, cmp_w2_v]), nsa_kc_gain, b, s)
    ksel, vsel, kwin, vwin = _kv_prep(kvsw, nsa_ks_gain, nsa_kw_gain, b, s)
    y_a = _nsa_attention(q, gl, cmp_kv, cmp_t, ksel, vsel, kwin, vwin, nsa_q_gain, b, s)
    y_b = _ssd_mixer(z, xbc, dt, ssm_conv_w, ssm_conv_b, ssm_dt_bias, ssm_a_log, ssm_d, ssm_norm, b, s)

    x1, h2, eid, ew = _merge_route(x2, y_a, y_b, brg, w_branch_a, w_branch_b, w_out, moe_norm,
                                   moe_w_group, moe_b_group, moe_w_expert, moe_b_expert)

    rank, cnt = _moe_rank(eid)
    rb = MOE_ROW_BLOCK
    counts = cnt[:, 0].astype(jnp.int32)
    padded = (counts + rb - 1) // rb * rb
    pend = jnp.cumsum(padded)
    pstart = pend - padded
    e_ids = jnp.arange(MOE_EXPERTS, dtype=jnp.int32)
    is_e = eid[None, :MOE_TOP_K] == e_ids[:, None, None]
    slot = rank[:MOE_TOP_K] + jnp.sum(jnp.where(is_e, pstart[:, None, None], 0), axis=0)
    n_blocks = -(-(n * MOE_TOP_K) // rb) + MOE_EXPERTS
    first_row = jnp.arange(n_blocks, dtype=jnp.int32) * rb
    block_e = jnp.minimum(jnp.sum(pend[None, :] <= first_row[:, None], axis=1), MOE_EXPERTS - 1)
    n_used = (pend[-1:] // rb).astype(jnp.int32)

    xs = _moe_dispatch(slot, h2, n_blocks * rb)
    expert_out = _moe_experts(block_e.astype(jnp.int32), n_used, xs, moe_w_gate, moe_w_up, moe_w_down)
    ew_cols = ew.T
    return _moe_combine_ple(slot, ew_cols, x1, p2, expert_out, ple_norm, ple_w_gate, ple_w_proj)


def kernel(x, p, mix_norm, w_in, nsa_q_gain, nsa_kc_gain, nsa_ks_gain, nsa_kw_gain, cmp_pos_k, cmp_w1_k, cmp_w2_k, cmp_pos_v, cmp_w1_v, cmp_w2_v, ssm_conv_w, ssm_conv_b, ssm_dt_bias, ssm_a_log, ssm_d, ssm_norm, w_branch_a, w_branch_b, w_out, moe_norm, moe_w_group, moe_b_group, moe_w_expert, moe_b_expert, moe_w_gate, moe_w_up, moe_w_down, ple_norm, ple_w_gate, ple_w_proj):
    b, s, d = x.shape
    params = (mix_norm, w_in, nsa_q_gain, nsa_kc_gain, nsa_ks_gain, nsa_kw_gain,
              cmp_pos_k, cmp_w1_k, cmp_w2_k, cmp_pos_v, cmp_w1_v, cmp_w2_v,
              ssm_conv_w, ssm_conv_b, ssm_dt_bias, ssm_a_log, ssm_d, ssm_norm,
              w_branch_a, w_branch_b, w_out,
              moe_norm, moe_w_group, moe_b_group, moe_w_expert, moe_b_expert,
              moe_w_gate, moe_w_up, moe_w_down, ple_norm, ple_w_gate, ple_w_proj)
    x2 = x.reshape(b * s, d)
    for i in range(p.shape[0]):
        x2 = _layer(x2, p[i].reshape(b * s, -1), b, s, *(w[i] for w in params))
    return x2.reshape(b, s, d)
```

```python
import functools
import math

import jax
import jax.numpy as jnp
from jax import lax
from jax.experimental import pallas as pl
from jax.experimental.pallas import tpu as pltpu

F32 = jnp.float32
BF16 = jnp.bfloat16

D_MODEL = 1024
RMS_EPS = 1e-6
NEG = -1e30

NSA_HEADS = 8
NSA_GROUPS = 2
NSA_E = NSA_HEADS // NSA_GROUPS
HD = 64
CMP_BLOCK = 32
CMP_STRIDE = 16
CMP_HIDDEN = 128
SEL_BLOCK = 64
TOP_N = 16
WINDOW = 512
FORCE_BONUS = 1e3

SSM_HEADS = 16
SSM_P = 64
SSM_INNER = SSM_HEADS * SSM_P
SSM_GROUPS = 2
SSM_N = 128
SSM_CONV = 4
SSM_CHUNK = 256
SSM_CONV_DIM = SSM_INNER + 2 * SSM_GROUPS * SSM_N

MOE_GROUPS = 4
MOE_EPG = 8
MOE_EXPERTS = MOE_GROUPS * MOE_EPG
MOE_TOP_K = 2
MOE_HIDDEN = 256
MOE_ROW_BLOCK = 256
PLE_DIM = 256

LANE = 128
VMEM_LIMIT = 52 * 1024 * 1024

Q_W = NSA_HEADS * HD
KVC_W = 2 * NSA_GROUPS * HD
KVSW_W = 4 * NSA_GROUPS * HD
GL_W = NSA_GROUPS * LANE
DT_W = LANE
BRG_W = 2 * D_MODEL
SEG_WIDTHS = (Q_W, KVC_W, KVSW_W, GL_W, SSM_INNER, SSM_CONV_DIM, DT_W, BRG_W)
SEG_DTYPES = (BF16, BF16, BF16, F32, BF16, BF16, F32, BF16)
PACKED_W = sum(SEG_WIDTHS)


def _cparams(sem):
    return pltpu.CompilerParams(dimension_semantics=sem, vmem_limit_bytes=VMEM_LIMIT)


def _const_spec(shape):
    n = len(shape)
    return pl.BlockSpec(shape, lambda *_: (0,) * n, pipeline_mode=pl.Buffered(1))


def _rms(xf, gain):
    return xf * lax.rsqrt(jnp.mean(xf * xf, axis=-1, keepdims=True) + RMS_EPS) * gain


def _silu(x):
    return x * jax.nn.sigmoid(x)


def _dot(a, b):
    return jnp.dot(a, b, preferred_element_type=F32)


def _dot_nt(a, b):
    return lax.dot_general(a, b, (((1,), (1,)), ((), ())), preferred_element_type=F32)


KVC_SEG = 1
CHUNK_W = CMP_STRIDE * NSA_GROUPS * HD


def _inproj_kernel(x_ref, g_ref, w_ref, *refs):
    outs, h_scr = refs[:-1], refs[-1]
    tm = x_ref.shape[0]
    h_scr[...] = _rms(x_ref[...], g_ref[...]).astype(BF16)
    off = 0
    for seg, (o_ref, width) in enumerate(zip(outs, SEG_WIDTHS)):
        if seg == KVC_SEG:
            val = _dot(h_scr[...], w_ref[:, off:off + width])
            for j in range(2):
                part = val[:, j * LANE:(j + 1) * LANE].reshape(tm // CMP_STRIDE, CMP_STRIDE, LANE)
                o_ref[:, j * CHUNK_W:(j + 1) * CHUNK_W] = part.reshape(tm // CMP_STRIDE, CHUNK_W).astype(o_ref.dtype)
        else:
            for lo in range(0, width, 512):
                hi = min(lo + 512, width)
                o_ref[:, lo:hi] = _dot(h_scr[...], w_ref[:, off + lo:off + hi]).astype(o_ref.dtype)
        off += width


def _pack_w_in(w_in):
    sizes = (Q_W, 128, 128, 128, 128, 128, 128, NSA_HEADS * 3, SSM_INNER, SSM_CONV_DIM, SSM_HEADS, BRG_W)
    offs = [0]
    for s in sizes:
        offs.append(offs[-1] + s)
    seg = lambda i: w_in[:, offs[i]:offs[i + 1]]
    q, kc, vc, ks, vs, kw, vw, ng, z, xbc, dt, brg = (seg(i) for i in range(12))
    d = w_in.shape[0]
    pad = lambda a, w: jnp.concatenate([a, jnp.zeros((d, w - a.shape[1]), a.dtype)], axis=1)
    per_g = NSA_E * 3
    gl = jnp.concatenate([pad(ng[:, g * per_g:(g + 1) * per_g], LANE) for g in range(NSA_GROUPS)], axis=1)
    packed = jnp.concatenate([q, kc, vc, ks, vs, kw, vw, gl, z, xbc, pad(dt, DT_W), brg], axis=1)
    return packed.astype(BF16)


def _in_proj(x2, gain, w_packed, tm=512):
    n, d = x2.shape
    tm = min(tm, n)
    shapes = [(n, w) for w in SEG_WIDTHS]
    blocks = [(tm, w) for w in SEG_WIDTHS]
    shapes[KVC_SEG] = (n // CMP_STRIDE, 2 * CHUNK_W)
    blocks[KVC_SEG] = (tm // CMP_STRIDE, 2 * CHUNK_W)
    return pl.pallas_call(
        _inproj_kernel,
        grid=(n // tm,),
        in_specs=[pl.BlockSpec((tm, d), lambda i: (i, 0)),
                  _const_spec((1, d)),
                  _const_spec((d, PACKED_W))],
        out_specs=tuple(pl.BlockSpec(blk, lambda i: (i, 0)) for blk in blocks),
        out_shape=tuple(jax.ShapeDtypeStruct(shp, dt) for shp, dt in zip(shapes, SEG_DTYPES)),
        scratch_shapes=[pltpu.VMEM((tm, d), BF16)],
        compiler_params=_cparams(("parallel",)),
        name="in_proj",
    )(x2, gain.reshape(1, d), w_packed)


def _compress_kernel(x_ref, w1x_ref, w1_ref, w2_ref, w2t_ref, pos_ref, gain_ref, o_ref, ot_ref):
    kv = pl.program_id(1)
    x = x_ref[...]
    h1 = _dot(x, w1x_ref[0, 0, 0])
    h2 = _dot(x, w1x_ref[0, 0, 1])
    nrow = x.shape[0]
    h2 = pltpu.roll(h2, nrow - 1, 0)
    bias = _dot(pos_ref[0], w1_ref[0])[0:1]
    hid = _silu(h1 + h2 + bias)
    out = _dot(hid.astype(BF16), w2_ref[0])
    normed = _rms(out, gain_ref[...])
    o_ref[0, 0, 0] = jnp.where(kv == 0, normed, out).astype(o_ref.dtype)
    ot_ref[0, 0, 0] = _dot_nt(w2t_ref[0], hid.astype(BF16)).astype(ot_ref.dtype)


def _compress(kvc, pos, w1, w2, kc_gain, b, s):
    nrow = s // CMP_STRIDE
    cl = CMP_BLOCK * HD
    pos_flat = jnp.zeros((2, 8, cl), BF16).at[:, 0, :].set(pos.reshape(2, cl).astype(BF16))
    w1r = w1.reshape(2, 2, CMP_STRIDE, 1, HD, CMP_HIDDEN)
    own = jnp.eye(NSA_GROUPS, dtype=w1.dtype)
    w1x = w1r[:, None] * own[None, :, None, None, :, None, None]
    w1x = w1x.reshape(2, NSA_GROUPS, 2, CHUNK_W, CMP_HIDDEN).astype(BF16)
    return pl.pallas_call(
        _compress_kernel,
        grid=(b, 2, NSA_GROUPS),
        in_specs=[pl.BlockSpec((nrow, CHUNK_W), lambda i, k, g: (i, k)),
                  pl.BlockSpec((1, 1, 2, CHUNK_W, CMP_HIDDEN), lambda i, k, g: (k, g, 0, 0, 0)),
                  pl.BlockSpec((1, cl, CMP_HIDDEN), lambda i, k, g: (k, 0, 0)),
                  pl.BlockSpec((1, CMP_HIDDEN, HD), lambda i, k, g: (k, 0, 0)),
                  pl.BlockSpec((1, HD, CMP_HIDDEN), lambda i, k, g: (k, 0, 0)),
                  pl.BlockSpec((1, 8, cl), lambda i, k, g: (k, 0, 0)),
                  pl.BlockSpec((1, HD), lambda i, k, g: (0, 0))],
        out_specs=(pl.BlockSpec((1, 1, 1, nrow, HD), lambda i, k, g: (i, k, g, 0, 0)),
                   pl.BlockSpec((1, 1, 1, HD, nrow), lambda i, k, g: (i, k, g, 0, 0))),
        out_shape=(jax.ShapeDtypeStruct((b, 2, NSA_GROUPS, nrow, HD), BF16),
                   jax.ShapeDtypeStruct((b, 2, NSA_GROUPS, HD, nrow), BF16)),
        compiler_params=_cparams(("parallel", "parallel", "parallel")),
        name="nsa_compress",
    )(kvc, w1x, w1.astype(BF16), w2.astype(BF16), w2.transpose(0, 2, 1).astype(BF16), pos_flat,
      kc_gain.reshape(1, HD))


V_ROWS = HD + 16


def _kvprep_kernel(x_ref, gs_ref, gw_ref, ksel_ref, vsel_ref, kwin_ref, vwin_ref):
    ts = x_ref.shape[0]
    base = pl.program_id(1) * ts
    x = x_ref[...].astype(F32)
    n_hot = ksel_ref.shape[-1] - HD
    blk = (base + lax.broadcasted_iota(jnp.int32, (ts, n_hot), 0)) // SEL_BLOCK
    onehot = jnp.where(blk == lax.broadcasted_iota(jnp.int32, (ts, n_hot), 1), 1.0, 0.0)
    ones_row = jnp.where(lax.broadcasted_iota(jnp.int32, (V_ROWS - HD, ts), 0) == 0, 1.0, 0.0)
    vs_t = x[:, LANE:2 * LANE].T
    vw_t = x[:, 3 * LANE:4 * LANE].T
    for g in range(NSA_GROUPS):
        col = lambda j: x[:, j * LANE + g * HD:j * LANE + (g + 1) * HD]
        ksel_ref[0, g] = jnp.concatenate([_rms(col(0), gs_ref[...]), onehot], axis=1).astype(BF16)
        kwin_ref[0, g] = _rms(col(2), gw_ref[...]).astype(BF16)
        vsel_ref[0, g] = jnp.concatenate([vs_t[g * HD:(g + 1) * HD], ones_row], axis=0).astype(BF16)
        vwin_ref[0, g] = jnp.concatenate([vw_t[g * HD:(g + 1) * HD], ones_row], axis=0).astype(BF16)


def _kv_prep(kvsw, ks_gain, kw_gain, b, s, ts=1024):
    ts = min(ts, s)
    g = NSA_GROUPS
    aug_w = max(LANE, HD + s // SEL_BLOCK)
    spec = lambda w: pl.BlockSpec((1, g, ts, w), lambda i, j: (i, 0, j, 0))
    spec_t = pl.BlockSpec((1, g, V_ROWS, ts), lambda i, j: (i, 0, 0, j))
    shape_t = jax.ShapeDtypeStruct((b, g, V_ROWS, s), BF16)
    return pl.pallas_call(
        _kvprep_kernel,
        grid=(b, s // ts),
        in_specs=[pl.BlockSpec((ts, KVSW_W), lambda i, j: (i * (s // ts) + j, 0)),
                  pl.BlockSpec((1, HD), lambda i, j: (0, 0)),
                  pl.BlockSpec((1, HD), lambda i, j: (0, 0))],
        out_specs=(spec(aug_w), spec_t, spec(HD), spec_t),
        out_shape=(jax.ShapeDtypeStruct((b, g, s, aug_w), BF16), shape_t,
                   jax.ShapeDtypeStruct((b, g, s, HD), BF16), shape_t),
        compiler_params=_cparams(("parallel", "parallel")),
        name="nsa_kv_prep",
    )(kvsw, ks_gain.reshape(1, HD), kw_gain.reshape(1, HD))


def _softmax_pv_step(s, v_t, m_ref, acc_ref):
    m_old = m_ref[...]
    m_new = jnp.maximum(m_old, _col_max(s))
    alpha = jnp.exp2(m_old - m_new)
    p = jnp.exp2((s - m_new).astype(BF16))
    acc_ref[...] = alpha * acc_ref[...] + _dot(v_t, p)
    m_ref[...] = m_new


def _col_max(s):
    rows, lanes = s.shape
    return jnp.max(jnp.max(s.reshape(4, rows // 4, lanes), axis=0), axis=0, keepdims=True)


def _nsa_kernel(q_ref, gl_ref, kc_ref, vct_ref, ksel_ref, vsel_ref, kwin_ref, vwin_ref, ovl_ref, qg_ref,
                o_ref, qt_ref, imp_ref, m_ref, acc_ref, oc_ref, s_ref, w_ref, *, tq):
    i = pl.program_id(2)
    lanes = NSA_E * tq
    n_sel = ovl_ref.shape[0]
    scale = HD ** -0.5 * math.log2(math.e)

    q_t = q_ref[...].astype(F32).T
    for e in range(NSA_E):
        qe = q_t[e * HD:(e + 1) * HD, :]
        inv = lax.rsqrt(jnp.mean(qe * qe, axis=0, keepdims=True) + RMS_EPS)
        qt_ref[0:HD, e * tq:(e + 1) * tq] = (qe * inv * (qg_ref[...] * scale)).astype(BF16)
    q64 = qt_ref[0:HD, :]

    r_in = lax.broadcasted_iota(jnp.int32, (1, lanes), 1) & (tq - 1)
    t_lane = i * tq + r_in
    key_in = lax.broadcasted_iota(jnp.int32, (tq, 1), 0)
    tile_rows = lambda t: pl.ds(pl.multiple_of(t * tq, tq), tq)

    big = 2 * tq
    win_tiles = (i, jnp.maximum(i - 1, 0), jnp.maximum(i - 2, 0))
    win_masks = (lambda: key_in <= r_in,
                 lambda: key_in > jnp.where(i >= 1, -1, big),
                 lambda: key_in > r_in + jnp.where(i >= 2, 0, big))

    def window_scores(w):
        w_ref[w] = jnp.where(win_masks[w](), _dot(kwin_ref[0, 0, tile_rows(win_tiles[w]), :], q64), NEG)

    n_cmp = kc_ref.shape[3]
    s_c = _dot(kc_ref[0, 0, 0], q64)
    c_end = lax.broadcasted_iota(jnp.int32, (n_cmp, 1), 0) * CMP_STRIDE + (CMP_BLOCK - 1)
    s_c = jnp.where(c_end <= t_lane, s_c, NEG)
    e_c = jnp.exp2(s_c - jnp.max(s_c, axis=0, keepdims=True))
    inv_c = jnp.where(t_lane >= CMP_BLOCK - 1, 1.0 / jnp.sum(e_c, axis=0, keepdims=True), 0.0)
    p_c = e_c * inv_c
    oc_ref[...] = _dot(vct_ref[0, 0, 0], p_c.astype(BF16))

    p_sum = p_c[:, 0:tq]
    for e in range(1, NSA_E):
        p_sum = p_sum + p_c[:, e * tq:(e + 1) * tq]
    p_hi = p_sum.astype(BF16)
    p_lo = (p_sum - p_hi.astype(F32)).astype(BF16)
    imp = _dot(ovl_ref[...], p_hi) + _dot(ovl_ref[...], p_lo)
    j_idx = lax.broadcasted_iota(jnp.int32, (n_sel, tq), 0)
    cur = (i * tq + lax.broadcasted_iota(jnp.int32, (n_sel, tq), 1)) // SEL_BLOCK
    forced = (j_idx == 0) | (j_idx == cur) | (j_idx == cur - 1)
    valid = j_idx <= cur
    imp = jnp.where(valid, imp + jnp.where(forced, FORCE_BONUS, 0.0), -jnp.inf)
    imp_ref[...] = imp
    n_grp = n_sel // 8
    grp = [imp[8 * r:8 * r + 8] for r in range(n_grp)]
    j_in = lax.broadcasted_iota(jnp.int32, (8, tq), 0)
    rank = [jnp.zeros((8, tq), F32) for _ in range(n_grp)]
    for j2 in range(n_sel):
        if j2 % -(-n_sel // 3) == 0:
            window_scores(j2 // -(-n_sel // 3))
        other = imp_ref[j2:j2 + 1, :]
        for r in range(n_grp):
            if 8 * r > j2:
                before = other >= grp[r]
            elif 8 * r + 7 < j2:
                before = other > grp[r]
            else:
                before = (other > grp[r]) | ((other == grp[r]) & (j_in > j2 - 8 * r))
            rank[r] = rank[r] + jnp.where(before, 1.0, 0.0)
    sel = (jnp.concatenate(rank, axis=0) < min(TOP_N, n_sel)) & valid
    sel_bias = jnp.where(sel, 0.0, NEG).astype(BF16)
    for e in range(NSA_E):
        qt_ref[HD:HD + n_sel, e * tq:(e + 1) * tq] = sel_bias
    if HD + n_sel < qt_ref.shape[0]:
        qt_ref[HD + n_sel:, :] = jnp.zeros((qt_ref.shape[0] - HD - n_sel, lanes), BF16)

    m_w = jnp.maximum(jnp.maximum(_col_max(w_ref[0]), _col_max(w_ref[1])), _col_max(w_ref[2]))
    acc_w = _dot(vwin_ref[0, 0, :, tile_rows(win_tiles[0])], jnp.exp2((w_ref[0] - m_w).astype(BF16)))
    for w in range(1, 3):
        acc_w = acc_w + _dot(vwin_ref[0, 0, :, tile_rows(win_tiles[w])], jnp.exp2((w_ref[w] - m_w).astype(BF16)))
    o_w = acc_w[0:HD, :] * (1.0 / acc_w[HD:HD + 1, :])

    gate = jax.nn.sigmoid(gl_ref[...].T)
    for e in range(NSA_E):
        cs = slice(e * tq, (e + 1) * tq)
        oc_ref[:, cs] = gate[3 * e:3 * e + 1, :] * oc_ref[:, cs] + gate[3 * e + 2:3 * e + 3, :] * o_w[:, cs]

    qk_sel = lambda t: _dot(ksel_ref[0, 0, tile_rows(t), :], qt_ref[...])
    m_ref[...] = jnp.full(m_ref.shape, NEG, F32)
    acc_ref[...] = jnp.zeros(acc_ref.shape, F32)
    s_ref[0] = jnp.where(key_in <= r_in, qk_sel(i), NEG)

    def consume(slot, elem):
        v_tile = tile_rows(jnp.where(elem == 0, i, elem - 1))
        _softmax_pv_step(s_ref[slot], vsel_ref[0, 0, :, v_tile], m_ref, acc_ref)

    def sel_pair(j, carry):
        s_ref[1] = qk_sel(2 * j)
        consume(0, 2 * j)
        s_ref[0] = qk_sel(2 * j + 1)
        consume(1, 2 * j + 1)
        return carry

    def sel_quad(q, carry):
        sel_pair(2 * q, carry)
        return sel_pair(2 * q + 1, carry)

    lax.fori_loop(0, i // 4, sel_quad, 0)
    lax.fori_loop(i // 4 * 2, i // 2, sel_pair, 0)

    @pl.when(i % 2 == 1)
    def _():
        s_ref[1] = qk_sel(i - 1)
        consume(0, i - 1)
        consume(1, i)

    @pl.when(i % 2 == 0)
    def _():
        consume(0, i)

    o_s = acc_ref[0:HD, :] * (1.0 / acc_ref[HD:HD + 1, :])

    ys = []
    for e in range(NSA_E):
        cs = slice(e * tq, (e + 1) * tq)
        ys.append(oc_ref[:, cs] + gate[3 * e + 1:3 * e + 2, :] * o_s[:, cs])
    o_ref[...] = jnp.concatenate(ys, axis=0).T.astype(o_ref.dtype)


def _nsa_attention(q, gl, cmp_kv, cmp_t, ksel, vsel, kwin, vwin, q_gain, b, s, tq=256):
    assert WINDOW == 2 * tq and s % tq == 0
    nq = s // tq
    n_sel = s // SEL_BLOCK
    nrow = s // CMP_STRIDE
    g = NSA_GROUPS
    c0 = jnp.arange(nrow) * CMP_STRIDE
    j0 = jnp.arange(n_sel) * SEL_BLOCK
    ovl = ((c0[None, :] <= j0[:, None] + SEL_BLOCK - 1) & (c0[None, :] + CMP_BLOCK - 1 >= j0[:, None]))
    ovl = ovl.astype(BF16)
    aug_w = max(2 * HD, HD + n_sel)
    assert ksel.shape[-1] == aug_w
    lanes = NSA_E * tq
    kv_spec = lambda w: pl.BlockSpec((1, 1, s, w), lambda bi, gi, i: (bi, gi, 0, 0))
    vt_spec = pl.BlockSpec((1, 1, V_ROWS, s), lambda bi, gi, i: (bi, gi, 0, 0))
    return pl.pallas_call(
        functools.partial(_nsa_kernel, tq=tq),
        grid=(b, g, nq),
        in_specs=[pl.BlockSpec((tq, NSA_E * HD), lambda bi, gi, i: (bi * nq + i, gi)),
                  pl.BlockSpec((tq, LANE), lambda bi, gi, i: (bi * nq + i, gi)),
                  pl.BlockSpec((1, 1, 1, nrow, HD), lambda bi, gi, i: (bi, 0, gi, 0, 0)),
                  pl.BlockSpec((1, 1, 1, HD, nrow), lambda bi, gi, i: (bi, 1, gi, 0, 0)),
                  kv_spec(aug_w), vt_spec, kv_spec(HD), vt_spec,
                  pl.BlockSpec((n_sel, nrow), lambda bi, gi, i: (0, 0)),
                  pl.BlockSpec((HD, 1), lambda bi, gi, i: (0, 0))],
        out_specs=pl.BlockSpec((tq, NSA_E * HD), lambda bi, gi, i: (bi * nq + i, gi)),
        out_shape=jax.ShapeDtypeStruct((b * s, NSA_HEADS * HD), BF16),
        scratch_shapes=[pltpu.VMEM((aug_w, lanes), BF16),
                        pltpu.VMEM((n_sel, tq), F32),
                        pltpu.VMEM((1, lanes), F32),
                        pltpu.VMEM((V_ROWS, lanes), F32),
                        pltpu.VMEM((HD, lanes), F32),
                        pltpu.VMEM((2, tq, lanes), F32),
                        pltpu.VMEM((3, tq, lanes), F32)],
        compiler_params=_cparams(("parallel", "parallel", "arbitrary")),
        name="nsa_attention",
    )(q, gl, cmp_kv, cmp_t, ksel, vsel, kwin, vwin, ovl, q_gain.reshape(HD, 1))


def _softplus(x):
    return jnp.maximum(x, 0.0) + jnp.log1p(jnp.exp(-jnp.abs(x)))


def _split_bf16(x, pieces):
    out = []
    for _ in range(pieces - 1):
        hi = x.astype(BF16)
        out.append(hi)
        x = x - hi.astype(F32)
    out.append(x.astype(BF16))
    return out


def _ssd_kernel(z_ref, xbc_ref, dt_ref, dtT_ref, cw_ref, cb_ref, dtb_ref, dtbT_ref, alog_ref, alogT_ref,
                dskip_ref, ng_ref, exp_ref, shift_ref, o_ref, xpad_ref, state_ref, y_ref):
    c = pl.program_id(1)
    L = xbc_ref.shape[0]
    gw = SSM_INNER // SSM_GROUPS
    e_per_g = SSM_HEADS // SSM_GROUPS

    @pl.when(c == 0)
    def _():
        xpad_ref[0:8, :] = jnp.zeros((8, SSM_CONV_DIM), F32)
        state_ref[...] = jnp.zeros(state_ref.shape, F32)

    xb = xbc_ref[...]
    xf = xb.astype(F32)
    shifted = _dot(shift_ref[...], xb)
    acc = cb_ref[...] + cw_ref[SSM_CONV - 1:SSM_CONV, :] * xf
    for k in range(SSM_CONV - 1):
        acc = acc + cw_ref[k:k + 1, :] * shifted[k * L:(k + 1) * L]
    xpad_ref[8:16, :] = xf[0:8]
    head = cb_ref[...] + cw_ref[0:1, :] * xpad_ref[5:13, :]
    for k in range(1, SSM_CONV):
        head = head + cw_ref[k:k + 1, :] * xpad_ref[5 + k:13 + k, :]
    xpad_ref[0:8, :] = xf[L - 8:L]
    xact = _silu(jnp.concatenate([head, acc[8:]], axis=0))
    xs = xact[:, :SSM_INNER]

    li = lax.broadcasted_iota(jnp.int32, (L, L), 0)
    si = lax.broadcasted_iota(jnp.int32, (L, L), 1)
    causal = li >= si
    log2e = math.log2(math.e)
    dt = _softplus(dt_ref[...] + dtb_ref[...])
    da = jnp.concatenate(_split_bf16(dt * (-log2e * jnp.exp(alog_ref[...])), 3), axis=1)
    r = _dot(jnp.where(causal, 1.0, 0.0).astype(BF16), da)
    a_cs = r[:, 0:LANE] + r[:, LANE:2 * LANE] + r[:, 2 * LANE:]
    dtT = _softplus(dtT_ref[0] + dtbT_ref[...])
    daT = jnp.concatenate(_split_bf16(dtT * (-log2e * jnp.exp(alogT_ref[...])), 3), axis=0)
    rT = _dot(daT, jnp.where(li <= si, 1.0, 0.0).astype(BF16))
    nh = SSM_HEADS
    a_csT = rT[0:nh] + rT[nh:2 * nh] + rT[2 * nh:]

    widen = lambda v: _dot(jnp.concatenate(_split_bf16(v, 2), axis=1), exp_ref[...])
    ea_x = widen(jnp.exp2(a_cs))
    din_x = widen(jnp.exp2(a_cs[L - 1:L, :] - a_cs))
    xd = xs * widen(dt)
    xdd = (xd * din_x).astype(BF16)
    xd = xd.astype(BF16)

    half = L // 2
    first_head = lax.broadcasted_iota(jnp.int32, (L, 2 * SSM_P), 1) < SSM_P
    for g in range(SSM_GROUPS):
        gs = slice(g * gw, (g + 1) * gw)
        bg = xact[:, SSM_INNER + g * SSM_N:SSM_INNER + (g + 1) * SSM_N].astype(BF16)
        cg = xact[:, SSM_INNER + (SSM_GROUPS + g) * SSM_N:SSM_INNER + (SSM_GROUPS + g + 1) * SSM_N].astype(BF16)
        cb = _dot_nt(cg, bg)
        st = state_ref[g]
        y_ref[:, gs] = _dot(cg, st.astype(BF16)) * ea_x[:, gs] + xs[:, gs] * dskip_ref[:, gs]
        for pair in range(e_per_g // 2):
            ps = slice(g * gw + pair * 2 * SSM_P, g * gw + (pair + 1) * 2 * SSM_P)
            y_top = jnp.zeros((half, 2 * SSM_P), F32)
            y_bot = jnp.zeros((half, 2 * SSM_P), F32)
            for sub in range(2):
                h = g * e_per_g + 2 * pair + sub
                xh = jnp.where(first_head if sub == 0 else ~first_head, xd[:, ps], jnp.zeros_like(xd[:, ps]))
                diff = a_cs[:, h:h + 1] - a_csT[h:h + 1, :]
                seg_t = jnp.exp2(jnp.where(causal[:half, :half], diff[:half, :half], NEG))
                seg_b = jnp.exp2(jnp.where(causal[half:, :], diff[half:, :], NEG))
                y_top = y_top + _dot((cb[:half, :half] * seg_t).astype(BF16), xh[:half])
                y_bot = y_bot + _dot((cb[half:, :] * seg_b).astype(BF16), xh)
            y_ref[:half, ps] += y_top
            y_ref[half:, ps] += y_bot
        upd = lax.dot_general(bg, xdd[:, gs], (((0,), (0,)), ((), ())), preferred_element_type=F32)
        state_ref[g] = st * ea_x[L - 1:L, gs] + upd

    y = y_ref[...] * _silu(z_ref[...].astype(F32))
    for g in range(SSM_GROUPS):
        gs = slice(g * gw, (g + 1) * gw)
        o_ref[:, gs] = _rms(y[:, gs], ng_ref[:, gs]).astype(o_ref.dtype)


def _ssd_mixer(z, xbc, dt, conv_w, conv_b, dt_bias, a_log, d_skip, norm_gain, b, s):
    L = math.gcd(SSM_CHUNK, s)
    nc = s // L
    h = SSM_HEADS
    dtT = dt[:, :h].reshape(b, s, h).transpose(0, 2, 1)
    pad_l = lambda v: jnp.zeros((1, LANE), F32).at[0, :h].set(v)
    expand = jnp.zeros((LANE, SSM_INNER), F32).at[:h].set(jnp.repeat(jnp.eye(h, dtype=F32), SSM_P, axis=1))
    expand = jnp.concatenate([expand, expand], axis=0).astype(BF16)
    t_out = jnp.arange(L)[:, None]
    t_in = jnp.arange(L)[None, :]
    shift = jnp.concatenate([(t_in == t_out - (SSM_CONV - 1 - k)) for k in range(SSM_CONV - 1)], axis=0).astype(BF16)
    row = lambda w: pl.BlockSpec((L, w), lambda bi, ci: (bi * nc + ci, 0))
    full = lambda shp: pl.BlockSpec(shp, lambda bi, ci: (0,) * len(shp))
    return pl.pallas_call(
        _ssd_kernel,
        grid=(b, nc),
        in_specs=[row(SSM_INNER), row(SSM_CONV_DIM), row(LANE),
                  pl.BlockSpec((1, h, L), lambda bi, ci: (bi, 0, ci)),
                  full((SSM_CONV, SSM_CONV_DIM)), full((1, SSM_CONV_DIM)),
                  full((1, LANE)), full((h, 1)), full((1, LANE)), full((h, 1)),
                  full((1, SSM_INNER)), full((1, SSM_INNER)), full((2 * LANE, SSM_INNER)),
                  full(((SSM_CONV - 1) * L, L))],
        out_specs=row(SSM_INNER),
        out_shape=jax.ShapeDtypeStruct((b * s, SSM_INNER), BF16),
        scratch_shapes=[pltpu.VMEM((16, SSM_CONV_DIM), F32),
                        pltpu.VMEM((SSM_GROUPS, SSM_N, SSM_INNER // SSM_GROUPS), F32),
                        pltpu.VMEM((L, SSM_INNER), F32)],
        compiler_params=_cparams(("parallel", "arbitrary")),
        name="ssd_mixer",
    )(z, xbc, dt, dtT, conv_w, conv_b.reshape(1, -1), pad_l(dt_bias), dt_bias.reshape(h, 1),
      pad_l(a_log), a_log.reshape(h, 1), jnp.repeat(d_skip, SSM_P).reshape(1, -1),
      norm_gain.reshape(1, -1), expand, shift)


ROUTE_ROWS = 8 + MOE_EXPERTS


def _merge_kernel(x_ref, ya_ref, yb_ref, brg_ref, wa_ref, wb_ref, wo_ref, mg_ref, wr_ref, br_ref,
                  x1_ref, h2_ref, eid_ref, ew_ref):
    tm = x_ref.shape[0]
    gate = jax.nn.sigmoid(brg_ref[...].astype(F32))
    merged = gate[:, :D_MODEL] * _dot(ya_ref[...], wa_ref[...]) + gate[:, D_MODEL:] * _dot(yb_ref[...], wb_ref[...])
    x1 = x_ref[...] + _dot(merged.astype(BF16), wo_ref[...])
    x1_ref[...] = x1
    h2 = _rms(x1, mg_ref[...])
    _rows_to_tiles(h2, h2_ref)

    lg = _dot_nt(wr_ref[...], h2.astype(BF16)) + br_ref[...]
    r = lax.broadcasted_iota(jnp.int32, (ROUTE_ROWS, tm), 0)
    big = ROUTE_ROWS
    is_g = r < MOE_GROUPS
    gmax = jnp.max(jnp.where(is_g, lg, -jnp.inf), axis=0, keepdims=True)
    gsum = jnp.sum(jnp.where(is_g, jnp.exp(lg - gmax), 0.0), axis=0, keepdims=True)
    grp_w = 1.0 / gsum
    grp_i = jnp.min(jnp.where(is_g & (lg == gmax), r, big), axis=0, keepdims=True)
    is_e = (r >= 8) & (((r - 8) // MOE_EPG) == grp_i)
    le = jnp.where(is_e, lg, -jnp.inf)
    emax = jnp.max(le, axis=0, keepdims=True)
    esum = jnp.sum(jnp.where(is_e, jnp.exp(lg - emax), 0.0), axis=0, keepdims=True)
    i1 = jnp.min(jnp.where(le == emax, r, big), axis=0, keepdims=True)
    le2 = jnp.where(r == i1, -jnp.inf, le)
    e2max = jnp.max(le2, axis=0, keepdims=True)
    i2 = jnp.min(jnp.where((le2 == e2max) & is_e & (r != i1), r, big), axis=0, keepdims=True)
    p1 = 1.0 / esum
    p2 = jnp.exp(e2max - emax) / esum
    w1 = grp_w * p1 / (p1 + p2)
    w2 = grp_w * p2 / (p1 + p2)
    r8 = lax.broadcasted_iota(jnp.int32, (8, tm), 0)
    eid_ref[...] = jnp.where(r8 == 0, i1 - 8, jnp.where(r8 == 1, i2 - 8, 0))
    ew_ref[...] = jnp.where(r8 == 0, w1, jnp.where(r8 == 1, w2, 0.0))


def _merge_route(x2, y_a, y_b, brg, w_a, w_b, w_o, moe_gain, w_group, b_group, w_expert, b_expert, tm=512):
    n, d = x2.shape
    tm = min(tm, n)
    wr = jnp.zeros((ROUTE_ROWS, d), F32).at[:MOE_GROUPS].set(w_group.T).at[8:].set(w_expert.T).astype(BF16)
    br = jnp.zeros((ROUTE_ROWS, 1), F32).at[:MOE_GROUPS, 0].set(b_group).at[8:, 0].set(b_expert)
    row = lambda w: pl.BlockSpec((tm, w), lambda i: (i, 0))
    colblk = pl.BlockSpec((8, tm), lambda i: (0, i))
    return pl.pallas_call(
        _merge_kernel,
        grid=(n // tm,),
        in_specs=[row(d), row(Q_W), row(SSM_INNER), row(BRG_W),
                  _const_spec((Q_W, d)), _const_spec((SSM_INNER, d)), _const_spec((d, d)),
                  _const_spec((1, d)), _const_spec((ROUTE_ROWS, d)), _const_spec((ROUTE_ROWS, 1))],
        out_specs=(row(d), pl.BlockSpec((tm, ROW_SUB, LANE), lambda i: (i, 0, 0)), colblk, colblk),
        out_shape=(jax.ShapeDtypeStruct((n, d), F32), jax.ShapeDtypeStruct((n, ROW_SUB, LANE), ROW_DTYPE),
                   jax.ShapeDtypeStruct((8, n), jnp.int32), jax.ShapeDtypeStruct((8, n), F32)),
        compiler_params=_cparams(("parallel",)),
        name="merge_route",
    )(x2, y_a, y_b, brg, w_a.astype(BF16), w_b.astype(BF16), w_o.astype(BF16), moe_gain.reshape(1, d), wr, br)


def _rank_kernel(eid_ref, rank_ref, cnt_ref, run_ref, tri_ref):
    i = pl.program_id(0)
    tm = eid_ref.shape[1]

    @pl.when(i == 0)
    def _():
        run_ref[...] = jnp.zeros(run_ref.shape, F32)
        earlier = lax.broadcasted_iota(jnp.int32, (tm, tm), 0) < lax.broadcasted_iota(jnp.int32, (tm, tm), 1)
        tri_ref[...] = jnp.where(earlier, 1.0, 0.0).astype(BF16)

    r = lax.broadcasted_iota(jnp.int32, (MOE_EXPERTS, tm), 0)
    oh0 = r == eid_ref[0:1, :]
    oh1 = r == eid_ref[1:2, :]
    oh = jnp.where(oh0 | oh1, 1.0, 0.0)
    prefix = _dot(oh.astype(BF16), tri_ref[...]) + run_ref[:, 0:1]
    rank0 = jnp.sum(jnp.where(oh0, prefix, 0.0), axis=0, keepdims=True)
    rank1 = jnp.sum(jnp.where(oh1, prefix, 0.0), axis=0, keepdims=True)
    r8 = lax.broadcasted_iota(jnp.int32, (8, tm), 0)
    rank_ref[...] = jnp.where(r8 == 0, rank0, jnp.where(r8 == 1, rank1, 0.0)).astype(jnp.int32)
    run_ref[...] = run_ref[...] + jnp.sum(oh, axis=1, keepdims=True)
    cnt_ref[...] = run_ref[...]


def _moe_rank(eid, tm=512):
    n = eid.shape[1]
    tm = min(tm, n)
    return pl.pallas_call(
        _rank_kernel,
        grid=(n // tm,),
        in_specs=[pl.BlockSpec((8, tm), lambda i: (0, i))],
        out_specs=(pl.BlockSpec((8, tm), lambda i: (0, i)),
                   pl.BlockSpec((MOE_EXPERTS, LANE), lambda i: (0, 0))),
        out_shape=(jax.ShapeDtypeStruct((8, n), jnp.int32),
                   jax.ShapeDtypeStruct((MOE_EXPERTS, LANE), F32)),
        scratch_shapes=[pltpu.VMEM((MOE_EXPERTS, LANE), F32), pltpu.VMEM((tm, tm), BF16)],
        compiler_params=_cparams(("arbitrary",)),
        name="moe_rank",
    )(eid)


def _row_copy(src, dst, sem):
    return pltpu.make_async_copy(src, dst, sem)


ROW_SUB = D_MODEL // LANE
ROW_DTYPE = BF16
ROW_UNROLL = 8


def _rows_to_tiles(x, ref):
    ref[...] = x.astype(ref.dtype).reshape(ref.shape)


def _tiles_to_rows(ref):
    x = ref[...]
    return x.reshape(x.shape[0], ROW_SUB * LANE)


def _dispatch_kernel(slot_ref, h_ref, xs_in_ref, xs_ref, stage_ref, sems):
    del xs_in_ref
    i = pl.program_id(0)
    tm = h_ref.shape[0]
    buf = i % 2
    stage_ref[buf] = h_ref[...]

    def issue(t, carry):
        for u in range(ROW_UNROLL):
            n = t * ROW_UNROLL + u
            for k in range(MOE_TOP_K):
                _row_copy(stage_ref.at[buf, n], xs_ref.at[slot_ref[k, n]], sems.at[buf]).start(priority=k)
        return carry

    lax.fori_loop(0, tm // ROW_UNROLL, issue, 0)

    def drain(b):
        def body(t, carry):
            for _ in range(ROW_UNROLL * MOE_TOP_K):
                _row_copy(stage_ref.at[b, 0], xs_ref.at[0], sems.at[b]).wait()
            return carry

        lax.fori_loop(0, tm // ROW_UNROLL, body, 0)

    @pl.when(i > 0)
    def _():
        drain(1 - buf)

    @pl.when(i == pl.num_programs(0) - 1)
    def _():
        drain(buf)


def _moe_dispatch(slot, h2, n_slots, tm=512):
    n = h2.shape[0]
    tm = min(tm, n)
    xs0 = jnp.zeros((n_slots, ROW_SUB, LANE), h2.dtype)
    return pl.pallas_call(
        _dispatch_kernel,
        grid=(n // tm,),
        in_specs=[pl.BlockSpec((MOE_TOP_K, tm), lambda i: (0, i), memory_space=pltpu.SMEM),
                  pl.BlockSpec((tm, ROW_SUB, LANE), lambda i: (i, 0, 0)),
                  pl.BlockSpec(memory_space=pl.ANY)],
        out_specs=pl.BlockSpec(memory_space=pl.ANY),
        out_shape=jax.ShapeDtypeStruct(xs0.shape, h2.dtype),
        scratch_shapes=[pltpu.VMEM((2, tm, ROW_SUB, LANE), h2.dtype), pltpu.SemaphoreType.DMA((2,))],
        input_output_aliases={2: 0},
        compiler_params=_cparams(("arbitrary",)),
        name="moe_dispatch",
    )(slot, h2, xs0)


def _expert_kernel(be_ref, nu_ref, xs_ref, wg_ref, wu_ref, wd_ref, o_ref):
    j = pl.program_id(0)

    @pl.when(j < nu_ref[0])
    def _():
        x = _tiles_to_rows(xs_ref).astype(BF16)
        act = _silu(_dot(x, wg_ref[0])) * _dot(x, wu_ref[0])
        _rows_to_tiles(_dot(act.astype(BF16), wd_ref[0]), o_ref)

    @pl.when(j >= nu_ref[0])
    def _():
        o_ref[...] = jnp.zeros(o_ref.shape, o_ref.dtype)


def _moe_experts(block_e, n_used, xs, w_gate, w_up, w_down):
    n_slots = xs.shape[0]
    d = D_MODEL
    rb = MOE_ROW_BLOCK
    wspec = lambda shp: pl.BlockSpec((1,) + shp, lambda j, be, nu: (be[j], 0, 0))
    rows = pl.BlockSpec((rb, ROW_SUB, LANE), lambda j, be, nu: (j, 0, 0))
    return pl.pallas_call(
        _expert_kernel,
        grid_spec=pltpu.PrefetchScalarGridSpec(
            num_scalar_prefetch=2,
            grid=(n_slots // rb,),
            in_specs=[rows, wspec((d, MOE_HIDDEN)), wspec((d, MOE_HIDDEN)), wspec((MOE_HIDDEN, d))],
            out_specs=rows),
        out_shape=jax.ShapeDtypeStruct(xs.shape, ROW_DTYPE),
        compiler_params=_cparams(("arbitrary",)),
        name="moe_experts",
    )(block_e, n_used, xs, w_gate.astype(BF16), w_up.astype(BF16), w_down.astype(BF16))


def _combine_kernel(slot_ref, slot_next_ref, ew_ref, x1_ref, p_ref, eo_ref, pg_ref, wg_ref, wp_ref, o_ref,
                    rows_ref, sems):
    i = pl.program_id(0)
    n_steps = pl.num_programs(0)
    tm = x1_ref.shape[0]

    def gather(idx_ref, buf):
        def issue(t, carry):
            for u in range(ROW_UNROLL):
                n = t * ROW_UNROLL + u
                for k in range(MOE_TOP_K):
                    _row_copy(eo_ref.at[idx_ref[k, n]], rows_ref.at[buf, k, n], sems.at[buf]).start(priority=k)
            return carry

        lax.fori_loop(0, tm // ROW_UNROLL, issue, 0)

    @pl.when(i == 0)
    def _():
        gather(slot_ref, 0)

    @pl.when(i + 1 < n_steps)
    def _():
        gather(slot_next_ref, (i + 1) % 2)

    buf = i % 2

    def drain(t, carry):
        for _ in range(ROW_UNROLL * MOE_TOP_K):
            _row_copy(eo_ref.at[0], rows_ref.at[buf, 0, 0], sems.at[buf]).wait()
        return carry

    lax.fori_loop(0, tm // ROW_UNROLL, drain, 0)

    rows = lambda k: _tiles_to_rows(rows_ref.at[buf, k]).astype(F32)
    x2 = x1_ref[...] + ew_ref[:, 0:1] * rows(0) + ew_ref[:, 1:2] * rows(1)
    h3 = _rms(x2, pg_ref[...]).astype(BF16)
    o_ref[...] = x2 + jax.nn.sigmoid(_dot(h3, wg_ref[...])) * _dot(p_ref[...].astype(BF16), wp_ref[...])


def _moe_combine_ple(slot, ew_cols, x1, p2, expert_out, ple_gain, w_gate, w_proj, tm=512):
    n, d = x1.shape
    tm = min(tm, n)
    row = lambda w: pl.BlockSpec((tm, w), lambda i: (i, 0))
    last = n // tm - 1
    idx = lambda fn: pl.BlockSpec((MOE_TOP_K, tm), fn, memory_space=pltpu.SMEM)
    return pl.pallas_call(
        _combine_kernel,
        grid=(n // tm,),
        in_specs=[idx(lambda i: (0, i)), idx(lambda i: (0, jnp.minimum(i + 1, last))),
                  row(8), row(d), row(PLE_DIM),
                  pl.BlockSpec(memory_space=pl.ANY),
                  _const_spec((1, d)), _const_spec((d, d)), _const_spec((PLE_DIM, d))],
        out_specs=row(d),
        out_shape=jax.ShapeDtypeStruct((n, d), F32),
        scratch_shapes=[pltpu.VMEM((2, MOE_TOP_K, tm, ROW_SUB, LANE), ROW_DTYPE),
                        pltpu.SemaphoreType.DMA((2,))],
        compiler_params=_cparams(("arbitrary",)),
        name="moe_combine_ple",
    )(slot, slot, ew_cols, x1, p2, expert_out, ple_gain.reshape(1, d), w_gate.astype(BF16), w_proj.astype(BF16))


def _layer(x2, p2, b, s, mix_norm, w_in, nsa_q_gain, nsa_kc_gain, nsa_ks_gain, nsa_kw_gain,
           cmp_pos_k, cmp_w1_k, cmp_w2_k, cmp_pos_v, cmp_w1_v, cmp_w2_v,
           ssm_conv_w, ssm_conv_b, ssm_dt_bias, ssm_a_log, ssm_d, ssm_norm,
           w_branch_a, w_branch_b, w_out,
           moe_norm, moe_w_group, moe_b_group, moe_w_expert, moe_b_expert,
           moe_w_gate, moe_w_up, moe_w_down, ple_norm, ple_w_gate, ple_w_proj):
    n = b * s
    q, kvc, kvsw, gl, z, xbc, dt, brg = _in_proj(x2, mix_norm, _pack_w_in(w_in))

    cmp_kv, cmp_t = _compress(kvc, jnp.stack([cmp_pos_k, cmp_pos_v]), jnp.stack([cmp_w1_k, cmp_w1_v]),
                              jnp.stack([cmp_w2_k, cmp_w2_v]), nsa_kc_gain, b, s)
    ksel, vsel, kwin, vwin = _kv_prep(kvsw, nsa_ks_gain, nsa_kw_gain, b, s)
    y_a = _nsa_attention(q, gl, cmp_kv, cmp_t, ksel, vsel, kwin, vwin, nsa_q_gain, b, s)
    y_b = _ssd_mixer(z, xbc, dt, ssm_conv_w, ssm_conv_b, ssm_dt_bias, ssm_a_log, ssm_d, ssm_norm, b, s)

    x1, h2, eid, ew = _merge_route(x2, y_a, y_b, brg, w_branch_a, w_branch_b, w_out, moe_norm,
                                   moe_w_group, moe_b_group, moe_w_expert, moe_b_expert)

    rank, cnt = _moe_rank(eid)
    rb = MOE_ROW_BLOCK
    counts = cnt[:, 0].astype(jnp.int32)
    padded = (counts + rb - 1) // rb * rb
    pend = jnp.cumsum(padded)
    pstart = pend - padded
    e_ids = jnp.arange(MOE_EXPERTS, dtype=jnp.int32)
    is_e = eid[None, :MOE_TOP_K] == e_ids[:, None, None]
    slot = rank[:MOE_TOP_K] + jnp.sum(jnp.where(is_e, pstart[:, None, None], 0), axis=0)
    n_blocks = -(-(n * MOE_TOP_K) // rb) + MOE_EXPERTS
    first_row = jnp.arange(n_blocks, dtype=jnp.int32) * rb
    block_e = jnp.minimum(jnp.sum(pend[None, :] <= first_row[:, None], axis=1), MOE_EXPERTS - 1)
    n_used = (pend[-1:] // rb).astype(jnp.int32)

    xs = _moe_dispatch(slot, h2, n_blocks * rb)
    expert_out = _moe_experts(block_e.astype(jnp.int32), n_used, xs, moe_w_gate, moe_w_up, moe_w_down)
    ew_cols = ew.T
    return _moe_combine_ple(slot, ew_cols, x1, p2, expert_out, ple_norm, ple_w_gate, ple_w_proj)


def kernel(x, p, mix_norm, w_in, nsa_q_gain, nsa_kc_gain, nsa_ks_gain, nsa_kw_gain, cmp_pos_k, cmp_w1_k, cmp_w2_k, cmp_pos_v, cmp_w1_v, cmp_w2_v, ssm_conv_w, ssm_conv_b, ssm_dt_bias, ssm_a_log, ssm_d, ssm_norm, w_branch_a, w_branch_b, w_out, moe_norm, moe_w_group, moe_b_group, moe_w_expert, moe_b_expert, moe_w_gate, moe_w_up, moe_w_down, ple_norm, ple_w_gate, ple_w_proj):
    b, s, d = x.shape
    params = (mix_norm, w_in, nsa_q_gain, nsa_kc_gain, nsa_ks_gain, nsa_kw_gain,
              cmp_pos_k, cmp_w1_k, cmp_w2_k, cmp_pos_v, cmp_w1_v, cmp_w2_v,
              ssm_conv_w, ssm_conv_b, ssm_dt_bias, ssm_a_log, ssm_d, ssm_norm,
              w_branch_a, w_branch_b, w_out,
              moe_norm, moe_w_group, moe_b_group, moe_w_expert, moe_b_expert,
              moe_w_gate, moe_w_up, moe_w_down, ple_norm, ple_w_gate, ple_w_proj)
    x2 = x.reshape(b * s, d)
    for i in range(p.shape[0]):
        x2 = _layer(x2, p[i].reshape(b * s, -1), b, s, *(w[i] for w in params))
    return x2.reshape(b, s, d)
```

```python
import functools
import math

import jax
import jax.numpy as jnp
from jax import lax
from jax.experimental import pallas as pl
from jax.experimental.pallas import tpu as pltpu

F32 = jnp.float32
BF16 = jnp.bfloat16

D_MODEL = 1024
RMS_EPS = 1e-6
NEG = -1e30

NSA_HEADS = 8
NSA_GROUPS = 2
NSA_E = NSA_HEADS // NSA_GROUPS
HD = 64
CMP_BLOCK = 32
CMP_STRIDE = 16
CMP_HIDDEN = 128
SEL_BLOCK = 64
TOP_N = 16
WINDOW = 512
FORCE_BONUS = 1e3

SSM_HEADS = 16
SSM_P = 64
SSM_INNER = SSM_HEADS * SSM_P
SSM_GROUPS = 2
SSM_N = 128
SSM_CONV = 4
SSM_CHUNK = 256
SSM_CONV_DIM = SSM_INNER + 2 * SSM_GROUPS * SSM_N

MOE_GROUPS = 4
MOE_EPG = 8
MOE_EXPERTS = MOE_GROUPS * MOE_EPG
MOE_TOP_K = 2
MOE_HIDDEN = 256
MOE_ROW_BLOCK = 256
PLE_DIM = 256

LANE = 128
VMEM_LIMIT = 52 * 1024 * 1024

Q_W = NSA_HEADS * HD
KVC_W = 2 * NSA_GROUPS * HD
KVSW_W = 4 * NSA_GROUPS * HD
GL_W = NSA_GROUPS * LANE
DT_W = LANE
BRG_W = 2 * D_MODEL
SEG_WIDTHS = (Q_W, KVC_W, KVSW_W, GL_W, SSM_INNER, SSM_CONV_DIM, DT_W, BRG_W)
SEG_DTYPES = (BF16, BF16, BF16, F32, BF16, BF16, F32, BF16)
PACKED_W = sum(SEG_WIDTHS)


def _cparams(sem):
    return pltpu.CompilerParams(dimension_semantics=sem, vmem_limit_bytes=VMEM_LIMIT)


def _const_spec(shape):
    n = len(shape)
    return pl.BlockSpec(shape, lambda *_: (0,) * n, pipeline_mode=pl.Buffered(1))


def _rms(xf, gain):
    return xf * lax.rsqrt(jnp.mean(xf * xf, axis=-1, keepdims=True) + RMS_EPS) * gain


def _silu(x):
    return x * jax.nn.sigmoid(x)


def _dot(a, b):
    return jnp.dot(a, b, preferred_element_type=F32)


def _dot_nt(a, b):
    return lax.dot_general(a, b, (((1,), (1,)), ((), ())), preferred_element_type=F32)


KVC_SEG = 1
CHUNK_W = CMP_STRIDE * NSA_GROUPS * HD


def _inproj_kernel(x_ref, g_ref, w_ref, *refs):
    outs, h_scr = refs[:-1], refs[-1]
    tm = x_ref.shape[0]
    h_scr[...] = _rms(x_ref[...], g_ref[...]).astype(BF16)
    off = 0
    for seg, (o_ref, width) in enumerate(zip(outs, SEG_WIDTHS)):
        if seg == KVC_SEG:
            val = _dot(h_scr[...], w_ref[:, off:off + width])
            for j in range(2):
                part = val[:, j * LANE:(j + 1) * LANE].reshape(tm // CMP_STRIDE, CMP_STRIDE, LANE)
                o_ref[:, j * CHUNK_W:(j + 1) * CHUNK_W] = part.reshape(tm // CMP_STRIDE, CHUNK_W).astype(o_ref.dtype)
        else:
            for lo in range(0, width, 512):
                hi = min(lo + 512, width)
                o_ref[:, lo:hi] = _dot(h_scr[...], w_ref[:, off + lo:off + hi]).astype(o_ref.dtype)
        off += width


def _pack_w_in(w_in):
    sizes = (Q_W, 128, 128, 128, 128, 128, 128, NSA_HEADS * 3, SSM_INNER, SSM_CONV_DIM, SSM_HEADS, BRG_W)
    offs = [0]
    for s in sizes:
        offs.append(offs[-1] + s)
    seg = lambda i: w_in[:, offs[i]:offs[i + 1]]
    q, kc, vc, ks, vs, kw, vw, ng, z, xbc, dt, brg = (seg(i) for i in range(12))
    d = w_in.shape[0]
    pad = lambda a, w: jnp.concatenate([a, jnp.zeros((d, w - a.shape[1]), a.dtype)], axis=1)
    per_g = NSA_E * 3
    gl = jnp.concatenate([pad(ng[:, g * per_g:(g + 1) * per_g], LANE) for g in range(NSA_GROUPS)], axis=1)
    packed = jnp.concatenate([q, kc, vc, ks, vs, kw, vw, gl, z, xbc, pad(dt, DT_W), brg], axis=1)
    return packed.astype(BF16)


def _in_proj(x2, gain, w_packed, tm=512):
    n, d = x2.shape
    tm = min(tm, n)
    shapes = [(n, w) for w in SEG_WIDTHS]
    blocks = [(tm, w) for w in SEG_WIDTHS]
    shapes[KVC_SEG] = (n // CMP_STRIDE, 2 * CHUNK_W)
    blocks[KVC_SEG] = (tm // CMP_STRIDE, 2 * CHUNK_W)
    return pl.pallas_call(
        _inproj_kernel,
        grid=(n // tm,),
        in_specs=[pl.BlockSpec((tm, d), lambda i: (i, 0)),
                  _const_spec((1, d)),
                  _const_spec((d, PACKED_W))],
        out_specs=tuple(pl.BlockSpec(blk, lambda i: (i, 0)) for blk in blocks),
        out_shape=tuple(jax.ShapeDtypeStruct(shp, dt) for shp, dt in zip(shapes, SEG_DTYPES)),
        scratch_shapes=[pltpu.VMEM((tm, d), BF16)],
        compiler_params=_cparams(("parallel",)),
        name="in_proj",
    )(x2, gain.reshape(1, d), w_packed)


def _compress_kernel(x_ref, w1x_ref, w1_ref, w2_ref, w2t_ref, pos_ref, gain_ref, o_ref, ot_ref):
    kv = pl.program_id(1)
    x = x_ref[...]
    h1 = _dot(x, w1x_ref[0, 0, 0])
    h2 = _dot(x, w1x_ref[0, 0, 1])
    nrow = x.shape[0]
    h2 = pltpu.roll(h2, nrow - 1, 0)
    bias = _dot(pos_ref[0], w1_ref[0])[0:1]
    hid = _silu(h1 + h2 + bias)
    out = _dot(hid.astype(BF16), w2_ref[0])
    normed = _rms(out, gain_ref[...])
    o_ref[0, 0, 0] = jnp.where(kv == 0, normed, out).astype(o_ref.dtype)
    ot_ref[0, 0, 0] = _dot_nt(w2t_ref[0], hid.astype(BF16)).astype(ot_ref.dtype)


def _compress(kvc, pos, w1, w2, kc_gain, b, s):
    nrow = s // CMP_STRIDE
    cl = CMP_BLOCK * HD
    pos_flat = jnp.zeros((2, 8, cl), BF16).at[:, 0, :].set(pos.reshape(2, cl).astype(BF16))
    w1r = w1.reshape(2, 2, CMP_STRIDE, 1, HD, CMP_HIDDEN)
    own = jnp.eye(NSA_GROUPS, dtype=w1.dtype)
    w1x = w1r[:, None] * own[None, :, None, None, :, None, None]
    w1x = w1x.reshape(2, NSA_GROUPS, 2, CHUNK_W, CMP_HIDDEN).astype(BF16)
    return pl.pallas_call(
        _compress_kernel,
        grid=(b, 2, NSA_GROUPS),
        in_specs=[pl.BlockSpec((nrow, CHUNK_W), lambda i, k, g: (i, k)),
                  pl.BlockSpec((1, 1, 2, CHUNK_W, CMP_HIDDEN), lambda i, k, g: (k, g, 0, 0, 0)),
                  pl.BlockSpec((1, cl, CMP_HIDDEN), lambda i, k, g: (k, 0, 0)),
                  pl.BlockSpec((1, CMP_HIDDEN, HD), lambda i, k, g: (k, 0, 0)),
                  pl.BlockSpec((1, HD, CMP_HIDDEN), lambda i, k, g: (k, 0, 0)),
                  pl.BlockSpec((1, 8, cl), lambda i, k, g: (k, 0, 0)),
                  pl.BlockSpec((1, HD), lambda i, k, g: (0, 0))],
        out_specs=(pl.BlockSpec((1, 1, 1, nrow, HD), lambda i, k, g: (i, k, g, 0, 0)),
                   pl.BlockSpec((1, 1, 1, HD, nrow), lambda i, k, g: (i, k, g, 0, 0))),
        out_shape=(jax.ShapeDtypeStruct((b, 2, NSA_GROUPS, nrow, HD), BF16),
                   jax.ShapeDtypeStruct((b, 2, NSA_GROUPS, HD, nrow), BF16)),
        compiler_params=_cparams(("parallel", "parallel", "parallel")),
        name="nsa_compress",
    )(kvc, w1x, w1.astype(BF16), w2.astype(BF16), w2.transpose(0, 2, 1).astype(BF16), pos_flat,
      kc_gain.reshape(1, HD))


V_ROWS = HD + 16


def _kvprep_kernel(x_ref, gs_ref, gw_ref, ksel_ref, vsel_ref, kwin_ref, vwin_ref):
    ts = x_ref.shape[0]
    base = pl.program_id(1) * ts
    x = x_ref[...].astype(F32)
    n_hot = ksel_ref.shape[-1] - HD
    blk = (base + lax.broadcasted_iota(jnp.int32, (ts, n_hot), 0)) // SEL_BLOCK
    onehot = jnp.where(blk == lax.broadcasted_iota(jnp.int32, (ts, n_hot), 1), 1.0, 0.0)
    ones_row = jnp.where(lax.broadcasted_iota(jnp.int32, (V_ROWS - HD, ts), 0) == 0, 1.0, 0.0)
    vs_t = x[:, LANE:2 * LANE].T
    vw_t = x[:, 3 * LANE:4 * LANE].T
    for g in range(NSA_GROUPS):
        col = lambda j: x[:, j * LANE + g * HD:j * LANE + (g + 1) * HD]
        ksel_ref[0, g] = jnp.concatenate([_rms(col(0), gs_ref[...]), onehot], axis=1).astype(BF16)
        kwin_ref[0, g] = _rms(col(2), gw_ref[...]).astype(BF16)
        vsel_ref[0, g] = jnp.concatenate([vs_t[g * HD:(g + 1) * HD], ones_row], axis=0).astype(BF16)
        vwin_ref[0, g] = jnp.concatenate([vw_t[g * HD:(g + 1) * HD], ones_row], axis=0).astype(BF16)


def _kv_prep(kvsw, ks_gain, kw_gain, b, s, ts=1024):
    ts = min(ts, s)
    g = NSA_GROUPS
    aug_w = max(LANE, HD + s // SEL_BLOCK)
    spec = lambda w: pl.BlockSpec((1, g, ts, w), lambda i, j: (i, 0, j, 0))
    spec_t = pl.BlockSpec((1, g, V_ROWS, ts), lambda i, j: (i, 0, 0, j))
    shape_t = jax.ShapeDtypeStruct((b, g, V_ROWS, s), BF16)
    return pl.pallas_call(
        _kvprep_kernel,
        grid=(b, s // ts),
        in_specs=[pl.BlockSpec((ts, KVSW_W), lambda i, j: (i * (s // ts) + j, 0)),
                  pl.BlockSpec((1, HD), lambda i, j: (0, 0)),
                  pl.BlockSpec((1, HD), lambda i, j: (0, 0))],
        out_specs=(spec(aug_w), spec_t, spec(HD), spec_t),
        out_shape=(jax.ShapeDtypeStruct((b, g, s, aug_w), BF16), shape_t,
                   jax.ShapeDtypeStruct((b, g, s, HD), BF16), shape_t),
        compiler_params=_cparams(("parallel", "parallel")),
        name="nsa_kv_prep",
    )(kvsw, ks_gain.reshape(1, HD), kw_gain.reshape(1, HD))


def _softmax_pv_step(s, v_t, m_ref, acc_ref):
    m_old = m_ref[...]
    m_new = jnp.maximum(m_old, _col_max(s))
    alpha = jnp.exp2(m_old - m_new)
    p = jnp.exp2((s - m_new).astype(BF16))
    acc_ref[...] = alpha * acc_ref[...] + _dot(v_t, p)
    m_ref[...] = m_new


def _col_max(s):
    rows, lanes = s.shape
    return jnp.max(jnp.max(s.reshape(4, rows // 4, lanes), axis=0), axis=0, keepdims=True)


def _nsa_kernel(q_ref, gl_ref, kc_ref, vct_ref, ksel_ref, vsel_ref, kwin_ref, vwin_ref, ovl_ref, qg_ref,
                o_ref, qt_ref, imp_ref, m_ref, acc_ref, oc_ref, s_ref, w_ref, *, tq):
    i = pl.program_id(2)
    lanes = NSA_E * tq
    n_sel = ovl_ref.shape[0]
    scale = HD ** -0.5 * math.log2(math.e)

    q_t = q_ref[...].astype(F32).T
    for e in range(NSA_E):
        qe = q_t[e * HD:(e + 1) * HD, :]
        inv = lax.rsqrt(jnp.mean(qe * qe, axis=0, keepdims=True) + RMS_EPS)
        qt_ref[0:HD, e * tq:(e + 1) * tq] = (qe * inv * (qg_ref[...] * scale)).astype(BF16)
    q64 = qt_ref[0:HD, :]

    r_in = lax.broadcasted_iota(jnp.int32, (1, lanes), 1) & (tq - 1)
    t_lane = i * tq + r_in
    key_in = lax.broadcasted_iota(jnp.int32, (tq, 1), 0)
    tile_rows = lambda t: pl.ds(pl.multiple_of(t * tq, tq), tq)

    big = 2 * tq
    win_tiles = (i, jnp.maximum(i - 1, 0), jnp.maximum(i - 2, 0))
    win_masks = (lambda: key_in <= r_in,
                 lambda: key_in > jnp.where(i >= 1, -1, big),
                 lambda: key_in > r_in + jnp.where(i >= 2, 0, big))

    def window_scores(w):
        w_ref[w] = jnp.where(win_masks[w](), _dot(kwin_ref[0, 0, tile_rows(win_tiles[w]), :], q64), NEG)

    n_cmp = kc_ref.shape[3]
    s_c = _dot(kc_ref[0, 0, 0], q64)
    c_end = lax.broadcasted_iota(jnp.int32, (n_cmp, 1), 0) * CMP_STRIDE + (CMP_BLOCK - 1)
    s_c = jnp.where(c_end <= t_lane, s_c, NEG)
    e_c = jnp.exp2(s_c - jnp.max(s_c, axis=0, keepdims=True))
    inv_c = jnp.where(t_lane >= CMP_BLOCK - 1, 1.0 / jnp.sum(e_c, axis=0, keepdims=True), 0.0)
    p_c = e_c * inv_c
    oc_ref[...] = _dot(vct_ref[0, 0, 0], p_c.astype(BF16))

    p_sum = p_c[:, 0:tq]
    for e in range(1, NSA_E):
        p_sum = p_sum + p_c[:, e * tq:(e + 1) * tq]
    p_hi = p_sum.astype(BF16)
    p_lo = (p_sum - p_hi.astype(F32)).astype(BF16)
    imp = _dot(ovl_ref[...], p_hi) + _dot(ovl_ref[...], p_lo)
    j_idx = lax.broadcasted_iota(jnp.int32, (n_sel, tq), 0)
    cur = (i * tq + lax.broadcasted_iota(jnp.int32, (n_sel, tq), 1)) // SEL_BLOCK
    forced = (j_idx == 0) | (j_idx == cur) | (j_idx == cur - 1)
    valid = j_idx <= cur
    imp = jnp.where(valid, imp + jnp.where(forced, FORCE_BONUS, 0.0), -jnp.inf)
    imp_ref[...] = imp
    n_grp = n_sel // 8
    grp = [imp[8 * r:8 * r + 8] for r in range(n_grp)]
    j_in = lax.broadcasted_iota(jnp.int32, (8, tq), 0)
    rank = [jnp.zeros((8, tq), F32) for _ in range(n_grp)]
    for j2 in range(n_sel):
        if j2 % -(-n_sel // 3) == 0:
            window_scores(j2 // -(-n_sel // 3))
        other = imp_ref[j2:j2 + 1, :]
        for r in range(n_grp):
            if 8 * r > j2:
                before = other >= grp[r]
            elif 8 * r + 7 < j2:
                before = other > grp[r]
            else:
                before = (other > grp[r]) | ((other == grp[r]) & (j_in > j2 - 8 * r))
            rank[r] = rank[r] + jnp.where(before, 1.0, 0.0)
    sel = (jnp.concatenate(rank, axis=0) < min(TOP_N, n_sel)) & valid
    sel_bias = jnp.where(sel, 0.0, NEG).astype(BF16)
    for e in range(NSA_E):
        qt_ref[HD:HD + n_sel, e * tq:(e + 1) * tq] = sel_bias
    if HD + n_sel < qt_ref.shape[0]:
        qt_ref[HD + n_sel:, :] = jnp.zeros((qt_ref.shape[0] - HD - n_sel, lanes), BF16)

    m_w = jnp.maximum(jnp.maximum(_col_max(w_ref[0]), _col_max(w_ref[1])), _col_max(w_ref[2]))
    acc_w = _dot(vwin_ref[0, 0, :, tile_rows(win_tiles[0])], jnp.exp2((w_ref[0] - m_w).astype(BF16)))
    for w in range(1, 3):
        acc_w = acc_w + _dot(vwin_ref[0, 0, :, tile_rows(win_tiles[w])], jnp.exp2((w_ref[w] - m_w).astype(BF16)))
    o_w = acc_w[0:HD, :] * (1.0 / acc_w[HD:HD + 1, :])

    gate = jax.nn.sigmoid(gl_ref[...].T)
    for e in range(NSA_E):
        cs = slice(e * tq, (e + 1) * tq)
        oc_ref[:, cs] = gate[3 * e:3 * e + 1, :] * oc_ref[:, cs] + gate[3 * e + 2:3 * e + 3, :] * o_w[:, cs]

    qk_sel = lambda t: _dot(ksel_ref[0, 0, tile_rows(t), :], qt_ref[...])
    m_ref[...] = jnp.full(m_ref.shape, NEG, F32)
    acc_ref[...] = jnp.zeros(acc_ref.shape, F32)
    s_ref[0] = jnp.where(key_in <= r_in, qk_sel(i), NEG)

    def consume(slot, elem):
        v_tile = tile_rows(jnp.where(elem == 0, i, elem - 1))
        _softmax_pv_step(s_ref[slot], vsel_ref[0, 0, :, v_tile], m_ref, acc_ref)

    def sel_pair(j, carry):
        s_ref[1] = qk_sel(2 * j)
        consume(0, 2 * j)
        s_ref[0] = qk_sel(2 * j + 1)
        consume(1, 2 * j + 1)
        return carry

    def sel_quad(q, carry):
        sel_pair(2 * q, carry)
        return sel_pair(2 * q + 1, carry)

    lax.fori_loop(0, i // 4, sel_quad, 0)
    lax.fori_loop(i // 4 * 2, i // 2, sel_pair, 0)

    @pl.when(i % 2 == 1)
    def _():
        s_ref[1] = qk_sel(i - 1)
        consume(0, i - 1)
        consume(1, i)

    @pl.when(i % 2 == 0)
    def _():
        consume(0, i)

    o_s = acc_ref[0:HD, :] * (1.0 / acc_ref[HD:HD + 1, :])

    ys = []
    for e in range(NSA_E):
        cs = slice(e * tq, (e + 1) * tq)
        ys.append(oc_ref[:, cs] + gate[3 * e + 1:3 * e + 2, :] * o_s[:, cs])
    o_ref[...] = jnp.concatenate(ys, axis=0).T.astype(o_ref.dtype)


def _nsa_attention(q, gl, cmp_kv, cmp_t, ksel, vsel, kwin, vwin, q_gain, b, s, tq=256):
    assert WINDOW == 2 * tq and s % tq == 0
    nq = s // tq
    n_sel = s // SEL_BLOCK
    nrow = s // CMP_STRIDE
    g = NSA_GROUPS
    c0 = jnp.arange(nrow) * CMP_STRIDE
    j0 = jnp.arange(n_sel) * SEL_BLOCK
    ovl = ((c0[None, :] <= j0[:, None] + SEL_BLOCK - 1) & (c0[None, :] + CMP_BLOCK - 1 >= j0[:, None]))
    ovl = ovl.astype(BF16)
    aug_w = max(2 * HD, HD + n_sel)
    assert ksel.shape[-1] == aug_w
    lanes = NSA_E * tq
    kv_spec = lambda w: pl.BlockSpec((1, 1, s, w), lambda bi, gi, i: (bi, gi, 0, 0))
    vt_spec = pl.BlockSpec((1, 1, V_ROWS, s), lambda bi, gi, i: (bi, gi, 0, 0))
    return pl.pallas_call(
        functools.partial(_nsa_kernel, tq=tq),
        grid=(b, g, nq),
        in_specs=[pl.BlockSpec((tq, NSA_E * HD), lambda bi, gi, i: (bi * nq + i, gi)),
                  pl.BlockSpec((tq, LANE), lambda bi, gi, i: (bi * nq + i, gi)),
                  pl.BlockSpec((1, 1, 1, nrow, HD), lambda bi, gi, i: (bi, 0, gi, 0, 0)),
                  pl.BlockSpec((1, 1, 1, HD, nrow), lambda bi, gi, i: (bi, 1, gi, 0, 0)),
                  kv_spec(aug_w), vt_spec, kv_spec(HD), vt_spec,
                  pl.BlockSpec((n_sel, nrow), lambda bi, gi, i: (0, 0)),
                  pl.BlockSpec((HD, 1), lambda bi, gi, i: (0, 0))],
        out_specs=pl.BlockSpec((tq, NSA_E * HD), lambda bi, gi, i: (bi * nq + i, gi)),
        out_shape=jax.ShapeDtypeStruct((b * s, NSA_HEADS * HD), BF16),
        scratch_shapes=[pltpu.VMEM((aug_w, lanes), BF16),
                        pltpu.VMEM((n_sel, tq), F32),
                        pltpu.VMEM((1, lanes), F32),
                        pltpu.VMEM((V_ROWS, lanes), F32),
                        pltpu.VMEM((HD, lanes), F32),
                        pltpu.VMEM((2, tq, lanes), F32),
                        pltpu.VMEM((3, tq, lanes), F32)],
        compiler_params=_cparams(("parallel", "parallel", "arbitrary")),
        name="nsa_attention",
    )(q, gl, cmp_kv, cmp_t, ksel, vsel, kwin, vwin, ovl, q_gain.reshape(HD, 1))


def _softplus(x):
    return jnp.maximum(x, 0.0) + jnp.log1p(jnp.exp(-jnp.abs(x)))


def _split_bf16(x, pieces):
    out = []
    for _ in range(pieces - 1):
        hi = x.astype(BF16)
        out.append(hi)
        x = x - hi.astype(F32)
    out.append(x.astype(BF16))
    return out


def _ssd_kernel(z_ref, xbc_ref, dt_ref, dtT_ref, cw_ref, cb_ref, dtb_ref, dtbT_ref, alog_ref, alogT_ref,
                dskip_ref, ng_ref, exp_ref, shift_ref, o_ref, xpad_ref, state_ref, y_ref):
    c = pl.program_id(1)
    L = xbc_ref.shape[0]
    gw = SSM_INNER // SSM_GROUPS
    e_per_g = SSM_HEADS // SSM_GROUPS

    @pl.when(c == 0)
    def _():
        xpad_ref[0:8, :] = jnp.zeros((8, SSM_CONV_DIM), F32)
        state_ref[...] = jnp.zeros(state_ref.shape, F32)

    xb = xbc_ref[...]
    xf = xb.astype(F32)
    shifted = _dot(shift_ref[...], xb)
    acc = cb_ref[...] + cw_ref[SSM_CONV - 1:SSM_CONV, :] * xf
    for k in range(SSM_CONV - 1):
        acc = acc + cw_ref[k:k + 1, :] * shifted[k * L:(k + 1) * L]
    xpad_ref[8:16, :] = xf[0:8]
    head = cb_ref[...] + cw_ref[0:1, :] * xpad_ref[5:13, :]
    for k in range(1, SSM_CONV):
        head = head + cw_ref[k:k + 1, :] * xpad_ref[5 + k:13 + k, :]
    xpad_ref[0:8, :] = xf[L - 8:L]
    xact = _silu(jnp.concatenate([head, acc[8:]], axis=0))
    xs = xact[:, :SSM_INNER]

    li = lax.broadcasted_iota(jnp.int32, (L, L), 0)
    si = lax.broadcasted_iota(jnp.int32, (L, L), 1)
    causal = li >= si
    log2e = math.log2(math.e)
    dt = _softplus(dt_ref[...] + dtb_ref[...])
    da = jnp.concatenate(_split_bf16(dt * (-log2e * jnp.exp(alog_ref[...])), 3), axis=1)
    r = _dot(jnp.where(causal, 1.0, 0.0).astype(BF16), da)
    a_cs = r[:, 0:LANE] + r[:, LANE:2 * LANE] + r[:, 2 * LANE:]
    dtT = _softplus(dtT_ref[0] + dtbT_ref[...])
    daT = jnp.concatenate(_split_bf16(dtT * (-log2e * jnp.exp(alogT_ref[...])), 3), axis=0)
    rT = _dot(daT, jnp.where(li <= si, 1.0, 0.0).astype(BF16))
    nh = SSM_HEADS
    a_csT = rT[0:nh] + rT[nh:2 * nh] + rT[2 * nh:]

    widen = lambda v: _dot(jnp.concatenate(_split_bf16(v, 2), axis=1), exp_ref[...])
    ea_x = widen(jnp.exp2(a_cs))
    din_x = widen(jnp.exp2(a_cs[L - 1:L, :] - a_cs))
    xd = xs * widen(dt)
    xdd = (xd * din_x).astype(BF16)
    xd = xd.astype(BF16)

    half = L // 2
    first_head = lax.broadcasted_iota(jnp.int32, (L, 2 * SSM_P), 1) < SSM_P
    for g in range(SSM_GROUPS):
        gs = slice(g * gw, (g + 1) * gw)
        bg = xact[:, SSM_INNER + g * SSM_N:SSM_INNER + (g + 1) * SSM_N].astype(BF16)
        cg = xact[:, SSM_INNER + (SSM_GROUPS + g) * SSM_N:SSM_INNER + (SSM_GROUPS + g + 1) * SSM_N].astype(BF16)
        cb = _dot_nt(cg, bg)
        st = state_ref[g]
        y_ref[:, gs] = _dot(cg, st.astype(BF16)) * ea_x[:, gs] + xs[:, gs] * dskip_ref[:, gs]
        for pair in range(e_per_g // 2):
            ps = slice(g * gw + pair * 2 * SSM_P, g * gw + (pair + 1) * 2 * SSM_P)
            y_top = jnp.zeros((half, 2 * SSM_P), F32)
            y_bot = jnp.zeros((half, 2 * SSM_P), F32)
            for sub in range(2):
                h = g * e_per_g + 2 * pair + sub
                xh = jnp.where(first_head if sub == 0 else ~first_head, xd[:, ps], jnp.zeros_like(xd[:, ps]))
                diff = a_cs[:, h:h + 1] - a_csT[h:h + 1, :]
                seg_t = jnp.exp2(jnp.where(causal[:half, :half], diff[:half, :half], NEG))
                seg_b = jnp.exp2(jnp.where(causal[half:, :], diff[half:, :], NEG))
                y_top = y_top + _dot((cb[:half, :half] * seg_t).astype(BF16), xh[:half])
                y_bot = y_bot + _dot((cb[half:, :] * seg_b).astype(BF16), xh)
            y_ref[:half, ps] += y_top
            y_ref[half:, ps] += y_bot
        upd = lax.dot_general(bg, xdd[:, gs], (((0,), (0,)), ((), ())), preferred_element_type=F32)
        state_ref[g] = st * ea_x[L - 1:L, gs] + upd

    y = y_ref[...] * _silu(z_ref[...].astype(F32))
    for g in range(SSM_GROUPS):
        gs = slice(g * gw, (g + 1) * gw)
        o_ref[:, gs] = _rms(y[:, gs], ng_ref[:, gs]).astype(o_ref.dtype)


def _ssd_mixer(z, xbc, dt, conv_w, conv_b, dt_bias, a_log, d_skip, norm_gain, b, s):
    L = math.gcd(SSM_CHUNK, s)
    nc = s // L
    h = SSM_HEADS
    dtT = dt[:, :h].reshape(b, s, h).transpose(0, 2, 1)
    pad_l = lambda v: jnp.zeros((1, LANE), F32).at[0, :h].set(v)
    expand = jnp.zeros((LANE, SSM_INNER), F32).at[:h].set(jnp.repeat(jnp.eye(h, dtype=F32), SSM_P, axis=1))
    expand = jnp.concatenate([expand, expand], axis=0).astype(BF16)
    t_out = jnp.arange(L)[:, None]
    t_in = jnp.arange(L)[None, :]
    shift = jnp.concatenate([(t_in == t_out - (SSM_CONV - 1 - k)) for k in range(SSM_CONV - 1)], axis=0).astype(BF16)
    row = lambda w: pl.BlockSpec((L, w), lambda bi, ci: (bi * nc + ci, 0))
    full = lambda shp: pl.BlockSpec(shp, lambda bi, ci: (0,) * len(shp))
    return pl.pallas_call(
        _ssd_kernel,
        grid=(b, nc),
        in_specs=[row(SSM_INNER), row(SSM_CONV_DIM), row(LANE),
                  pl.BlockSpec((1, h, L), lambda bi, ci: (bi, 0, ci)),
                  full((SSM_CONV, SSM_CONV_DIM)), full((1, SSM_CONV_DIM)),
                  full((1, LANE)), full((h, 1)), full((1, LANE)), full((h, 1)),
                  full((1, SSM_INNER)), full((1, SSM_INNER)), full((2 * LANE, SSM_INNER)),
                  full(((SSM_CONV - 1) * L, L))],
        out_specs=row(SSM_INNER),
        out_shape=jax.ShapeDtypeStruct((b * s, SSM_INNER), BF16),
        scratch_shapes=[pltpu.VMEM((16, SSM_CONV_DIM), F32),
                        pltpu.VMEM((SSM_GROUPS, SSM_N, SSM_INNER // SSM_GROUPS), F32),
                        pltpu.VMEM((L, SSM_INNER), F32)],
        compiler_params=_cparams(("parallel", "arbitrary")),
        name="ssd_mixer",
    )(z, xbc, dt, dtT, conv_w, conv_b.reshape(1, -1), pad_l(dt_bias), dt_bias.reshape(h, 1),
      pad_l(a_log), a_log.reshape(h, 1), jnp.repeat(d_skip, SSM_P).reshape(1, -1),
      norm_gain.reshape(1, -1), expand, shift)


ROUTE_ROWS = 8 + MOE_EXPERTS


def _merge_kernel(x_ref, ya_ref, yb_ref, brg_ref, wa_ref, wb_ref, wo_ref, mg_ref, wr_ref, br_ref,
                  x1_ref, h2_ref, eid_ref, ew_ref):
    tm = x_ref.shape[0]
    gate = jax.nn.sigmoid(brg_ref[...].astype(F32))
    merged = gate[:, :D_MODEL] * _dot(ya_ref[...], wa_ref[...]) + gate[:, D_MODEL:] * _dot(yb_ref[...], wb_ref[...])
    x1 = x_ref[...] + _dot(merged.astype(BF16), wo_ref[...])
    x1_ref[...] = x1
    h2 = _rms(x1, mg_ref[...])
    _rows_to_tiles(h2, h2_ref)

    lg = _dot_nt(wr_ref[...], h2.astype(BF16)) + br_ref[...]
    r = lax.broadcasted_iota(jnp.int32, (ROUTE_ROWS, tm), 0)
    big = ROUTE_ROWS
    is_g = r < MOE_GROUPS
    gmax = jnp.max(jnp.where(is_g, lg, -jnp.inf), axis=0, keepdims=True)
    gsum = jnp.sum(jnp.where(is_g, jnp.exp(lg - gmax), 0.0), axis=0, keepdims=True)
    grp_w = 1.0 / gsum
    grp_i = jnp.min(jnp.where(is_g & (lg == gmax), r, big), axis=0, keepdims=True)
    is_e = (r >= 8) & (((r - 8) // MOE_EPG) == grp_i)
    le = jnp.where(is_e, lg, -jnp.inf)
    emax = jnp.max(le, axis=0, keepdims=True)
    esum = jnp.sum(jnp.where(is_e, jnp.exp(lg - emax), 0.0), axis=0, keepdims=True)
    i1 = jnp.min(jnp.where(le == emax, r, big), axis=0, keepdims=True)
    le2 = jnp.where(r == i1, -jnp.inf, le)
    e2max = jnp.max(le2, axis=0, keepdims=True)
    i2 = jnp.min(jnp.where((le2 == e2max) & is_e & (r != i1), r, big), axis=0, keepdims=True)
    p1 = 1.0 / esum
    p2 = jnp.exp(e2max - emax) / esum
    w1 = grp_w * p1 / (p1 + p2)
    w2 = grp_w * p2 / (p1 + p2)
    r8 = lax.broadcasted_iota(jnp.int32, (8, tm), 0)
    eid_ref[...] = jnp.where(r8 == 0, i1 - 8, jnp.where(r8 == 1, i2 - 8, 0))
    ew_ref[...] = jnp.where(r8 == 0, w1, jnp.where(r8 == 1, w2, 0.0))


def _merge_route(x2, y_a, y_b, brg, w_a, w_b, w_o, moe_gain, w_group, b_group, w_expert, b_expert, tm=512):
    n, d = x2.shape
    tm = min(tm, n)
    wr = jnp.zeros((ROUTE_ROWS, d), F32).at[:MOE_GROUPS].set(w_group.T).at[8:].set(w_expert.T).astype(BF16)
    br = jnp.zeros((ROUTE_ROWS, 1), F32).at[:MOE_GROUPS, 0].set(b_group).at[8:, 0].set(b_expert)
    row = lambda w: pl.BlockSpec((tm, w), lambda i: (i, 0))
    colblk = pl.BlockSpec((8, tm), lambda i: (0, i))
    return pl.pallas_call(
        _merge_kernel,
        grid=(n // tm,),
        in_specs=[row(d), row(Q_W), row(SSM_INNER), row(BRG_W),
                  _const_spec((Q_W, d)), _const_spec((SSM_INNER, d)), _const_spec((d, d)),
                  _const_spec((1, d)), _const_spec((ROUTE_ROWS, d)), _const_spec((ROUTE_ROWS, 1))],
        out_specs=(row(d), pl.BlockSpec((tm, ROW_SUB, LANE), lambda i: (i, 0, 0)), colblk, colblk),
        out_shape=(jax.ShapeDtypeStruct((n, d), F32), jax.ShapeDtypeStruct((n, ROW_SUB, LANE), ROW_DTYPE),
                   jax.ShapeDtypeStruct((8, n), jnp.int32), jax.ShapeDtypeStruct((8, n), F32)),
        compiler_params=_cparams(("parallel",)),
        name="merge_route",
    )(x2, y_a, y_b, brg, w_a.astype(BF16), w_b.astype(BF16), w_o.astype(BF16), moe_gain.reshape(1, d), wr, br)


def _rank_kernel(eid_ref, rank_ref, cnt_ref, run_ref, tri_ref):
    i = pl.program_id(0)
    tm = eid_ref.shape[1]

    @pl.when(i == 0)
    def _():
        run_ref[...] = jnp.zeros(run_ref.shape, F32)
        earlier = lax.broadcasted_iota(jnp.int32, (tm, tm), 0) < lax.broadcasted_iota(jnp.int32, (tm, tm), 1)
        tri_ref[...] = jnp.where(earlier, 1.0, 0.0).astype(BF16)

    r = lax.broadcasted_iota(jnp.int32, (MOE_EXPERTS, tm), 0)
    oh0 = r == eid_ref[0:1, :]
    oh1 = r == eid_ref[1:2, :]
    oh = jnp.where(oh0 | oh1, 1.0, 0.0)
    prefix = _dot(oh.astype(BF16), tri_ref[...]) + run_ref[:, 0:1]
    rank0 = jnp.sum(jnp.where(oh0, prefix, 0.0), axis=0, keepdims=True)
    rank1 = jnp.sum(jnp.where(oh1, prefix, 0.0), axis=0, keepdims=True)
    r8 = lax.broadcasted_iota(jnp.int32, (8, tm), 0)
    rank_ref[...] = jnp.where(r8 == 0, rank0, jnp.where(r8 == 1, rank1, 0.0)).astype(jnp.int32)
    run_ref[...] = run_ref[...] + jnp.sum(oh, axis=1, keepdims=True)
    cnt_ref[...] = run_ref[...]


def _moe_rank(eid, tm=512):
    n = eid.shape[1]
    tm = min(tm, n)
    return pl.pallas_call(
        _rank_kernel,
        grid=(n // tm,),
        in_specs=[pl.BlockSpec((8, tm), lambda i: (0, i))],
        out_specs=(pl.BlockSpec((8, tm), lambda i: (0, i)),
                   pl.BlockSpec((MOE_EXPERTS, LANE), lambda i: (0, 0))),
        out_shape=(jax.ShapeDtypeStruct((8, n), jnp.int32),
                   jax.ShapeDtypeStruct((MOE_EXPERTS, LANE), F32)),
        scratch_shapes=[pltpu.VMEM((MOE_EXPERTS, LANE), F32), pltpu.VMEM((tm, tm), BF16)],
        compiler_params=_cparams(("arbitrary",)),
        name="moe_rank",
    )(eid)


def _row_copy(src, dst, sem):
    return pltpu.make_async_copy(src, dst, sem)


ROW_SUB = D_MODEL // LANE
ROW_DTYPE = BF16
ROW_UNROLL = 8


def _rows_to_tiles(x, ref):
    ref[...] = x.astype(ref.dtype).reshape(ref.shape)


def _tiles_to_rows(ref):
    x = ref[...]
    return x.reshape(x.shape[0], ROW_SUB * LANE)


def _dispatch_kernel(slot_ref, h_ref, xs_in_ref, xs_ref, stage_ref, sems):
    del xs_in_ref
    i = pl.program_id(0)
    tm = h_ref.shape[0]
    buf = i % 2
    stage_ref[buf] = h_ref[...]

    def issue(t, carry):
        for u in range(ROW_UNROLL):
            n = t * ROW_UNROLL + u
            for k in range(MOE_TOP_K):
                _row_copy(stage_ref.at[buf, n], xs_ref.at[slot_ref[0, MOE_TOP_K * n + k]], sems.at[buf]).start(priority=k)
        return carry

    lax.fori_loop(0, tm // ROW_UNROLL, issue, 0)

    def drain(b):
        def body(t, carry):
            for _ in range(ROW_UNROLL * MOE_TOP_K):
                _row_copy(stage_ref.at[b, 0], xs_ref.at[0], sems.at[b]).wait()
            return carry

        lax.fori_loop(0, tm // ROW_UNROLL, body, 0)

    @pl.when(i > 0)
    def _():
        drain(1 - buf)

    @pl.when(i == pl.num_programs(0) - 1)
    def _():
        drain(buf)


def _moe_dispatch(slot, h2, n_slots, tm=512):
    n = h2.shape[0]
    tm = min(tm, n)
    xs0 = jnp.zeros((n_slots, ROW_SUB, LANE), h2.dtype)
    return pl.pallas_call(
        _dispatch_kernel,
        grid=(n // tm,),
        in_specs=[pl.BlockSpec((1, MOE_TOP_K * tm), lambda i: (0, i), memory_space=pltpu.SMEM),
                  pl.BlockSpec((tm, ROW_SUB, LANE), lambda i: (i, 0, 0)),
                  pl.BlockSpec(memory_space=pl.ANY)],
        out_specs=pl.BlockSpec(memory_space=pl.ANY),
        out_shape=jax.ShapeDtypeStruct(xs0.shape, h2.dtype),
        scratch_shapes=[pltpu.VMEM((2, tm, ROW_SUB, LANE), h2.dtype), pltpu.SemaphoreType.DMA((2,))],
        input_output_aliases={2: 0},
        compiler_params=_cparams(("arbitrary",)),
        name="moe_dispatch",
    )(slot, h2, xs0)


def _expert_kernel(be_ref, nu_ref, xs_ref, wg_ref, wu_ref, wd_ref, o_ref):
    j = pl.program_id(0)

    @pl.when(j < nu_ref[0])
    def _():
        x = _tiles_to_rows(xs_ref).astype(BF16)
        act = _silu(_dot(x, wg_ref[0])) * _dot(x, wu_ref[0])
        _rows_to_tiles(_dot(act.astype(BF16), wd_ref[0]), o_ref)

    @pl.when(j >= nu_ref[0])
    def _():
        o_ref[...] = jnp.zeros(o_ref.shape, o_ref.dtype)


def _moe_experts(block_e, n_used, xs, w_gate, w_up, w_down):
    n_slots = xs.shape[0]
    d = D_MODEL
    rb = MOE_ROW_BLOCK
    wspec = lambda shp: pl.BlockSpec((1,) + shp, lambda j, be, nu: (be[j], 0, 0))
    rows = pl.BlockSpec((rb, ROW_SUB, LANE), lambda j, be, nu: (j, 0, 0))
    return pl.pallas_call(
        _expert_kernel,
        grid_spec=pltpu.PrefetchScalarGridSpec(
            num_scalar_prefetch=2,
            grid=(n_slots // rb,),
            in_specs=[rows, wspec((d, MOE_HIDDEN)), wspec((d, MOE_HIDDEN)), wspec((MOE_HIDDEN, d))],
            out_specs=rows),
        out_shape=jax.ShapeDtypeStruct(xs.shape, ROW_DTYPE),
        compiler_params=_cparams(("arbitrary",)),
        name="moe_experts",
    )(block_e, n_used, xs, w_gate.astype(BF16), w_up.astype(BF16), w_down.astype(BF16))


def _combine_kernel(slot_ref, slot_next_ref, ew_ref, x1_ref, p_ref, eo_ref, pg_ref, wg_ref, wp_ref, o_ref,
                    rows_ref, sems):
    i = pl.program_id(0)
    n_steps = pl.num_programs(0)
    tm = x1_ref.shape[0]

    def gather(idx_ref, buf):
        def issue(t, carry):
            for u in range(ROW_UNROLL):
                n = t * ROW_UNROLL + u
                for k in range(MOE_TOP_K):
                    _row_copy(eo_ref.at[idx_ref[0, MOE_TOP_K * n + k]], rows_ref.at[buf, k, n], sems.at[buf]).start(priority=k)
            return carry

        lax.fori_loop(0, tm // ROW_UNROLL, issue, 0)

    @pl.when(i == 0)
    def _():
        gather(slot_ref, 0)

    @pl.when(i + 1 < n_steps)
    def _():
        gather(slot_next_ref, (i + 1) % 2)

    buf = i % 2

    def drain(t, carry):
        for _ in range(ROW_UNROLL * MOE_TOP_K):
            _row_copy(eo_ref.at[0], rows_ref.at[buf, 0, 0], sems.at[buf]).wait()
        return carry

    lax.fori_loop(0, tm // ROW_UNROLL, drain, 0)

    rows = lambda k: _tiles_to_rows(rows_ref.at[buf, k]).astype(F32)
    x2 = x1_ref[...] + ew_ref[:, 0:1] * rows(0) + ew_ref[:, 1:2] * rows(1)
    h3 = _rms(x2, pg_ref[...]).astype(BF16)
    o_ref[...] = x2 + jax.nn.sigmoid(_dot(h3, wg_ref[...])) * _dot(p_ref[...].astype(BF16), wp_ref[...])


def _moe_combine_ple(slot, ew_cols, x1, p2, expert_out, ple_gain, w_gate, w_proj, tm=512):
    n, d = x1.shape
    tm = min(tm, n)
    row = lambda w: pl.BlockSpec((tm, w), lambda i: (i, 0))
    last = n // tm - 1
    idx = lambda fn: pl.BlockSpec((1, MOE_TOP_K * tm), fn, memory_space=pltpu.SMEM)
    return pl.pallas_call(
        _combine_kernel,
        grid=(n // tm,),
        in_specs=[idx(lambda i: (0, i)), idx(lambda i: (0, jnp.minimum(i + 1, last))),
                  row(8), row(d), row(PLE_DIM),
                  pl.BlockSpec(memory_space=pl.ANY),
                  _const_spec((1, d)), _const_spec((d, d)), _const_spec((PLE_DIM, d))],
        out_specs=row(d),
        out_shape=jax.ShapeDtypeStruct((n, d), F32),
        scratch_shapes=[pltpu.VMEM((2, MOE_TOP_K, tm, ROW_SUB, LANE), ROW_DTYPE),
                        pltpu.SemaphoreType.DMA((2,))],
        compiler_params=_cparams(("arbitrary",)),
        name="moe_combine_ple",
    )(slot, slot, ew_cols, x1, p2, expert_out, ple_gain.reshape(1, d), w_gate.astype(BF16), w_proj.astype(BF16))


def _layer(x2, p2, b, s, mix_norm, w_in, nsa_q_gain, nsa_kc_gain, nsa_ks_gain, nsa_kw_gain,
           cmp_pos_k, cmp_w1_k, cmp_w2_k, cmp_pos_v, cmp_w1_v, cmp_w2_v,
           ssm_conv_w, ssm_conv_b, ssm_dt_bias, ssm_a_log, ssm_d, ssm_norm,
           w_branch_a, w_branch_b, w_out,
           moe_norm, moe_w_group, moe_b_group, moe_w_expert, moe_b_expert,
           moe_w_gate, moe_w_up, moe_w_down, ple_norm, ple_w_gate, ple_w_proj):
    n = b * s
    q, kvc, kvsw, gl, z, xbc, dt, brg = _in_proj(x2, mix_norm, _pack_w_in(w_in))

    cmp_kv, cmp_t = _compress(kvc, jnp.stack([cmp_pos_k, cmp_pos_v]), jnp.stack([cmp_w1_k, cmp_w1_v]),
                              jnp.stack([cmp_w2_k, cmp_w2_v]), nsa_kc_gain, b, s)
    ksel, vsel, kwin, vwin = _kv_prep(kvsw, nsa_ks_gain, nsa_kw_gain, b, s)
    y_a = _nsa_attention(q, gl, cmp_kv, cmp_t, ksel, vsel, kwin, vwin, nsa_q_gain, b, s)
    y_b = _ssd_mixer(z, xbc, dt, ssm_conv_w, ssm_conv_b, ssm_dt_bias, ssm_a_log, ssm_d, ssm_norm, b, s)

    x1, h2, eid, ew = _merge_route(x2, y_a, y_b, brg, w_branch_a, w_branch_b, w_out, moe_norm,
                                   moe_w_group, moe_b_group, moe_w_expert, moe_b_expert)

    rank, cnt = _moe_rank(eid)
    rb = MOE_ROW_BLOCK
    counts = cnt[:, 0].astype(jnp.int32)
    padded = (counts + rb - 1) // rb * rb
    pend = jnp.cumsum(padded)
    pstart = pend - padded
    e_ids = jnp.arange(MOE_EXPERTS, dtype=jnp.int32)
    is_e = eid[None, :MOE_TOP_K] == e_ids[:, None, None]
    slot = rank[:MOE_TOP_K] + jnp.sum(jnp.where(is_e, pstart[:, None, None], 0), axis=0)
    n_blocks = -(-(n * MOE_TOP_K) // rb) + MOE_EXPERTS
    first_row = jnp.arange(n_blocks, dtype=jnp.int32) * rb
    block_e = jnp.minimum(jnp.sum(pend[None, :] <= first_row[:, None], axis=1), MOE_EXPERTS - 1)
    n_used = (pend[-1:] // rb).astype(jnp.int32)

    slot = slot.T.reshape(1, n * MOE_TOP_K)
    xs = _moe_dispatch(slot, h2, n_blocks * rb)
    expert_out = _moe_experts(block_e.astype(jnp.int32), n_used, xs, moe_w_gate, moe_w_up, moe_w_down)
    ew_cols = ew.T
    return _moe_combine_ple(slot, ew_cols, x1, p2, expert_out, ple_norm, ple_w_gate, ple_w_proj)


def kernel(x, p, mix_norm, w_in, nsa_q_gain, nsa_kc_gain, nsa_ks_gain, nsa_kw_gain, cmp_pos_k, cmp_w1_k, cmp_w2_k, cmp_pos_v, cmp_w1_v, cmp_w2_v, ssm_conv_w, ssm_conv_b, ssm_dt_bias, ssm_a_log, ssm_d, ssm_norm, w_branch_a, w_branch_b, w_out, moe_norm, moe_w_group, moe_b_group, moe_w_expert, moe_b_expert, moe_w_gate, moe_w_up, moe_w_down, ple_norm, ple_w_gate, ple_w_proj):
    b, s, d = x.shape
    params = (mix_norm, w_in, nsa_q_gain, nsa_kc_gain, nsa_ks_gain, nsa_kw_gain,
              cmp_pos_k, cmp_w1_k, cmp_w2_k, cmp_pos_v, cmp_w1_v, cmp_w2_v,
              ssm_conv_w, ssm_conv_b, ssm_dt_bias, ssm_a_log, ssm_d, ssm_norm,
              w_branch_a, w_branch_b, w_out,
              moe_norm, moe_w_group, moe_b_group, moe_w_expert, moe_b_expert,
              moe_w_gate, moe_w_up, moe_w_down, ple_norm, ple_w_gate, ple_w_proj)
    x2 = x.reshape(b * s, d)
    for i in range(p.shape[0]):
        x2 = _layer(x2, p[i].reshape(b * s, -1), b, s, *(w[i] for w in params))
    return x2.reshape(b, s, d)
```
